```python
import jax, jax.numpy as jnp
from jax import lax
import numpy as np

D_MODEL = 4096
BATCH = 1
SEQ = 16384
DEPTH = 2

HEAD_DIM = 128
BLK = 128
EPS = 1e-5
NEG = -1e30

H_SB = 8
H_SW = 24
KV_SW = 3
SW_WINDOW = 128
AB_IN_WIDTH = 3 * H_SB * HEAD_DIM + (H_SW + 2 * KV_SW) * HEAD_DIM
AB_OUT_WIDTH = (H_SB + H_SW) * HEAD_DIM

H_NSA = 32
KV_NSA = 2
NSA_CMP_LEN = 32
NSA_CMP_STRIDE = 16
NSA_CMP_HID = 128
NSA_SEL_LEN = 64
NSA_TOP_N = 8
NSA_WINDOW = 512
NSA_FORCE = 1e4
NSA_IN_WIDTH = H_NSA * HEAD_DIM + 6 * KV_NSA * HEAD_DIM + 3 * H_NSA
NSA_OUT_WIDTH = H_NSA * HEAD_DIM

D_FF = ((8 * D_MODEL + 3 * 256 - 1) // (3 * 256)) * 256

N_EVEN = (DEPTH + 1) // 2
N_ODD = DEPTH // 2

kernel_name = "hybrid_stickbreak_swa_sink_nsa_swiglu"


def rmsnorm(x, g):
    xf = x.astype(jnp.float32)
    y = xf * lax.rsqrt(jnp.mean(xf * xf, axis=-1, keepdims=True) + EPS)
    return (y * g.astype(jnp.float32)).astype(x.dtype)


def split_cols(z, widths):
    outs, off = [], 0
    for w in widths:
        outs.append(z[..., off:off + w])
        off += w
    return outs


def alibi_slopes(n_heads, n_kv):
    s = jnp.exp2(-8.0 * jnp.arange(1, n_heads + 1, dtype=jnp.float32) / n_heads)
    return s.reshape(n_heads // n_kv, n_kv).T


def masked_softmax(s, mask):
    s = jnp.where(mask, s, NEG)
    m = jnp.max(s, axis=-1, keepdims=True)
    e = jnp.where(mask, jnp.exp(s - m), 0.0)
    return e / jnp.maximum(jnp.sum(e, axis=-1, keepdims=True), 1e-30)


def map_query_blocks(fn, q):
    B, S = q.shape[:2]
    nb = S // BLK
    qb = jnp.moveaxis(q.reshape((B, nb, BLK) + q.shape[2:]), 1, 0)
    out = lax.map(lambda a: fn(a[0], a[1]), (jnp.arange(nb), qb))
    return jnp.moveaxis(out, 0, 1).reshape((B, S) + out.shape[3:])


def stick_breaking_attention(q, k, v):
    B, S, H, d = q.shape
    nb = S // BLK
    scale = HEAD_DIM ** -0.5
    j_in = jnp.arange(BLK)
    upper = (j_in[:, None] >= j_in[None, :]).astype(jnp.float32)
    outs = []
    for i in range(nb):
        n_k = (i + 1) * BLK
        qb = q[:, i * BLK:n_k]
        kb, vb = k[:, :n_k], v[:, :n_k]
        z = jnp.einsum("bqhd,bkhd->bhqk", qb, kb).astype(jnp.float32) * scale
        t = i * BLK + jnp.arange(BLK)
        before = jnp.arange(n_k)[None, :] < t[:, None]
        a = jnp.where(before, jax.nn.log_sigmoid(-z), 0.0).reshape(B, H, BLK, i + 1, BLK)
        within = jnp.einsum("bhqnj,js->bhqns", a, upper)
        tot = jnp.sum(a, axis=-1)
        later = lax.cumsum(tot, axis=3, reverse=True) - tot
        r = (within + later[..., None]).reshape(B, H, BLK, n_k)
        w = jnp.where(before, jnp.exp(z + r), 0.0)
        outs.append(jnp.einsum("bhqk,bkhd->bqhd", w.astype(vb.dtype), vb))
    return jnp.concatenate(outs, axis=1)


def window_attention(q, k, v, window, slopes, sinks=None):
    n_prev = -(-(window - 1) // BLK)
    pad = n_prev * BLK
    span = pad + BLK
    kp = jnp.pad(k, ((0, 0), (pad, 0), (0, 0), (0, 0)))
    vp = jnp.pad(v, ((0, 0), (pad, 0), (0, 0), (0, 0)))
    scale = HEAD_DIM ** -0.5

    def block(i, qb):
        start = i * BLK
        kb = lax.dynamic_slice_in_dim(kp, start, span, axis=1)
        vb = lax.dynamic_slice_in_dim(vp, start, span, axis=1)
        t = start + jnp.arange(BLK)
        s_pos = start - pad + jnp.arange(span)
        dist = t[:, None] - s_pos[None, :]
        mask = (dist >= 0) & (dist < window) & (s_pos[None, :] >= 0)
        s = (jnp.einsum("bqhgd,bkhd->bhgqk", qb, kb).astype(jnp.float32) * scale
             - slopes[:, :, None, None] * dist.astype(jnp.float32))
        s = jnp.where(mask, s, NEG)
        if sinks is None:
            p = jax.nn.softmax(s, axis=-1)
        else:
            sink = jnp.broadcast_to(sinks.astype(jnp.float32)[None, :, :, None, None], s.shape[:-1] + (1,))
            p = jax.nn.softmax(jnp.concatenate([s, sink], axis=-1), axis=-1)[..., :-1]
        return jnp.einsum("bhgqk,bkhd->bqhgd", p.astype(vb.dtype), vb)

    return map_query_blocks(block, q)


def compress_blocks(x, pos, w1, w2):
    B, S, Hkv, d = x.shape
    chunks = x.reshape(B, S // NSA_CMP_STRIDE, NSA_CMP_STRIDE, Hkv, d)
    blocks = jnp.concatenate([chunks[:, :-1], chunks[:, 1:]], axis=2) + pos[None, None, :, None, :]
    flat = jnp.swapaxes(blocks, 2, 3).reshape(B, -1, Hkv, NSA_CMP_LEN * d)
    return jax.nn.gelu(flat @ w1) @ w2


def nsa_attention(q, k_cmp, v_cmp, k_slc, v_slc, k_win, v_win, gates, slopes):
    B, S, Hkv, G, d = q.shape
    n_cmp = k_cmp.shape[1]
    n_slc = S // NSA_SEL_LEN
    top_n = min(NSA_TOP_N, n_slc)
    r = NSA_SEL_LEN // NSA_CMP_STRIDE
    scale = HEAD_DIM ** -0.5
    cmp_end = jnp.arange(n_cmp) * NSA_CMP_STRIDE + NSA_CMP_LEN - 1
    ks_blocks = k_slc.reshape(B, n_slc, NSA_SEL_LEN, Hkv, d).transpose(0, 3, 1, 2, 4)
    vs_blocks = v_slc.reshape(B, n_slc, NSA_SEL_LEN, Hkv, d).transpose(0, 3, 1, 2, 4)
    b_idx = jnp.arange(B)[:, None, None, None]
    h_idx = jnp.arange(Hkv)[None, :, None, None]
    blk = jnp.arange(n_slc)
    tok = jnp.arange(NSA_SEL_LEN)

    def block(i, qb):
        t = i * BLK + jnp.arange(BLK)
        dist_c = (t[:, None] - cmp_end[None, :]).astype(jnp.float32)
        s_c = (jnp.einsum("bqhgd,bchd->bhgqc", qb, k_cmp).astype(jnp.float32) * scale
               - slopes[:, :, None, None] * dist_c)
        p_c = masked_softmax(s_c, dist_c >= 0)
        o_cmp = jnp.einsum("bhgqc,bchd->bqhgd", p_c.astype(v_cmp.dtype), v_cmp)
        imp = jnp.pad(jnp.sum(p_c, axis=2), ((0, 0), (0, 0), (0, 0), (1, 1)))
        imp = imp[..., :r * n_slc].reshape(B, Hkv, BLK, n_slc, r).sum(-1) + imp[..., r::r]
        cur = t // NSA_SEL_LEN
        valid = blk[None, :] <= cur[:, None]
        forced = (blk[None, :] == 0) | (blk[None, :] == cur[:, None]) | (blk[None, :] == cur[:, None] - 1)
        score = jnp.where(valid, imp + jnp.where(forced, NSA_FORCE, 0.0), NEG)
        _, idx = lax.top_k(score, top_n)
        k_sel = ks_blocks[b_idx, h_idx, idx].reshape(B, Hkv, BLK, top_n * NSA_SEL_LEN, d)
        v_sel = vs_blocks[b_idx, h_idx, idx].reshape(B, Hkv, BLK, top_n * NSA_SEL_LEN, d)
        pos_sel = (idx[..., None] * NSA_SEL_LEN + tok).reshape(B, Hkv, BLK, top_n * NSA_SEL_LEN)
        dist_s = t[None, None, :, None] - pos_sel
        s_s = (jnp.einsum("bqhgd,bhqnd->bhgqn", qb, k_sel).astype(jnp.float32) * scale
               - slopes[:, :, None, None] * dist_s[:, :, None].astype(jnp.float32))
        p_s = masked_softmax(s_s, (dist_s >= 0)[:, :, None])
        o_slc = jnp.einsum("bhgqn,bhqnd->bqhgd", p_s.astype(v_sel.dtype), v_sel)
        return jnp.stack([o_cmp, o_slc], axis=-1)

    o = map_query_blocks(block, q)
    o_win = window_attention(q, k_win, v_win, NSA_WINDOW, slopes)
    return gates[..., 0:1] * o[..., 0] + gates[..., 1:2] * o[..., 1] + gates[..., 2:3] * o_win


def ab_mixer(h, w_in, sinks, w_out):
    B, S, _ = h.shape
    d_sb, d_swq, d_swkv = H_SB * HEAD_DIM, H_SW * HEAD_DIM, KV_SW * HEAD_DIM
    q_a, k_a, v_a, q_b, k_b, v_b = split_cols(h @ w_in, [d_sb, d_sb, d_sb, d_swq, d_swkv, d_swkv])
    g_sw = H_SW // KV_SW
    o_a = stick_breaking_attention(q_a.reshape(B, S, H_SB, HEAD_DIM),
                                   k_a.reshape(B, S, H_SB, HEAD_DIM),
                                   v_a.reshape(B, S, H_SB, HEAD_DIM))
    o_b = window_attention(q_b.reshape(B, S, KV_SW, g_sw, HEAD_DIM),
                           k_b.reshape(B, S, KV_SW, HEAD_DIM),
                           v_b.reshape(B, S, KV_SW, HEAD_DIM),
                           SW_WINDOW, alibi_slopes(H_SW, KV_SW), sinks.reshape(KV_SW, g_sw))
    o = jnp.concatenate([o_a.reshape(B, S, -1), o_b.reshape(B, S, -1)], axis=-1)
    return o @ w_out


def nsa_mixer(h, w_in, cmp_pos, wk1, wk2, wv1, wv2, w_out):
    B, S, _ = h.shape
    dq, dkv = H_NSA * HEAD_DIM, KV_NSA * HEAD_DIM
    g_nsa = H_NSA // KV_NSA
    q, kc, vc, ks, vs, kw, vw, g = split_cols(h @ w_in, [dq, dkv, dkv, dkv, dkv, dkv, dkv, 3 * H_NSA])
    kv = lambda z: z.reshape(B, S, KV_NSA, HEAD_DIM)
    k_cmp = compress_blocks(kv(kc), cmp_pos, wk1, wk2)
    v_cmp = compress_blocks(kv(vc), cmp_pos, wv1, wv2)
    gates = jax.nn.sigmoid(g.reshape(B, S, KV_NSA, g_nsa, 3))
    o = nsa_attention(q.reshape(B, S, KV_NSA, g_nsa, HEAD_DIM), k_cmp, v_cmp,
                      kv(ks), kv(vs), kv(kw), kv(vw), gates, alibi_slopes(H_NSA, KV_NSA))
    return o.reshape(B, S, -1) @ w_out


def swiglu(h, w_gate_up, w_down):
    gate, up = jnp.split(h @ w_gate_up, 2, axis=-1)
    return (jax.nn.silu(gate) * up) @ w_down


def setup_inputs(seed: int = 0) -> dict:
    key = jax.random.key(seed)
    ks = jax.random.split(key, 16)
    nrm = lambda k, shape: jax.random.normal(k, shape, jnp.float32)
    w = lambda k, shape, fan_in: nrm(k, shape) * fan_in ** -0.5
    gain = lambda k, shape: 1.0 + 0.02 * nrm(k, shape)
    cmp_in = NSA_CMP_LEN * HEAD_DIM
    return {
        "x": nrm(ks[0], (BATCH, SEQ, D_MODEL)),
        "attn_norm": gain(ks[1], (DEPTH, D_MODEL)),
        "ffn_norm": gain(ks[2], (DEPTH, D_MODEL)),
        "w_gate_up": w(ks[3], (DEPTH, D_MODEL, 2 * D_FF), D_MODEL),
        "w_down": w(ks[4], (DEPTH, D_FF, D_MODEL), D_FF),
        "ab_w_in": w(ks[5], (N_EVEN, D_MODEL, AB_IN_WIDTH), D_MODEL),
        "ab_sinks": 0.5 * nrm(ks[6], (N_EVEN, H_SW)),
        "ab_w_out": w(ks[7], (N_EVEN, AB_OUT_WIDTH, D_MODEL), AB_OUT_WIDTH),
        "nsa_w_in": w(ks[8], (N_ODD, D_MODEL, NSA_IN_WIDTH), D_MODEL),
        "nsa_cmp_pos": 0.1 * nrm(ks[9], (N_ODD, NSA_CMP_LEN, HEAD_DIM)),
        "nsa_cmp_wk1": w(ks[10], (N_ODD, cmp_in, NSA_CMP_HID), cmp_in),
        "nsa_cmp_wk2": w(ks[11], (N_ODD, NSA_CMP_HID, HEAD_DIM), NSA_CMP_HID),
        "nsa_cmp_wv1": w(ks[12], (N_ODD, cmp_in, NSA_CMP_HID), cmp_in),
        "nsa_cmp_wv2": w(ks[13], (N_ODD, NSA_CMP_HID, HEAD_DIM), NSA_CMP_HID),
        "nsa_w_out": w(ks[14], (N_ODD, NSA_OUT_WIDTH, D_MODEL), NSA_OUT_WIDTH),
        "final_norm": gain(ks[15], (D_MODEL,)),
    }


def reference(x, attn_norm, ffn_norm, w_gate_up, w_down, ab_w_in, ab_sinks, ab_w_out,
              nsa_w_in, nsa_cmp_pos, nsa_cmp_wk1, nsa_cmp_wk2, nsa_cmp_wv1, nsa_cmp_wv2,
              nsa_w_out, final_norm):
    for layer in range(DEPTH):
        h = rmsnorm(x, attn_norm[layer])
        if layer % 2 == 0:
            e = layer // 2
            x = x + ab_mixer(h, ab_w_in[e], ab_sinks[e], ab_w_out[e])
        else:
            o = layer // 2
            x = x + nsa_mixer(h, nsa_w_in[o], nsa_cmp_pos[o], nsa_cmp_wk1[o], nsa_cmp_wk2[o],
                              nsa_cmp_wv1[o], nsa_cmp_wv2[o], nsa_w_out[o])
        x = x + swiglu(rmsnorm(x, ffn_norm[layer]), w_gate_up[layer], w_down[layer])
    return rmsnorm(x, final_norm)
```

```python
import functools

import numpy as np
import jax
import jax.numpy as jnp
from jax import lax
from jax.experimental import pallas as pl
from jax.experimental.pallas import tpu as pltpu

F32 = jnp.float32
BF16 = jnp.bfloat16

HEAD_DIM = 128
BLK = 128
EPS = 1e-5
NEG = -1e30
SCALE = HEAD_DIM ** -0.5

H_SB = 8
H_SW = 24
KV_SW = 3
SW_WINDOW = 128

H_NSA = 32
KV_NSA = 2
NSA_CMP_LEN = 32
NSA_CMP_STRIDE = 16
NSA_SEL_LEN = 64
NSA_TOP_N = 8
NSA_WINDOW = 512
NSA_FORCE = 1e4

VMEM_LIMIT_BYTES = 52 * 2 ** 20

_NT = (((1,), (1,)), ((), ()))


def _params(*sem):
    return pltpu.CompilerParams(dimension_semantics=sem,
                                vmem_limit_bytes=VMEM_LIMIT_BYTES)


def _dot(a, b):
    return jnp.dot(a, b, preferred_element_type=F32)


def _dot_nt(a, b):
    return lax.dot_general(a, b, _NT, preferred_element_type=F32)


def _alibi_slopes(n_heads, n_kv):
    s = np.exp2(np.float32(-8.0) * np.arange(1, n_heads + 1, dtype=np.float32)
                / np.float32(n_heads)).astype(np.float32)
    return s.reshape(n_heads // n_kv, n_kv).T


def _rmsnorm_kernel(x_ref, g_ref, o_ref):
    x = x_ref[...]
    ms = jnp.mean(x * x, axis=-1, keepdims=True)
    o_ref[...] = (x * lax.rsqrt(ms + EPS) * g_ref[...]).astype(o_ref.dtype)


def _rmsnorm(x, g, out_dtype, tm=256):
    m, d = x.shape
    return pl.pallas_call(
        _rmsnorm_kernel,
        grid=(m // tm,),
        in_specs=[pl.BlockSpec((tm, d), lambda i: (i, 0)),
                  pl.BlockSpec((1, d), lambda i: (0, 0))],
        out_specs=pl.BlockSpec((tm, d), lambda i: (i, 0)),
        out_shape=jax.ShapeDtypeStruct((m, d), out_dtype),
        compiler_params=_params("parallel"),
        name="rmsnorm",
    )(x, g.reshape(1, d).astype(F32))


def _mm_kernel(*refs, nk, has_res):
    if has_res:
        a_ref, b_ref, r_ref, o_ref = refs[:4]
        scratch = refs[4:]
    else:
        a_ref, b_ref, o_ref = refs[:3]
        r_ref = None
        scratch = refs[3:]

    def finish(acc):
        if has_res:
            acc = acc + r_ref[...]
        o_ref[...] = acc.astype(o_ref.dtype)

    part = _dot(a_ref[...], b_ref[...])
    if nk == 1:
        finish(part)
    else:
        acc_ref, = scratch
        k = pl.program_id(2)

        @pl.when(k == 0)
        def _():
            acc_ref[...] = part

        @pl.when(k > 0)
        def _():
            acc_ref[...] += part

        @pl.when(k == nk - 1)
        def _():
            finish(acc_ref[...])


def _matmul(a, b, out_dtype, *, tm, tn, tk=None, residual=None):
    m, kd = a.shape
    _, n = b.shape
    tk = kd if tk is None else tk
    nk = kd // tk
    assert m % tm == 0 and n % tn == 0 and kd % tk == 0
    in_specs = [pl.BlockSpec((tm, tk), lambda i, j, k: (i, k)),
                pl.BlockSpec((tk, tn), lambda i, j, k: (k, j))]
    args = [a, b]
    if residual is not None:
        in_specs.append(pl.BlockSpec((tm, tn), lambda i, j, k: (i, j)))
        args.append(residual)
    return pl.pallas_call(
        functools.partial(_mm_kernel, nk=nk, has_res=residual is not None),
        grid=(m // tm, n // tn, nk),
        in_specs=in_specs,
        out_specs=pl.BlockSpec((tm, tn), lambda i, j, k: (i, j)),
        out_shape=jax.ShapeDtypeStruct((m, n), out_dtype),
        scratch_shapes=[pltpu.VMEM((tm, tn), F32)] if nk > 1 else [],
        compiler_params=_params("parallel", "parallel", "arbitrary"),
        name="matmul",
    )(*args)


def _gate_up_kernel(a_ref, wg_ref, wu_ref, o_ref):
    a = a_ref[...]
    gate = _dot(a, wg_ref[...])
    up = _dot(a, wu_ref[...])
    o_ref[...] = (gate * jax.nn.sigmoid(gate) * up).astype(o_ref.dtype)


def _gate_up(a, w_gate_up, *, tm, tn):
    m, kd = a.shape
    f = w_gate_up.shape[1] // 2
    nf = f // tn
    assert m % tm == 0 and f % tn == 0
    return pl.pallas_call(
        _gate_up_kernel,
        grid=(m // tm, nf),
        in_specs=[pl.BlockSpec((tm, kd), lambda i, j: (i, 0)),
                  pl.BlockSpec((kd, tn), lambda i, j: (0, j)),
                  pl.BlockSpec((kd, tn), lambda i, j: (0, j + nf))],
        out_specs=pl.BlockSpec((tm, tn), lambda i, j: (i, j)),
        out_shape=jax.ShapeDtypeStruct((m, f), BF16),
        compiler_params=_params("parallel", "parallel"),
        name="gate_up",
    )(a, w_gate_up, w_gate_up)


def _sb_kernel(q_ref, k_ref, v_ref, o_ref, acc_ref, *, tq, tk):
    qi = pl.program_id(1)
    ratio = tq // tk
    q = q_ref[...]
    row = lax.broadcasted_iota(jnp.int32, (tk, tk), 0)
    col = lax.broadcasted_iota(jnp.int32, (tk, tk), 1)
    suffix = jnp.where(row >= col, 1.0, 0.0).astype(BF16)
    acc_ref[...] = jnp.zeros_like(acc_ref)

    def sweep(j, later, masked):
        start = pl.multiple_of(j * tk, tk)
        k = k_ref[pl.ds(start, tk), :]
        v = v_ref[pl.ds(start, tk), :]
        z = _dot_nt(q, k) * SCALE
        a = -(jnp.maximum(z, 0.0) + jnp.log(1.0 + jnp.exp(-jnp.abs(z))))
        if masked:
            t_pos = qi * tq + lax.broadcasted_iota(jnp.int32, (tq, tk), 0)
            s_pos = j * tk + lax.broadcasted_iota(jnp.int32, (tq, tk), 1)
            before = s_pos < t_pos
            a = jnp.where(before, a, 0.0)
        a_hi = a.astype(BF16)
        a_lo = (a - a_hi.astype(F32)).astype(BF16)
        within = _dot(a_hi, suffix) + _dot(a_lo, suffix)
        w = jnp.exp(z + within + later)
        if masked:
            w = jnp.where(before, w, 0.0)
        acc_ref[...] += _dot(w.astype(BF16), v)
        return later + within[:, 0:1]

    later = jnp.zeros((tq, 1), F32)
    for d in range(ratio):
        later = sweep(ratio * qi + (ratio - 1 - d), later, True)
    n_full = ratio * qi
    lax.fori_loop(0, n_full, lambda s, c: sweep(n_full - 1 - s, c, False), later)
    o_ref[...] = acc_ref[...].astype(o_ref.dtype)


def _stick_breaking(z, *, n_heads, q_col, k_col, v_col, tq=256, tk=128):
    s = z.shape[0]
    return pl.pallas_call(
        functools.partial(_sb_kernel, tq=tq, tk=tk),
        grid=(n_heads, s // tq),
        in_specs=[pl.BlockSpec((tq, HEAD_DIM), lambda h, i: (i, q_col + h)),
                  pl.BlockSpec((s, HEAD_DIM), lambda h, i: (0, k_col + h)),
                  pl.BlockSpec((s, HEAD_DIM), lambda h, i: (0, v_col + h))],
        out_specs=pl.BlockSpec((tq, HEAD_DIM), lambda h, i: (i, h)),
        out_shape=jax.ShapeDtypeStruct((s, n_heads * HEAD_DIM), BF16),
        scratch_shapes=[pltpu.VMEM((tq, HEAD_DIM), F32)],
        compiler_params=_params("parallel", "arbitrary"),
        name="stick_breaking",
    )(z, z, z)


def _swa_kernel(sink_ref, q_ref, kp_ref, kc_ref, vp_ref, vc_ref, o_ref, *, slopes):
    i = pl.program_id(0)
    n_kv, g_per = slopes.shape
    qq = lax.broadcasted_iota(jnp.int32, (BLK, 2 * BLK), 0)
    kk = lax.broadcasted_iota(jnp.int32, (BLK, 2 * BLK), 1)
    dist = qq + BLK - kk
    first_key = jnp.where(i > 0, 0, BLK)
    mask = (dist >= 0) & (dist < SW_WINDOW) & (kk >= first_key)
    dist_f = dist.astype(F32)
    for h in range(n_kv):
        cs = slice(h * HEAD_DIM, (h + 1) * HEAD_DIM)
        k = jnp.concatenate([kp_ref[:, cs], kc_ref[:, cs]], axis=0)
        v = jnp.concatenate([vp_ref[:, cs], vc_ref[:, cs]], axis=0)
        for g in range(g_per):
            head = h * g_per + g
            hs = slice(head * HEAD_DIM, (head + 1) * HEAD_DIM)
            s = _dot_nt(q_ref[:, hs], k) * SCALE - float(slopes[h, g]) * dist_f
            s = jnp.where(mask, s, NEG)
            sink = sink_ref[head]
            m = jnp.maximum(jnp.max(s, axis=-1, keepdims=True), sink)
            e = jnp.where(mask, jnp.exp(s - m), 0.0)
            denom = jnp.sum(e, axis=-1, keepdims=True) + jnp.exp(sink - m)
            p = e / denom
            o_ref[:, hs] = _dot(p.astype(BF16), v).astype(o_ref.dtype)


def _window_sink_attention(z, sinks, *, q_col, k_col, v_col):
    s = z.shape[0]
    qw = H_SW * HEAD_DIM
    kw = KV_SW * HEAD_DIM
    assert q_col % qw == 0 and k_col % kw == 0 and v_col % kw == 0
    qb, kb, vb = q_col // qw, k_col // kw, v_col // kw
    prev = lambda i: jnp.maximum(i - 1, 0)
    return pl.pallas_call(
        functools.partial(_swa_kernel, slopes=_alibi_slopes(H_SW, KV_SW)),
        grid=(s // BLK,),
        in_specs=[pl.BlockSpec(memory_space=pltpu.SMEM),
                  pl.BlockSpec((BLK, qw), lambda i: (i, qb)),
                  pl.BlockSpec((BLK, kw), lambda i: (prev(i), kb)),
                  pl.BlockSpec((BLK, kw), lambda i: (i, kb)),
                  pl.BlockSpec((BLK, kw), lambda i: (prev(i), vb)),
                  pl.BlockSpec((BLK, kw), lambda i: (i, vb))],
        out_specs=pl.BlockSpec((BLK, qw), lambda i: (i, 0)),
        out_shape=jax.ShapeDtypeStruct((s, qw), BF16),
        compiler_params=_params("parallel"),
        name="window_sink_attention",
    )(sinks.astype(F32), z, z, z, z, z)


def _compress_kernel(x_ref, pos_ref, w1_ref, w2_ref, o_ref):
    x = x_ref[0].astype(F32)
    half = x.shape[1]
    n_chunk = x.shape[0]
    lo = _dot((x + pos_ref[:, :half]).astype(BF16), w1_ref[0, :half, :])
    hi = _dot((x + pos_ref[:, half:]).astype(BF16), w1_ref[0, half:, :])
    pre = lo + pltpu.roll(hi, n_chunk - 1, 0)
    act = jax.nn.gelu(pre, approximate=True)
    out = _dot(act.astype(BF16), w2_ref[0])
    last = lax.broadcasted_iota(jnp.int32, out.shape, 0) == n_chunk - 1
    o_ref[0] = jnp.where(last, 0.0, out).astype(o_ref.dtype)


def _compress(x, pos, w1, w2):
    n, n_chunk, width = x.shape
    hid = w1.shape[-1]
    return pl.pallas_call(
        _compress_kernel,
        grid=(n,),
        in_specs=[pl.BlockSpec((1, n_chunk, width), lambda i: (i, 0, 0)),
                  pl.BlockSpec((1, 2 * width), lambda i: (0, 0)),
                  pl.BlockSpec((1, 2 * width, hid), lambda i: (i // KV_NSA, 0, 0)),
                  pl.BlockSpec((1, hid, HEAD_DIM), lambda i: (i // KV_NSA, 0, 0))],
        out_specs=pl.BlockSpec((1, n_chunk, HEAD_DIM), lambda i: (i, 0, 0)),
        out_shape=jax.ShapeDtypeStruct((n, n_chunk, HEAD_DIM), BF16),
        compiler_params=_params("parallel"),
        name="nsa_compress",
    )(x, pos, w1, w2)


def _split3(x):
    hi = x.astype(BF16)
    r1 = x - hi.astype(F32)
    mid = r1.astype(BF16)
    lo = (r1 - mid.astype(F32)).astype(BF16)
    return hi, mid, lo


def _nsa_cmp_kernel(slope_ref, q_ref, kc_ref, vc_ref, o_ref, sel_ref, need_ref,
                    imp_ref, *, g_per, n_slc):
    h = pl.program_id(0)
    i = pl.program_id(1)
    n_cmp = kc_ref.shape[1]
    kc = kc_ref[0]
    vc = vc_ref[0]
    t = i * BLK + lax.broadcasted_iota(jnp.int32, (BLK, n_cmp), 0)
    c_end = lax.broadcasted_iota(jnp.int32, (BLK, n_cmp), 1) * NSA_CMP_STRIDE + (NSA_CMP_LEN - 1)
    dist = t - c_end
    mask = dist >= 0
    dist_f = dist.astype(F32)
    imp_ref[...] = jnp.zeros_like(imp_ref)
    for g in range(g_per):
        hs = slice(g * HEAD_DIM, (g + 1) * HEAD_DIM)
        s = _dot_nt(q_ref[:, hs], kc) * SCALE - slope_ref[h * g_per + g] * dist_f
        s = jnp.where(mask, s, NEG)
        m = jnp.max(s, axis=-1, keepdims=True)
        e = jnp.where(mask, jnp.exp(s - m), 0.0)
        p = e / jnp.maximum(jnp.sum(e, axis=-1, keepdims=True), 1e-30)
        o_ref[:, hs] = _dot(p.astype(BF16), vc)
        imp_ref[...] += p

    r = NSA_SEL_LEN // NSA_CMP_STRIDE
    ci = lax.broadcasted_iota(jnp.int32, (n_cmp, n_slc), 0)
    cj = lax.broadcasted_iota(jnp.int32, (n_cmp, n_slc), 1)
    pool = jnp.where((ci >= r * cj - 1) & (ci <= r * cj + r - 1), 1.0, 0.0).astype(BF16)
    hi, mid, lo = _split3(imp_ref[...])
    imp = _dot(hi, pool) + (_dot(mid, pool) + _dot(lo, pool))

    blk = lax.broadcasted_iota(jnp.int32, (BLK, n_slc), 1)
    cur = (i * BLK + lax.broadcasted_iota(jnp.int32, (BLK, n_slc), 0)) // NSA_SEL_LEN
    valid = blk <= cur
    forced = (blk == 0) | (blk == cur) | (blk == cur - 1)
    score = jnp.where(valid, imp + jnp.where(forced, NSA_FORCE, 0.0), NEG)
    blk_f = blk.astype(F32)
    chosen = jnp.zeros((BLK, n_slc), F32)
    for _ in range(min(NSA_TOP_N, n_slc)):
        best = jnp.max(score, axis=-1, keepdims=True)
        first = jnp.min(jnp.where(score == best, blk_f, float(n_slc)), axis=-1, keepdims=True)
        pick = blk_f == first
        chosen = jnp.where(pick, 1.0, chosen)
        score = jnp.where(pick, -jnp.inf, score)
    chosen = jnp.where(valid, chosen, 0.0)
    sel_ref[0] = chosen.astype(sel_ref.dtype)

    n_kb = n_slc * NSA_SEL_LEN // BLK
    any_q = jnp.max(chosen, axis=0, keepdims=True)
    any_q = jnp.broadcast_to(any_q, (8, n_slc)).astype(BF16)
    pi = lax.broadcasted_iota(jnp.int32, (n_slc, n_kb), 0)
    pj = lax.broadcasted_iota(jnp.int32, (n_slc, n_kb), 1)
    per_kb = _dot(any_q, jnp.where(pi * NSA_SEL_LEN // BLK == pj, 1.0, 0.0).astype(BF16))
    per_kb = jnp.where(per_kb > 0.5, 1.0, 0.0).astype(BF16)
    n_word = need_ref.shape[-1]
    bi = lax.broadcasted_iota(jnp.int32, (n_kb, n_word), 0)
    bw = lax.broadcasted_iota(jnp.int32, (n_kb, n_word), 1)
    weight = jnp.where(bi // 16 == bw, jnp.left_shift(1, bi % 16), 0).astype(F32).astype(BF16)
    words = _dot(per_kb, weight)
    need_ref[0, 0] = words.astype(jnp.int32)


def _nsa_compressed(z, k_cmp, v_cmp, slopes):
    s = z.shape[0]
    n_slc = s // NSA_SEL_LEN
    g_per = H_NSA // KV_NSA
    qw = g_per * HEAD_DIM
    n_cmp = k_cmp.shape[1]
    n_word = 128
    assert (s // BLK) <= 16 * n_word
    return pl.pallas_call(
        functools.partial(_nsa_cmp_kernel, g_per=g_per, n_slc=n_slc),
        grid=(KV_NSA, s // BLK),
        in_specs=[pl.BlockSpec(memory_space=pltpu.SMEM),
                  pl.BlockSpec((BLK, qw), lambda h, i: (i, h)),
                  pl.BlockSpec((1, n_cmp, HEAD_DIM), lambda h, i: (h, 0, 0)),
                  pl.BlockSpec((1, n_cmp, HEAD_DIM), lambda h, i: (h, 0, 0))],
        out_specs=[pl.BlockSpec((BLK, qw), lambda h, i: (i, h)),
                   pl.BlockSpec((1, BLK, n_slc), lambda h, i: (h, i, 0)),
                   pl.BlockSpec((1, 1, 8, n_word), lambda h, i: (h, i, 0, 0))],
        out_shape=[jax.ShapeDtypeStruct((s, H_NSA * HEAD_DIM), F32),
                   jax.ShapeDtypeStruct((KV_NSA, s, n_slc), BF16),
                   jax.ShapeDtypeStruct((KV_NSA, s // BLK, 8, n_word), jnp.int32)],
        scratch_shapes=[pltpu.VMEM((BLK, n_cmp), F32)],
        compiler_params=_params("parallel", "arbitrary"),
        name="nsa_compressed",
    )(slopes, z, k_cmp, v_cmp)


def _stack_heads(q_ref, qs_ref, g_per):
    for g in range(g_per):
        qs_ref[g * BLK:(g + 1) * BLK, :] = q_ref[:, g * HEAD_DIM:(g + 1) * HEAD_DIM]


def _online_visit(j, i, qs_ref, k_ref, v_ref, slope_ref, m_ref, l_ref, acc_ref, g_per, key_ok):
    rows = g_per * BLK
    start = pl.multiple_of(j * BLK, BLK)
    k = k_ref[pl.ds(start, BLK), :]
    v = v_ref[pl.ds(start, BLK), :]
    dist = ((i - j) * BLK + lax.broadcasted_iota(jnp.int32, (BLK, BLK), 0)
            - lax.broadcasted_iota(jnp.int32, (BLK, BLK), 1))
    ok = jnp.concatenate([key_ok(dist)] * g_per, axis=0) > 0.5
    dist_f = jnp.concatenate([dist.astype(F32)] * g_per, axis=0)
    s = _dot_nt(qs_ref[...], k) * SCALE - slope_ref[0] * dist_f
    s = jnp.where(ok, s, NEG)
    m_prev = m_ref[...]
    m_new = jnp.maximum(m_prev, jnp.max(s, axis=-1, keepdims=True))
    alpha = jnp.exp(m_prev - m_new)
    p = jnp.where(ok, jnp.exp(s - m_new), 0.0)
    l_ref[...] = alpha * l_ref[...] + jnp.sum(p, axis=-1, keepdims=True)
    acc_ref[...] = alpha * acc_ref[...] + _dot(p.astype(BF16), v)
    m_ref[...] = m_new


def _online_init(m_ref, l_ref, acc_ref):
    m_ref[...] = jnp.full_like(m_ref, NEG)
    l_ref[...] = jnp.zeros_like(l_ref)
    acc_ref[...] = jnp.zeros_like(acc_ref)


def _nsa_sel_kernel(need_ref, q_ref, sel_ref, slope_ref, k_ref, v_ref, o_ref,
                    qs_ref, m_ref, l_ref, acc_ref, *, g_per, n_qblk, n_word):
    h = pl.program_id(0)
    i = pl.program_id(1)
    _stack_heads(q_ref, qs_ref, g_per)
    _online_init(m_ref, l_ref, acc_ref)
    sel = sel_ref[0]
    n_slc = sel.shape[1]
    per_blk = BLK // NSA_SEL_LEN
    eb = lax.broadcasted_iota(jnp.int32, (n_slc, BLK), 0)
    el = lax.broadcasted_iota(jnp.int32, (n_slc, BLK), 1) // NSA_SEL_LEN
    base = (h * n_qblk + i) * n_word

    def body(j, carry):
        word = need_ref[base + jnp.right_shift(j, 4)]
        needed = jnp.bitwise_and(jnp.right_shift(word, jnp.bitwise_and(j, 15)), 1) == 1

        @pl.when(needed)
        def _():
            expand = jnp.where(eb == per_blk * j + el, 1.0, 0.0).astype(BF16)
            picked = _dot(sel, expand)
            key_ok = lambda dist: jnp.where(dist >= 0, picked, 0.0)
            _online_visit(j, i, qs_ref, k_ref, v_ref, slope_ref, m_ref, l_ref, acc_ref,
                          g_per, key_ok)
        return carry

    lax.fori_loop(0, i + 1, body, 0)
    out = acc_ref[...] / jnp.maximum(l_ref[...], 1e-30)
    for g in range(g_per):
        o_ref[:, g * HEAD_DIM:(g + 1) * HEAD_DIM] = out[g * BLK:(g + 1) * BLK, :]


def _nsa_selected(z, sel, need, slope_rows, *, k_col, v_col):
    s = z.shape[0]
    g_per = H_NSA // KV_NSA
    qw = g_per * HEAD_DIM
    rows = g_per * BLK
    n_slc = sel.shape[-1]
    n_qblk = s // BLK
    n_word = -(-n_qblk // 16)
    need = need[:, :, 0, :n_word]
    return pl.pallas_call(
        functools.partial(_nsa_sel_kernel, g_per=g_per, n_qblk=n_qblk, n_word=n_word),
        grid=(KV_NSA, n_qblk),
        in_specs=[pl.BlockSpec(memory_space=pltpu.SMEM),
                  pl.BlockSpec((BLK, qw), lambda h, i: (i, h)),
                  pl.BlockSpec((1, BLK, n_slc), lambda h, i: (h, i, 0)),
                  pl.BlockSpec((1, rows, 1), lambda h, i: (h, 0, 0)),
                  pl.BlockSpec((s, HEAD_DIM), lambda h, i: (0, k_col + h)),
                  pl.BlockSpec((s, HEAD_DIM), lambda h, i: (0, v_col + h))],
        out_specs=pl.BlockSpec((BLK, qw), lambda h, i: (i, h)),
        out_shape=jax.ShapeDtypeStruct((s, H_NSA * HEAD_DIM), F32),
        scratch_shapes=[pltpu.VMEM((rows, HEAD_DIM), BF16),
                        pltpu.VMEM((rows, 1), F32),
                        pltpu.VMEM((rows, 1), F32),
                        pltpu.VMEM((rows, HEAD_DIM), F32)],
        compiler_params=_params("parallel", "arbitrary"),
        name="nsa_selected",
    )(need.reshape(-1), z, sel, slope_rows, z, z)


def _nsa_win_kernel(q_ref, slope_ref, k_ref, v_ref, gate_ref, ocmp_ref, oslc_ref, o_ref,
                    qs_ref, m_ref, l_ref, acc_ref, *, g_per):
    i = pl.program_id(1)
    _stack_heads(q_ref, qs_ref, g_per)
    _online_init(m_ref, l_ref, acc_ref)
    n_prev = -(-(NSA_WINDOW - 1) // BLK)

    def body(step, carry):
        j = i - n_prev + step

        @pl.when(j >= 0)
        def _():
            key_ok = lambda dist: jnp.where((dist >= 0) & (dist < NSA_WINDOW), 1.0, 0.0)
            _online_visit(j, i, qs_ref, k_ref, v_ref, slope_ref, m_ref, l_ref, acc_ref,
                          g_per, key_ok)
        return carry

    lax.fori_loop(0, n_prev + 1, body, 0)
    o_win = acc_ref[...] / jnp.maximum(l_ref[...], 1e-30)
    gates = jax.nn.sigmoid(gate_ref[...])
    for g in range(g_per):
        hs = slice(g * HEAD_DIM, (g + 1) * HEAD_DIM)
        o = (gates[:, 3 * g:3 * g + 1] * ocmp_ref[:, hs]
             + gates[:, 3 * g + 1:3 * g + 2] * oslc_ref[:, hs]
             + gates[:, 3 * g + 2:3 * g + 3] * o_win[g * BLK:(g + 1) * BLK, :])
        o_ref[:, hs] = o.astype(o_ref.dtype)


def _nsa_window_combine(z, gate_logits, o_cmp, o_slc, slope_rows, *, k_col, v_col):
    s = z.shape[0]
    g_per = H_NSA // KV_NSA
    qw = g_per * HEAD_DIM
    rows = g_per * BLK
    head_blk = lambda h, i: (i, h)
    return pl.pallas_call(
        functools.partial(_nsa_win_kernel, g_per=g_per),
        grid=(KV_NSA, s // BLK),
        in_specs=[pl.BlockSpec((BLK, qw), head_blk),
                  pl.BlockSpec((1, rows, 1), lambda h, i: (h, 0, 0)),
                  pl.BlockSpec((s, HEAD_DIM), lambda h, i: (0, k_col + h)),
                  pl.BlockSpec((s, HEAD_DIM), lambda h, i: (0, v_col + h)),
                  pl.BlockSpec((BLK, 128), head_blk),
                  pl.BlockSpec((BLK, qw), head_blk),
                  pl.BlockSpec((BLK, qw), head_blk)],
        out_specs=pl.BlockSpec((BLK, qw), head_blk),
        out_shape=jax.ShapeDtypeStruct((s, H_NSA * HEAD_DIM), BF16),
        scratch_shapes=[pltpu.VMEM((rows, HEAD_DIM), BF16),
                        pltpu.VMEM((rows, 1), F32),
                        pltpu.VMEM((rows, 1), F32),
                        pltpu.VMEM((rows, HEAD_DIM), F32)],
        compiler_params=_params("parallel", "arbitrary"),
        name="nsa_window_combine",
    )(z, slope_rows, z, z, gate_logits, o_cmp, o_slc)


def _ab_mixer(h, w_in, sinks, w_out, x_res):
    z = _matmul(h, w_in.astype(BF16), BF16, tm=512, tn=768)
    d_sb = H_SB * HEAD_DIM
    o_a = _stick_breaking(z, n_heads=H_SB, q_col=0, k_col=H_SB, v_col=2 * H_SB)
    q_b = 3 * d_sb
    k_b = q_b + H_SW * HEAD_DIM
    v_b = k_b + KV_SW * HEAD_DIM
    o_b = _window_sink_attention(z, sinks, q_col=q_b, k_col=k_b, v_col=v_b)
    o = jnp.concatenate([o_a, o_b], axis=-1)
    return _matmul(o, w_out.astype(BF16), F32, tm=512, tn=512, residual=x_res)


def _nsa_mixer(h, w_in, cmp_pos, wk1, wk2, wv1, wv2, w_out, x_res):
    s = h.shape[0]
    dq = H_NSA * HEAD_DIM
    dkv = KV_NSA * HEAD_DIM
    g_per = H_NSA // KV_NSA
    main = dq + 6 * dkv
    z = _matmul(h, w_in[:, :main].astype(BF16), BF16, tm=512, tn=512)
    w_gate = w_in[:, main:].reshape(-1, KV_NSA, 3 * g_per)
    w_gate = jnp.pad(w_gate, ((0, 0), (0, 0), (0, 128 - 3 * g_per))).reshape(-1, KV_NSA * 128)
    gate_logits = _matmul(h, w_gate.astype(BF16), F32, tm=512, tn=KV_NSA * 128)

    n_chunk = s // NSA_CMP_STRIDE
    to_chunks = lambda c: (z[:, c:c + dkv].reshape(n_chunk, NSA_CMP_STRIDE, KV_NSA, HEAD_DIM)
                           .transpose(2, 0, 1, 3).reshape(KV_NSA, n_chunk, NSA_CMP_STRIDE * HEAD_DIM))
    chunks = jnp.concatenate([to_chunks(dq), to_chunks(dq + dkv)], axis=0)
    kv_cmp = _compress(chunks, cmp_pos.reshape(1, -1).astype(F32),
                       jnp.stack([wk1, wv1]).astype(BF16), jnp.stack([wk2, wv2]).astype(BF16))
    k_cmp, v_cmp = kv_cmp[:KV_NSA], kv_cmp[KV_NSA:]

    slopes = _alibi_slopes(H_NSA, KV_NSA)
    slope_rows = jnp.asarray(np.repeat(slopes, BLK, axis=1)[..., None])
    o_cmp, sel, need = _nsa_compressed(z, k_cmp, v_cmp, jnp.asarray(slopes.reshape(-1)))
    col = lambda off: (dq + off * dkv) // HEAD_DIM
    o_slc = _nsa_selected(z, sel, need, slope_rows, k_col=col(2), v_col=col(3))
    o = _nsa_window_combine(z, gate_logits, o_cmp, o_slc, slope_rows, k_col=col(4), v_col=col(5))
    return _matmul(o, w_out.astype(BF16), F32, tm=512, tn=512, residual=x_res)


def _swiglu(h, w_gate_up, w_down, x_res):
    act = _gate_up(h, w_gate_up.astype(BF16), tm=512, tn=256)
    d_ff = w_down.shape[0]
    return _matmul(act, w_down.astype(BF16), F32, tm=512, tn=512, tk=d_ff // 2, residual=x_res)


def kernel(x, attn_norm, ffn_norm, w_gate_up, w_down, ab_w_in, ab_sinks, ab_w_out, nsa_w_in,
           nsa_cmp_pos, nsa_cmp_wk1, nsa_cmp_wk2, nsa_cmp_wv1, nsa_cmp_wv2, nsa_w_out, final_norm):
    b, s, d = x.shape
    outs = []
    for bi in range(b):
        xb = x[bi]
        for layer in range(attn_norm.shape[0]):
            h = _rmsnorm(xb, attn_norm[layer], BF16)
            if layer % 2 == 0:
                e = layer // 2
                xb = _ab_mixer(h, ab_w_in[e], ab_sinks[e], ab_w_out[e], xb)
            else:
                o = layer // 2
                xb = _nsa_mixer(h, nsa_w_in[o], nsa_cmp_pos[o], nsa_cmp_wk1[o], nsa_cmp_wk2[o],
                                nsa_cmp_wv1[o], nsa_cmp_wv2[o], nsa_w_out[o], xb)
            h = _rmsnorm(xb, ffn_norm[layer], BF16)
            xb = _swiglu(h, w_gate_up[layer], w_down[layer], xb)
        outs.append(_rmsnorm(xb, final_norm, F32))
    return jnp.stack(outs, axis=0)
```

```python
import functools

import numpy as np
import jax
import jax.numpy as jnp
from jax import lax
from jax.experimental import pallas as pl
from jax.experimental.pallas import tpu as pltpu

F32 = jnp.float32
BF16 = jnp.bfloat16

HEAD_DIM = 128
BLK = 128
EPS = 1e-5
NEG = -1e30
SCALE = HEAD_DIM ** -0.5
LOG2E = 1.4426950408889634

H_SB = 8
H_SW = 24
KV_SW = 3
SW_WINDOW = 128

H_NSA = 32
KV_NSA = 2
NSA_CMP_LEN = 32
NSA_CMP_STRIDE = 16
NSA_SEL_LEN = 64
NSA_TOP_N = 8
NSA_WINDOW = 512
NSA_FORCE = 1e4

VMEM_LIMIT_BYTES = 52 * 2 ** 20

_NT = (((1,), (1,)), ((), ()))


def _params(*sem):
    return pltpu.CompilerParams(dimension_semantics=sem,
                                vmem_limit_bytes=VMEM_LIMIT_BYTES)


def _dot(a, b):
    return jnp.dot(a, b, preferred_element_type=F32)


def _dot_nt(a, b):
    return lax.dot_general(a, b, _NT, preferred_element_type=F32)


def _alibi_slopes(n_heads, n_kv):
    s = np.exp2(np.float32(-8.0) * np.arange(1, n_heads + 1, dtype=np.float32)
                / np.float32(n_heads)).astype(np.float32)
    return s.reshape(n_heads // n_kv, n_kv).T


def _rmsnorm_kernel(x_ref, g_ref, o_ref):
    x = x_ref[...]
    ms = jnp.mean(x * x, axis=-1, keepdims=True)
    o_ref[...] = (x * lax.rsqrt(ms + EPS) * g_ref[...]).astype(o_ref.dtype)


def _rmsnorm(x, g, out_dtype, tm=256):
    m, d = x.shape
    return pl.pallas_call(
        _rmsnorm_kernel,
        grid=(m // tm,),
        in_specs=[pl.BlockSpec((tm, d), lambda i: (i, 0)),
                  pl.BlockSpec((1, d), lambda i: (0, 0))],
        out_specs=pl.BlockSpec((tm, d), lambda i: (i, 0)),
        out_shape=jax.ShapeDtypeStruct((m, d), out_dtype),
        compiler_params=_params("parallel"),
        name="rmsnorm",
    )(x, g.reshape(1, d).astype(F32))


def _mm_kernel(*refs, nk, has_res):
    if has_res:
        a_ref, b_ref, r_ref, o_ref = refs[:4]
        scratch = refs[4:]
    else:
        a_ref, b_ref, o_ref = refs[:3]
        r_ref = None
        scratch = refs[3:]

    def finish(acc):
        if has_res:
            acc = acc + r_ref[...]
        o_ref[...] = acc.astype(o_ref.dtype)

    part = _dot(a_ref[...], b_ref[...])
    if nk == 1:
        finish(part)
    else:
        acc_ref, = scratch
        k = pl.program_id(2)

        @pl.when(k == 0)
        def _():
            acc_ref[...] = part

        @pl.when(k > 0)
        def _():
            acc_ref[...] += part

        @pl.when(k == nk - 1)
        def _():
            finish(acc_ref[...])


def _matmul(a, b, out_dtype, *, tm, tn, tk=None, residual=None):
    m, kd = a.shape
    _, n = b.shape
    tk = kd if tk is None else tk
    nk = kd // tk
    assert m % tm == 0 and n % tn == 0 and kd % tk == 0
    in_specs = [pl.BlockSpec((tm, tk), lambda i, j, k: (i, k)),
                pl.BlockSpec((tk, tn), lambda i, j, k: (k, j))]
    args = [a, b]
    if residual is not None:
        in_specs.append(pl.BlockSpec((tm, tn), lambda i, j, k: (i, j)))
        args.append(residual)
    return pl.pallas_call(
        functools.partial(_mm_kernel, nk=nk, has_res=residual is not None),
        grid=(m // tm, n // tn, nk),
        in_specs=in_specs,
        out_specs=pl.BlockSpec((tm, tn), lambda i, j, k: (i, j)),
        out_shape=jax.ShapeDtypeStruct((m, n), out_dtype),
        scratch_shapes=[pltpu.VMEM((tm, tn), F32)] if nk > 1 else [],
        compiler_params=_params("parallel", "parallel", "arbitrary"),
        name="matmul",
    )(*args)


def _gate_up_kernel(a_ref, wg_ref, wu_ref, o_ref):
    a = a_ref[...]
    gate = _dot(a, wg_ref[...])
    up = _dot(a, wu_ref[...])
    o_ref[...] = (gate * jax.nn.sigmoid(gate) * up).astype(o_ref.dtype)


def _gate_up(a, w_gate_up, *, tm, tn):
    m, kd = a.shape
    f = w_gate_up.shape[1] // 2
    nf = f // tn
    assert m % tm == 0 and f % tn == 0
    return pl.pallas_call(
        _gate_up_kernel,
        grid=(m // tm, nf),
        in_specs=[pl.BlockSpec((tm, kd), lambda i, j: (i, 0)),
                  pl.BlockSpec((kd, tn), lambda i, j: (0, j)),
                  pl.BlockSpec((kd, tn), lambda i, j: (0, j + nf))],
        out_specs=pl.BlockSpec((tm, tn), lambda i, j: (i, j)),
        out_shape=jax.ShapeDtypeStruct((m, f), BF16),
        compiler_params=_params("parallel", "parallel"),
        name="gate_up",
    )(a, w_gate_up, w_gate_up)


def _sb_kernel(q_ref, k_ref, v_ref, o_ref, acc_ref, later_ref, z_ref, w_ref, *, tq, tk):
    qi = pl.program_id(1)
    ratio = tq // tk
    assert ratio % 2 == 0
    q = q_ref[...]
    row = lax.broadcasted_iota(jnp.int32, (tk, tk), 0)
    col = lax.broadcasted_iota(jnp.int32, (tk, tk), 1)
    suffix = jnp.where(row >= col, 1.0, 0.0).astype(BF16)
    suffix2 = jnp.concatenate([suffix, suffix], axis=0)
    acc_ref[...] = jnp.zeros_like(acc_ref)
    later_ref[...] = jnp.zeros_like(later_ref)

    def logits(j):
        start = pl.multiple_of(j * tk, tk)
        return _dot_nt(q, k_ref[pl.ds(start, tk), :]) * (-SCALE * LOG2E)

    def weights(j, nz, masked):
        neg_abs = lax.bitcast_convert_type(
            lax.bitcast_convert_type(nz, jnp.uint32) | jnp.uint32(0x80000000), F32)
        a = jnp.minimum(nz, 0.0) - jnp.log2(1.0 + jnp.exp2(neg_abs))
        if masked:
            t_pos = qi * tq + lax.broadcasted_iota(jnp.int32, (tq, tk), 0)
            s_pos = j * tk + lax.broadcasted_iota(jnp.int32, (tq, tk), 1)
            before = s_pos < t_pos
            a = jnp.where(before, a, 0.0)
        a_hi = a.astype(BF16)
        a_lo = (a - a_hi.astype(F32)).astype(BF16)
        within = _dot(jnp.concatenate([a_hi, a_lo], axis=1), suffix2)
        later = later_ref[...]
        w = jnp.exp2(within + jnp.concatenate([later] * (tk // HEAD_DIM), axis=1) - nz)
        later_ref[...] = later + within[:, 0:1]
        if masked:
            w = jnp.where(before, w, 0.0)
        return w.astype(BF16)

    def accumulate(j, w):
        start = pl.multiple_of(j * tk, tk)
        acc_ref[...] += _dot(w, v_ref[pl.ds(start, tk), :])

    n_full = ratio * qi

    def diagonal(d, carry):
        j = n_full + ratio - 1 - d
        accumulate(j, weights(j, logits(j), True))
        return carry

    lax.fori_loop(0, ratio, diagonal, 0)

    def step(j, slot):
        accumulate(j + 1, w_ref[1 - slot])
        z_ref[1 - slot] = logits(jnp.maximum(j - 1, 0))
        w_ref[slot] = weights(j, z_ref[slot], False)

    def two_steps(s, carry):
        j = n_full - 1 - 2 * s
        step(j, 0)
        step(j - 1, 1)
        return carry

    z_ref[0] = logits(jnp.maximum(n_full - 1, 0))
    w_ref[1] = jnp.zeros((tq, tk), BF16)
    lax.fori_loop(0, n_full // 2, two_steps, 0)
    accumulate(0, w_ref[1])
    o_ref[...] = acc_ref[...].astype(o_ref.dtype)


def _stick_breaking(z, *, n_heads, q_col, k_col, v_col, tq=1024, tk=256):
    s = z.shape[0]
    return pl.pallas_call(
        functools.partial(_sb_kernel, tq=tq, tk=tk),
        grid=(n_heads, s // tq),
        in_specs=[pl.BlockSpec((tq, HEAD_DIM), lambda h, i: (i, q_col + h)),
                  pl.BlockSpec((s, HEAD_DIM), lambda h, i: (0, k_col + h)),
                  pl.BlockSpec((s, HEAD_DIM), lambda h, i: (0, v_col + h))],
        out_specs=pl.BlockSpec((tq, HEAD_DIM), lambda h, i: (i, h)),
        out_shape=jax.ShapeDtypeStruct((s, n_heads * HEAD_DIM), BF16),
        scratch_shapes=[pltpu.VMEM((tq, HEAD_DIM), F32),
                        pltpu.VMEM((tq, HEAD_DIM), F32),
                        pltpu.VMEM((2, tq, tk), F32),
                        pltpu.VMEM((2, tq, tk), BF16)],
        compiler_params=_params("parallel", "arbitrary"),
        name="stick_breaking",
    )(z, z, z)


def _swa_kernel(sink_ref, q_ref, kp_ref, kc_ref, vp_ref, vc_ref, o_ref, *, slopes):
    i = pl.program_id(0)
    n_kv, g_per = slopes.shape
    qq = lax.broadcasted_iota(jnp.int32, (BLK, 2 * BLK), 0)
    kk = lax.broadcasted_iota(jnp.int32, (BLK, 2 * BLK), 1)
    dist = qq + BLK - kk
    first_key = jnp.where(i > 0, 0, BLK)
    mask = (dist >= 0) & (dist < SW_WINDOW) & (kk >= first_key)
    dist_f = dist.astype(F32)
    for h in range(n_kv):
        cs = slice(h * HEAD_DIM, (h + 1) * HEAD_DIM)
        k = jnp.concatenate([kp_ref[:, cs], kc_ref[:, cs]], axis=0)
        v = jnp.concatenate([vp_ref[:, cs], vc_ref[:, cs]], axis=0)
        for g in range(g_per):
            head = h * g_per + g
            hs = slice(head * HEAD_DIM, (head + 1) * HEAD_DIM)
            s = _dot_nt(q_ref[:, hs], k) * SCALE - float(slopes[h, g]) * dist_f
            s = jnp.where(mask, s, NEG)
            sink = sink_ref[head]
            m = jnp.maximum(jnp.max(s, axis=-1, keepdims=True), sink)
            e = jnp.where(mask, jnp.exp(s - m), 0.0)
            denom = jnp.sum(e, axis=-1, keepdims=True) + jnp.exp(sink - m)
            p = e / denom
            o_ref[:, hs] = _dot(p.astype(BF16), v).astype(o_ref.dtype)


def _window_sink_attention(z, sinks, *, q_col, k_col, v_col):
    s = z.shape[0]
    qw = H_SW * HEAD_DIM
    kw = KV_SW * HEAD_DIM
    assert q_col % qw == 0 and k_col % kw == 0 and v_col % kw == 0
    qb, kb, vb = q_col // qw, k_col // kw, v_col // kw
    prev = lambda i: jnp.maximum(i - 1, 0)
    return pl.pallas_call(
        functools.partial(_swa_kernel, slopes=_alibi_slopes(H_SW, KV_SW)),
        grid=(s // BLK,),
        in_specs=[pl.BlockSpec(memory_space=pltpu.SMEM),
                  pl.BlockSpec((BLK, qw), lambda i: (i, qb)),
                  pl.BlockSpec((BLK, kw), lambda i: (prev(i), kb)),
                  pl.BlockSpec((BLK, kw), lambda i: (i, kb)),
                  pl.BlockSpec((BLK, kw), lambda i: (prev(i), vb)),
                  pl.BlockSpec((BLK, kw), lambda i: (i, vb))],
        out_specs=pl.BlockSpec((BLK, qw), lambda i: (i, 0)),
        out_shape=jax.ShapeDtypeStruct((s, qw), BF16),
        compiler_params=_params("parallel"),
        name="window_sink_attention",
    )(sinks.astype(F32), z, z, z, z, z)


def _compress_kernel(x_ref, pos_ref, w1_ref, w2_ref, o_ref):
    x = x_ref[0].astype(F32)
    half = x.shape[1]
    n_chunk = x.shape[0]
    lo = _dot((x + pos_ref[:, :half]).astype(BF16), w1_ref[0, :half, :])
    hi = _dot((x + pos_ref[:, half:]).astype(BF16), w1_ref[0, half:, :])
    pre = lo + pltpu.roll(hi, n_chunk - 1, 0)
    act = jax.nn.gelu(pre, approximate=True)
    out = _dot(act.astype(BF16), w2_ref[0])
    last = lax.broadcasted_iota(jnp.int32, out.shape, 0) == n_chunk - 1
    o_ref[0] = jnp.where(last, 0.0, out).astype(o_ref.dtype)


def _compress(x, pos, w1, w2):
    n, n_chunk, width = x.shape
    hid = w1.shape[-1]
    return pl.pallas_call(
        _compress_kernel,
        grid=(n,),
        in_specs=[pl.BlockSpec((1, n_chunk, width), lambda i: (i, 0, 0)),
                  pl.BlockSpec((1, 2 * width), lambda i: (0, 0)),
                  pl.BlockSpec((1, 2 * width, hid), lambda i: (i // KV_NSA, 0, 0)),
                  pl.BlockSpec((1, hid, HEAD_DIM), lambda i: (i // KV_NSA, 0, 0))],
        out_specs=pl.BlockSpec((1, n_chunk, HEAD_DIM), lambda i: (i, 0, 0)),
        out_shape=jax.ShapeDtypeStruct((n, n_chunk, HEAD_DIM), BF16),
        compiler_params=_params("parallel"),
        name="nsa_compress",
    )(x, pos, w1, w2)


_CMP_CHUNK = 256
_CMP_UNROLL = 4


def _split3(x):
    hi = x.astype(BF16)
    r1 = x - hi.astype(F32)
    mid = r1.astype(BF16)
    lo = (r1 - mid.astype(F32)).astype(BF16)
    return hi, mid, lo


def _nsa_cmp_kernel(slope_ref, q_ref, kc_ref, vc_ref, o_ref, sel_ref, need_ref,
                    imp_ref, qs_ref, os_ref, *, g_per, n_slc):
    h = pl.program_id(0)
    i = pl.program_id(1)
    n_cmp = kc_ref.shape[1]
    _stack_heads(q_ref, qs_ref, g_per)
    imp_ref[...] = jnp.zeros_like(imp_ref)

    def attend(width):
        kc = kc_ref[0, :width, :]
        vc = vc_ref[0, :width, :]
        t = i * BLK + lax.broadcasted_iota(jnp.int32, (BLK, width), 0)
        c_end = (lax.broadcasted_iota(jnp.int32, (BLK, width), 1) * NSA_CMP_STRIDE
                 + (NSA_CMP_LEN - 1))
        dist = t - c_end
        mask = dist >= 0
        dist_f = dist.astype(F32)

        def group(gg, carry):
            for u in range(_CMP_UNROLL):
                g = gg * _CMP_UNROLL + u
                rows = pl.ds(pl.multiple_of(g * BLK, BLK), BLK)
                s = _dot_nt(qs_ref[rows, :], kc) * SCALE - slope_ref[h * g_per + g] * dist_f
                s = jnp.where(mask, s, NEG)
                m = jnp.max(s, axis=-1, keepdims=True)
                e = jnp.where(mask, jnp.exp(s - m), 0.0)
                p = e / jnp.maximum(jnp.sum(e, axis=-1, keepdims=True), 1e-30)
                os_ref[rows, :] = _dot(p.astype(BF16), vc)
                imp_ref[:, :width] += p
            return carry

        lax.fori_loop(0, g_per // _CMP_UNROLL, group, 0)

    n_visible = (BLK * i + BLK - NSA_CMP_LEN) // NSA_CMP_STRIDE + 1
    widths = sorted({min(w, n_cmp) for w in range(_CMP_CHUNK, n_cmp + _CMP_CHUNK, _CMP_CHUNK)})
    case = jnp.minimum((n_visible - 1) // _CMP_CHUNK, len(widths) - 1)
    for idx, width in enumerate(widths):
        pl.when(case == idx)(functools.partial(attend, width))
    for g in range(g_per):
        o_ref[:, g * HEAD_DIM:(g + 1) * HEAD_DIM] = os_ref[g * BLK:(g + 1) * BLK, :]

    r = NSA_SEL_LEN // NSA_CMP_STRIDE
    ci = lax.broadcasted_iota(jnp.int32, (n_cmp, n_slc), 0)
    cj = lax.broadcasted_iota(jnp.int32, (n_cmp, n_slc), 1)
    pool = jnp.where((ci >= r * cj - 1) & (ci <= r * cj + r - 1), 1.0, 0.0).astype(BF16)
    hi, mid, lo = _split3(imp_ref[...])
    imp = _dot(hi, pool) + (_dot(mid, pool) + _dot(lo, pool))

    blk = lax.broadcasted_iota(jnp.int32, (BLK, n_slc), 1)
    cur = (i * BLK + lax.broadcasted_iota(jnp.int32, (BLK, n_slc), 0)) // NSA_SEL_LEN
    valid = blk <= cur
    forced = (blk == 0) | (blk == cur) | (blk == cur - 1)
    score = jnp.where(valid, imp + jnp.where(forced, NSA_FORCE, 0.0), NEG)
    blk_f = blk.astype(F32)
    chosen = jnp.zeros((BLK, n_slc), F32)
    for _ in range(min(NSA_TOP_N, n_slc)):
        best = jnp.max(score, axis=-1, keepdims=True)
        first = jnp.min(jnp.where(score == best, blk_f, float(n_slc)), axis=-1, keepdims=True)
        pick = blk_f == first
        chosen = jnp.where(pick, 1.0, chosen)
        score = jnp.where(pick, -jnp.inf, score)
    chosen = jnp.where(valid, chosen, 0.0)
    sel_ref[0] = chosen.astype(sel_ref.dtype)

    n_kb = n_slc * NSA_SEL_LEN // BLK
    any_q = jnp.max(chosen, axis=0, keepdims=True)
    any_q = jnp.broadcast_to(any_q, (8, n_slc)).astype(BF16)
    pi = lax.broadcasted_iota(jnp.int32, (n_slc, n_kb), 0)
    pj = lax.broadcasted_iota(jnp.int32, (n_slc, n_kb), 1)
    per_kb = _dot(any_q, jnp.where(pi * NSA_SEL_LEN // BLK == pj, 1.0, 0.0).astype(BF16))
    per_kb = jnp.where(per_kb > 0.5, 1.0, 0.0).astype(BF16)
    n_word = need_ref.shape[-1]
    bi = lax.broadcasted_iota(jnp.int32, (n_kb, n_word), 0)
    bw = lax.broadcasted_iota(jnp.int32, (n_kb, n_word), 1)
    weight = jnp.where(bi // 16 == bw, jnp.left_shift(1, bi % 16), 0).astype(F32).astype(BF16)
    words = _dot(per_kb, weight)
    need_ref[0, 0] = words.astype(jnp.int32)


def _nsa_compressed(z, k_cmp, v_cmp, slopes):
    s = z.shape[0]
    n_slc = s // NSA_SEL_LEN
    g_per = H_NSA // KV_NSA
    qw = g_per * HEAD_DIM
    n_cmp = k_cmp.shape[1]
    n_word = 128
    assert (s // BLK) <= 16 * n_word
    return pl.pallas_call(
        functools.partial(_nsa_cmp_kernel, g_per=g_per, n_slc=n_slc),
        grid=(KV_NSA, s // BLK),
        in_specs=[pl.BlockSpec(memory_space=pltpu.SMEM),
                  pl.BlockSpec((BLK, qw), lambda h, i: (i, h)),
                  pl.BlockSpec((1, n_cmp, HEAD_DIM), lambda h, i: (h, 0, 0)),
                  pl.BlockSpec((1, n_cmp, HEAD_DIM), lambda h, i: (h, 0, 0))],
        out_specs=[pl.BlockSpec((BLK, qw), lambda h, i: (i, h)),
                   pl.BlockSpec((1, BLK, n_slc), lambda h, i: (h, i, 0)),
                   pl.BlockSpec((1, 1, 8, n_word), lambda h, i: (h, i, 0, 0))],
        out_shape=[jax.ShapeDtypeStruct((s, H_NSA * HEAD_DIM), F32),
                   jax.ShapeDtypeStruct((KV_NSA, s, n_slc), BF16),
                   jax.ShapeDtypeStruct((KV_NSA, s // BLK, 8, n_word), jnp.int32)],
        scratch_shapes=[pltpu.VMEM((BLK, n_cmp), F32),
                        pltpu.VMEM((g_per * BLK, HEAD_DIM), BF16),
                        pltpu.VMEM((g_per * BLK, HEAD_DIM), F32)],
        compiler_params=_params("parallel", "arbitrary"),
        name="nsa_compressed",
    )(slopes, z, k_cmp, v_cmp)


def _stack_heads(q_ref, qs_ref, g_per):
    for g in range(g_per):
        qs_ref[g * BLK:(g + 1) * BLK, :] = q_ref[:, g * HEAD_DIM:(g + 1) * HEAD_DIM]


def _online_visit(j, i, qs_ref, k_ref, v_ref, slope_ref, m_ref, l_ref, acc_ref, g_per, key_ok):
    start = pl.multiple_of(j * BLK, BLK)
    k = k_ref[pl.ds(start, BLK), :]
    v_ones = jnp.concatenate([v_ref[pl.ds(start, BLK), :], jnp.ones((BLK, HEAD_DIM), BF16)], axis=1)
    dist = ((i - j) * BLK + lax.broadcasted_iota(jnp.int32, (BLK, BLK), 0)
            - lax.broadcasted_iota(jnp.int32, (BLK, BLK), 1))
    ok = jnp.concatenate([key_ok(dist)] * g_per, axis=0) > 0.5
    dist_f = jnp.concatenate([dist.astype(F32)] * g_per, axis=0)
    s = _dot_nt(qs_ref[...], k) * SCALE - slope_ref[0] * dist_f
    s = jnp.where(ok, s, NEG)
    m_prev = m_ref[...]
    m_new = jnp.maximum(m_prev, jnp.max(s, axis=-1, keepdims=True))
    alpha = jnp.exp(m_prev - m_new)
    p = jnp.where(ok, jnp.exp(s - m_new), 0.0)
    pv = _dot(p.astype(BF16), v_ones)
    l_ref[...] = alpha * l_ref[...] + pv[:, HEAD_DIM:]
    acc_ref[...] = alpha * acc_ref[...] + pv[:, :HEAD_DIM]
    m_ref[...] = m_new


def _online_init(m_ref, l_ref, acc_ref):
    m_ref[...] = jnp.full_like(m_ref, NEG)
    l_ref[...] = jnp.zeros_like(l_ref)
    acc_ref[...] = jnp.zeros_like(acc_ref)


def _nsa_sel_kernel(need_ref, q_ref, sel_ref, slope_ref, k_ref, v_ref, o_ref,
                    qs_ref, m_ref, l_ref, acc_ref, *, g_per, n_qblk, n_word):
    h = pl.program_id(0)
    i = pl.program_id(1)
    _stack_heads(q_ref, qs_ref, g_per)
    _online_init(m_ref, l_ref, acc_ref)
    sel = sel_ref[0]
    n_slc = sel.shape[1]
    per_blk = BLK // NSA_SEL_LEN
    eb = lax.broadcasted_iota(jnp.int32, (n_slc, BLK), 0)
    el = lax.broadcasted_iota(jnp.int32, (n_slc, BLK), 1) // NSA_SEL_LEN
    base = (h * n_qblk + i) * n_word

    def body(j, carry):
        word = need_ref[base + jnp.right_shift(j, 4)]
        needed = jnp.bitwise_and(jnp.right_shift(word, jnp.bitwise_and(j, 15)), 1) == 1

        @pl.when(needed)
        def _():
            expand = jnp.where(eb == per_blk * j + el, 1.0, 0.0).astype(BF16)
            picked = _dot(sel, expand)
            key_ok = lambda dist: jnp.where(dist >= 0, picked, 0.0)
            _online_visit(j, i, qs_ref, k_ref, v_ref, slope_ref, m_ref, l_ref, acc_ref,
                          g_per, key_ok)
        return carry

    lax.fori_loop(0, i + 1, body, 0)
    out = acc_ref[...] / jnp.maximum(l_ref[...], 1e-30)
    for g in range(g_per):
        o_ref[:, g * HEAD_DIM:(g + 1) * HEAD_DIM] = out[g * BLK:(g + 1) * BLK, :]


def _nsa_selected(z, sel, need, slope_rows, *, k_col, v_col):
    s = z.shape[0]
    g_per = H_NSA // KV_NSA
    qw = g_per * HEAD_DIM
    rows = g_per * BLK
    n_slc = sel.shape[-1]
    n_qblk = s // BLK
    n_word = -(-n_qblk // 16)
    need = need[:, :, 0, :n_word]
    return pl.pallas_call(
        functools.partial(_nsa_sel_kernel, g_per=g_per, n_qblk=n_qblk, n_word=n_word),
        grid=(KV_NSA, n_qblk),
        in_specs=[pl.BlockSpec(memory_space=pltpu.SMEM),
                  pl.BlockSpec((BLK, qw), lambda h, i: (i, h)),
                  pl.BlockSpec((1, BLK, n_slc), lambda h, i: (h, i, 0)),
                  pl.BlockSpec((1, rows, HEAD_DIM), lambda h, i: (h, 0, 0)),
                  pl.BlockSpec((s, HEAD_DIM), lambda h, i: (0, k_col + h)),
                  pl.BlockSpec((s, HEAD_DIM), lambda h, i: (0, v_col + h))],
        out_specs=pl.BlockSpec((BLK, qw), lambda h, i: (i, h)),
        out_shape=jax.ShapeDtypeStruct((s, H_NSA * HEAD_DIM), F32),
        scratch_shapes=[pltpu.VMEM((rows, HEAD_DIM), BF16),
                        pltpu.VMEM((rows, HEAD_DIM), F32),
                        pltpu.VMEM((rows, HEAD_DIM), F32),
                        pltpu.VMEM((rows, HEAD_DIM), F32)],
        compiler_params=_params("parallel", "arbitrary"),
        name="nsa_selected",
    )(need.reshape(-1), z, sel, slope_rows, z, z)


def _nsa_win_kernel(q_ref, slope_ref, k_ref, v_ref, gate_ref, ocmp_ref, oslc_ref, o_ref,
                    qs_ref, m_ref, l_ref, acc_ref, *, g_per):
    i = pl.program_id(1)
    _stack_heads(q_ref, qs_ref, g_per)
    _online_init(m_ref, l_ref, acc_ref)
    n_prev = -(-(NSA_WINDOW - 1) // BLK)

    def body(step, carry):
        j = i - n_prev + step

        @pl.when(j >= 0)
        def _():
            key_ok = lambda dist: jnp.where((dist >= 0) & (dist < NSA_WINDOW), 1.0, 0.0)
            _online_visit(j, i, qs_ref, k_ref, v_ref, slope_ref, m_ref, l_ref, acc_ref,
                          g_per, key_ok)
        return carry

    lax.fori_loop(0, n_prev + 1, body, 0)
    o_win = acc_ref[...] / jnp.maximum(l_ref[...], 1e-30)
    gates = jax.nn.sigmoid(gate_ref[...])
    for g in range(g_per):
        hs = slice(g * HEAD_DIM, (g + 1) * HEAD_DIM)
        o = (gates[:, 3 * g:3 * g + 1] * ocmp_ref[:, hs]
             + gates[:, 3 * g + 1:3 * g + 2] * oslc_ref[:, hs]
             + gates[:, 3 * g + 2:3 * g + 3] * o_win[g * BLK:(g + 1) * BLK, :])
        o_ref[:, hs] = o.astype(o_ref.dtype)


def _nsa_window_combine(z, gate_logits, o_cmp, o_slc, slope_rows, *, k_col, v_col):
    s = z.shape[0]
    g_per = H_NSA // KV_NSA
    qw = g_per * HEAD_DIM
    rows = g_per * BLK
    head_blk = lambda h, i: (i, h)
    return pl.pallas_call(
        functools.partial(_nsa_win_kernel, g_per=g_per),
        grid=(KV_NSA, s // BLK),
        in_specs=[pl.BlockSpec((BLK, qw), head_blk),
                  pl.BlockSpec((1, rows, HEAD_DIM), lambda h, i: (h, 0, 0)),
                  pl.BlockSpec((s, HEAD_DIM), lambda h, i: (0, k_col + h)),
                  pl.BlockSpec((s, HEAD_DIM), lambda h, i: (0, v_col + h)),
                  pl.BlockSpec((BLK, 128), head_blk),
                  pl.BlockSpec((BLK, qw), head_blk),
                  pl.BlockSpec((BLK, qw), head_blk)],
        out_specs=pl.BlockSpec((BLK, qw), head_blk),
        out_shape=jax.ShapeDtypeStruct((s, H_NSA * HEAD_DIM), BF16),
        scratch_shapes=[pltpu.VMEM((rows, HEAD_DIM), BF16),
                        pltpu.VMEM((rows, HEAD_DIM), F32),
                        pltpu.VMEM((rows, HEAD_DIM), F32),
                        pltpu.VMEM((rows, HEAD_DIM), F32)],
        compiler_params=_params("parallel", "arbitrary"),
        name="nsa_window_combine",
    )(z, slope_rows, z, z, gate_logits, o_cmp, o_slc)


def _ab_mixer(h, w_in, sinks, w_out, x_res):
    z = _matmul(h, w_in.astype(BF16), BF16, tm=1024, tn=768)
    d_sb = H_SB * HEAD_DIM
    o_a = _stick_breaking(z, n_heads=H_SB, q_col=0, k_col=H_SB, v_col=2 * H_SB)
    q_b = 3 * d_sb
    k_b = q_b + H_SW * HEAD_DIM
    v_b = k_b + KV_SW * HEAD_DIM
    o_b = _window_sink_attention(z, sinks, q_col=q_b, k_col=k_b, v_col=v_b)
    o = jnp.concatenate([o_a, o_b], axis=-1)
    return _matmul(o, w_out.astype(BF16), F32, tm=1024, tn=512, residual=x_res)


def _nsa_mixer(h, w_in, cmp_pos, wk1, wk2, wv1, wv2, w_out, x_res):
    s = h.shape[0]
    dq = H_NSA * HEAD_DIM
    dkv = KV_NSA * HEAD_DIM
    g_per = H_NSA // KV_NSA
    main = dq + 6 * dkv
    z = _matmul(h, w_in[:, :main].astype(BF16), BF16, tm=1024, tn=512)
    w_gate = w_in[:, main:].reshape(-1, KV_NSA, 3 * g_per)
    w_gate = jnp.pad(w_gate, ((0, 0), (0, 0), (0, 128 - 3 * g_per))).reshape(-1, KV_NSA * 128)
    gate_logits = _matmul(h, w_gate.astype(BF16), F32, tm=1024, tn=KV_NSA * 128)

    n_chunk = s // NSA_CMP_STRIDE
    to_chunks = lambda c: (z[:, c:c + dkv].reshape(n_chunk, NSA_CMP_STRIDE, KV_NSA, HEAD_DIM)
                           .transpose(2, 0, 1, 3).reshape(KV_NSA, n_chunk, NSA_CMP_STRIDE * HEAD_DIM))
    chunks = jnp.concatenate([to_chunks(dq), to_chunks(dq + dkv)], axis=0)
    kv_cmp = _compress(chunks, cmp_pos.reshape(1, -1).astype(F32),
                       jnp.stack([wk1, wv1]).astype(BF16), jnp.stack([wk2, wv2]).astype(BF16))
    k_cmp, v_cmp = kv_cmp[:KV_NSA], kv_cmp[KV_NSA:]

    slopes = _alibi_slopes(H_NSA, KV_NSA)
    slope_rows = jnp.asarray(np.repeat(np.repeat(slopes, BLK, axis=1)[..., None], HEAD_DIM, axis=2))
    o_cmp, sel, need = _nsa_compressed(z, k_cmp, v_cmp, jnp.asarray(slopes.reshape(-1)))
    col = lambda off: (dq + off * dkv) // HEAD_DIM
    o_slc = _nsa_selected(z, sel, need, slope_rows, k_col=col(2), v_col=col(3))
    o = _nsa_window_combine(z, gate_logits, o_cmp, o_slc, slope_rows, k_col=col(4), v_col=col(5))
    return _matmul(o, w_out.astype(BF16), F32, tm=1024, tn=512, residual=x_res)


_FF_TILE = 1024


def _swiglu(h, w_gate_up, w_down, x_res):
    d_ff = w_down.shape[0]
    pad = (-d_ff) % _FF_TILE
    cols = lambda w: jnp.pad(w.astype(BF16), ((0, 0), (0, pad)))
    w_gu = jnp.concatenate([cols(w_gate_up[:, :d_ff]), cols(w_gate_up[:, d_ff:])], axis=1)
    w_dn = jnp.pad(w_down.astype(BF16), ((0, pad), (0, 0)))
    act = _gate_up(h, w_gu, tm=1024, tn=256)
    return _matmul(act, w_dn, F32, tm=1024, tn=1024, tk=_FF_TILE, residual=x_res)


def kernel(x, attn_norm, ffn_norm, w_gate_up, w_down, ab_w_in, ab_sinks, ab_w_out, nsa_w_in,
           nsa_cmp_pos, nsa_cmp_wk1, nsa_cmp_wk2, nsa_cmp_wv1, nsa_cmp_wv2, nsa_w_out, final_norm):
    b, s, d = x.shape
    outs = []
    for bi in range(b):
        xb = x[bi]
        for layer in range(attn_norm.shape[0]):
            h = _rmsnorm(xb, attn_norm[layer], BF16)
            if layer % 2 == 0:
                e = layer // 2
                xb = _ab_mixer(h, ab_w_in[e], ab_sinks[e], ab_w_out[e], xb)
            else:
                o = layer // 2
                xb = _nsa_mixer(h, nsa_w_in[o], nsa_cmp_pos[o], nsa_cmp_wk1[o], nsa_cmp_wk2[o],
                                nsa_cmp_wv1[o], nsa_cmp_wv2[o], nsa_w_out[o], xb)
            h = _rmsnorm(xb, ffn_norm[layer], BF16)
            xb = _swiglu(h, w_gate_up[layer], w_down[layer], xb)
        outs.append(_rmsnorm(xb, final_norm, F32))
    return jnp.stack(outs, axis=0)
```

```python
import functools

import numpy as np
import jax
import jax.numpy as jnp
from jax import lax
from jax.experimental import pallas as pl
from jax.experimental.pallas import tpu as pltpu

F32 = jnp.float32
BF16 = jnp.bfloat16

HEAD_DIM = 128
BLK = 128
EPS = 1e-5
NEG = -1e30
SCALE = HEAD_DIM ** -0.5
LOG2E = 1.4426950408889634

H_SB = 8
H_SW = 24
KV_SW = 3
SW_WINDOW = 128

H_NSA = 32
KV_NSA = 2
NSA_CMP_LEN = 32
NSA_CMP_STRIDE = 16
NSA_SEL_LEN = 64
NSA_TOP_N = 8
NSA_WINDOW = 512
NSA_FORCE = 1e4

VMEM_LIMIT_BYTES = 52 * 2 ** 20

_NT = (((1,), (1,)), ((), ()))


def _params(*sem):
    return pltpu.CompilerParams(dimension_semantics=sem,
                                vmem_limit_bytes=VMEM_LIMIT_BYTES)


def _dot(a, b):
    return jnp.dot(a, b, preferred_element_type=F32)


def _dot_nt(a, b):
    return lax.dot_general(a, b, _NT, preferred_element_type=F32)


def _alibi_slopes(n_heads, n_kv):
    s = np.exp2(np.float32(-8.0) * np.arange(1, n_heads + 1, dtype=np.float32)
                / np.float32(n_heads)).astype(np.float32)
    return s.reshape(n_heads // n_kv, n_kv).T


def _rmsnorm_kernel(x_ref, g_ref, o_ref):
    x = x_ref[...]
    ms = jnp.mean(x * x, axis=-1, keepdims=True)
    o_ref[...] = (x * lax.rsqrt(ms + EPS) * g_ref[...]).astype(o_ref.dtype)


def _rmsnorm(x, g, out_dtype, tm=256):
    m, d = x.shape
    return pl.pallas_call(
        _rmsnorm_kernel,
        grid=(m // tm,),
        in_specs=[pl.BlockSpec((tm, d), lambda i: (i, 0)),
                  pl.BlockSpec((1, d), lambda i: (0, 0))],
        out_specs=pl.BlockSpec((tm, d), lambda i: (i, 0)),
        out_shape=jax.ShapeDtypeStruct((m, d), out_dtype),
        compiler_params=_params("parallel"),
        name="rmsnorm",
    )(x, g.reshape(1, d).astype(F32))


def _mm_kernel(*refs, nk, has_res):
    if has_res:
        a_ref, b_ref, r_ref, o_ref = refs[:4]
        scratch = refs[4:]
    else:
        a_ref, b_ref, o_ref = refs[:3]
        r_ref = None
        scratch = refs[3:]

    def finish(acc):
        if has_res:
            acc = acc + r_ref[...]
        o_ref[...] = acc.astype(o_ref.dtype)

    part = _dot(a_ref[...], b_ref[...])
    if nk == 1:
        finish(part)
    else:
        acc_ref, = scratch
        k = pl.program_id(2)

        @pl.when(k == 0)
        def _():
            acc_ref[...] = part

        @pl.when(k > 0)
        def _():
            acc_ref[...] += part

        @pl.when(k == nk - 1)
        def _():
            finish(acc_ref[...])


def _matmul(a, b, out_dtype, *, tm, tn, tk=None, residual=None):
    m, kd = a.shape
    _, n = b.shape
    tk = kd if tk is None else tk
    nk = kd // tk
    assert m % tm == 0 and n % tn == 0 and kd % tk == 0
    in_specs = [pl.BlockSpec((tm, tk), lambda i, j, k: (i, k)),
                pl.BlockSpec((tk, tn), lambda i, j, k: (k, j))]
    args = [a, b]
    if residual is not None:
        in_specs.append(pl.BlockSpec((tm, tn), lambda i, j, k: (i, j)))
        args.append(residual)
    return pl.pallas_call(
        functools.partial(_mm_kernel, nk=nk, has_res=residual is not None),
        grid=(m // tm, n // tn, nk),
        in_specs=in_specs,
        out_specs=pl.BlockSpec((tm, tn), lambda i, j, k: (i, j)),
        out_shape=jax.ShapeDtypeStruct((m, n), out_dtype),
        scratch_shapes=[pltpu.VMEM((tm, tn), F32)] if nk > 1 else [],
        compiler_params=_params("parallel", "parallel", "arbitrary"),
        name="matmul",
    )(*args)


def _gate_up_kernel(a_ref, wg_ref, wu_ref, o_ref):
    a = a_ref[...]
    gate = _dot(a, wg_ref[...])
    up = _dot(a, wu_ref[...])
    o_ref[...] = (gate * jax.nn.sigmoid(gate) * up).astype(o_ref.dtype)


def _gate_up(a, w_gate_up, *, tm, tn):
    m, kd = a.shape
    f = w_gate_up.shape[1] // 2
    nf = f // tn
    assert m % tm == 0 and f % tn == 0
    return pl.pallas_call(
        _gate_up_kernel,
        grid=(m // tm, nf),
        in_specs=[pl.BlockSpec((tm, kd), lambda i, j: (i, 0)),
                  pl.BlockSpec((kd, tn), lambda i, j: (0, j)),
                  pl.BlockSpec((kd, tn), lambda i, j: (0, j + nf))],
        out_specs=pl.BlockSpec((tm, tn), lambda i, j: (i, j)),
        out_shape=jax.ShapeDtypeStruct((m, f), BF16),
        compiler_params=_params("parallel", "parallel"),
        name="gate_up",
    )(a, w_gate_up, w_gate_up)


def _sb_kernel(q_ref, k_ref, v_ref, o_ref, acc_ref, later_ref, z_ref, w_ref, *, tq, tk):
    qi = pl.program_id(1)
    ratio = tq // tk
    assert ratio % 2 == 0
    q = q_ref[...]
    row = lax.broadcasted_iota(jnp.int32, (tk, tk), 0)
    col = lax.broadcasted_iota(jnp.int32, (tk, tk), 1)
    suffix = jnp.where(row >= col, 1.0, 0.0).astype(BF16)
    suffix2 = jnp.concatenate([suffix, suffix], axis=0)
    acc_ref[...] = jnp.zeros_like(acc_ref)
    later_ref[...] = jnp.zeros_like(later_ref)

    def logits(j, r0=0):
        start = pl.multiple_of(j * tk, tk)
        return _dot_nt(q[r0:], k_ref[pl.ds(start, tk), :]) * (-SCALE * LOG2E)

    def weights(j, nz, masked, r0=0):
        neg_abs = lax.bitcast_convert_type(
            lax.bitcast_convert_type(nz, jnp.uint32) | jnp.uint32(0x80000000), F32)
        a = jnp.minimum(nz, 0.0) - jnp.log2(1.0 + jnp.exp2(neg_abs))
        if masked:
            t_pos = qi * tq + r0 + lax.broadcasted_iota(jnp.int32, nz.shape, 0)
            s_pos = j * tk + lax.broadcasted_iota(jnp.int32, nz.shape, 1)
            before = s_pos < t_pos
            a = jnp.where(before, a, 0.0)
        a_hi = a.astype(BF16)
        a_lo = (a - a_hi.astype(F32)).astype(BF16)
        within = _dot(jnp.concatenate([a_hi, a_lo], axis=1), suffix2)
        later = later_ref[r0:, :]
        w = jnp.exp2(within + jnp.concatenate([later] * (tk // HEAD_DIM), axis=1) - nz)
        later_ref[r0:, :] = later + within[:, 0:1]
        if masked:
            w = jnp.where(before, w, 0.0)
        return w.astype(BF16)

    def accumulate(j, w, r0=0):
        start = pl.multiple_of(j * tk, tk)
        acc_ref[r0:, :] += _dot(w, v_ref[pl.ds(start, tk), :])

    n_full = ratio * qi

    for b in reversed(range(ratio)):
        j = n_full + b
        accumulate(j, weights(j, logits(j, b * tk), True, b * tk), b * tk)

    def step(j, slot):
        accumulate(j + 1, w_ref[1 - slot])
        z_ref[1 - slot] = logits(jnp.maximum(j - 1, 0))
        w_ref[slot] = weights(j, z_ref[slot], False)

    def two_steps(s, carry):
        j = n_full - 1 - 2 * s
        step(j, 0)
        step(j - 1, 1)
        return carry

    z_ref[0] = logits(jnp.maximum(n_full - 1, 0))
    w_ref[1] = jnp.zeros((tq, tk), BF16)
    lax.fori_loop(0, n_full // 2, two_steps, 0)
    accumulate(0, w_ref[1])
    o_ref[...] = acc_ref[...].astype(o_ref.dtype)


def _stick_breaking(z, *, n_heads, q_col, k_col, v_col, tq=1024, tk=256):
    s = z.shape[0]
    return pl.pallas_call(
        functools.partial(_sb_kernel, tq=tq, tk=tk),
        grid=(n_heads, s // tq),
        in_specs=[pl.BlockSpec((tq, HEAD_DIM), lambda h, i: (i, q_col + h)),
                  pl.BlockSpec((s, HEAD_DIM), lambda h, i: (0, k_col + h)),
                  pl.BlockSpec((s, HEAD_DIM), lambda h, i: (0, v_col + h))],
        out_specs=pl.BlockSpec((tq, HEAD_DIM), lambda h, i: (i, h)),
        out_shape=jax.ShapeDtypeStruct((s, n_heads * HEAD_DIM), BF16),
        scratch_shapes=[pltpu.VMEM((tq, HEAD_DIM), F32),
                        pltpu.VMEM((tq, HEAD_DIM), F32),
                        pltpu.VMEM((2, tq, tk), F32),
                        pltpu.VMEM((2, tq, tk), BF16)],
        compiler_params=_params("parallel", "arbitrary"),
        name="stick_breaking",
    )(z, z, z)


def _swa_kernel(sink_ref, q_ref, kp_ref, kc_ref, vp_ref, vc_ref, o_ref, *, slopes):
    i = pl.program_id(0)
    n_kv, g_per = slopes.shape
    qq = lax.broadcasted_iota(jnp.int32, (BLK, 2 * BLK), 0)
    kk = lax.broadcasted_iota(jnp.int32, (BLK, 2 * BLK), 1)
    dist = qq + BLK - kk
    first_key = jnp.where(i > 0, 0, BLK)
    mask = (dist >= 0) & (dist < SW_WINDOW) & (kk >= first_key)
    dist_f = dist.astype(F32)
    for h in range(n_kv):
        cs = slice(h * HEAD_DIM, (h + 1) * HEAD_DIM)
        k = jnp.concatenate([kp_ref[:, cs], kc_ref[:, cs]], axis=0)
        v = jnp.concatenate([vp_ref[:, cs], vc_ref[:, cs]], axis=0)
        for g in range(g_per):
            head = h * g_per + g
            hs = slice(head * HEAD_DIM, (head + 1) * HEAD_DIM)
            s = _dot_nt(q_ref[:, hs], k) * SCALE - float(slopes[h, g]) * dist_f
            s = jnp.where(mask, s, NEG)
            sink = sink_ref[head]
            m = jnp.maximum(jnp.max(s, axis=-1, keepdims=True), sink)
            e = jnp.where(mask, jnp.exp(s - m), 0.0)
            denom = jnp.sum(e, axis=-1, keepdims=True) + jnp.exp(sink - m)
            p = e / denom
            o_ref[:, hs] = _dot(p.astype(BF16), v).astype(o_ref.dtype)


def _window_sink_attention(z, sinks, *, q_col, k_col, v_col):
    s = z.shape[0]
    qw = H_SW * HEAD_DIM
    kw = KV_SW * HEAD_DIM
    assert q_col % qw == 0 and k_col % kw == 0 and v_col % kw == 0
    qb, kb, vb = q_col // qw, k_col // kw, v_col // kw
    prev = lambda i: jnp.maximum(i - 1, 0)
    return pl.pallas_call(
        functools.partial(_swa_kernel, slopes=_alibi_slopes(H_SW, KV_SW)),
        grid=(s // BLK,),
        in_specs=[pl.BlockSpec(memory_space=pltpu.SMEM),
                  pl.BlockSpec((BLK, qw), lambda i: (i, qb)),
                  pl.BlockSpec((BLK, kw), lambda i: (prev(i), kb)),
                  pl.BlockSpec((BLK, kw), lambda i: (i, kb)),
                  pl.BlockSpec((BLK, kw), lambda i: (prev(i), vb)),
                  pl.BlockSpec((BLK, kw), lambda i: (i, vb))],
        out_specs=pl.BlockSpec((BLK, qw), lambda i: (i, 0)),
        out_shape=jax.ShapeDtypeStruct((s, qw), BF16),
        compiler_params=_params("parallel"),
        name="window_sink_attention",
    )(sinks.astype(F32), z, z, z, z, z)


def _compress_kernel(x_ref, pos_ref, w1_ref, w2_ref, o_ref):
    x = x_ref[0].astype(F32)
    half = x.shape[1]
    n_chunk = x.shape[0]
    lo = _dot((x + pos_ref[:, :half]).astype(BF16), w1_ref[0, :half, :])
    hi = _dot((x + pos_ref[:, half:]).astype(BF16), w1_ref[0, half:, :])
    pre = lo + pltpu.roll(hi, n_chunk - 1, 0)
    act = jax.nn.gelu(pre, approximate=True)
    out = _dot(act.astype(BF16), w2_ref[0])
    last = lax.broadcasted_iota(jnp.int32, out.shape, 0) == n_chunk - 1
    o_ref[0] = jnp.where(last, 0.0, out).astype(o_ref.dtype)


def _compress(x, pos, w1, w2):
    n, n_chunk, width = x.shape
    hid = w1.shape[-1]
    return pl.pallas_call(
        _compress_kernel,
        grid=(n,),
        in_specs=[pl.BlockSpec((1, n_chunk, width), lambda i: (i, 0, 0)),
                  pl.BlockSpec((1, 2 * width), lambda i: (0, 0)),
                  pl.BlockSpec((1, 2 * width, hid), lambda i: (i // KV_NSA, 0, 0)),
                  pl.BlockSpec((1, hid, HEAD_DIM), lambda i: (i // KV_NSA, 0, 0))],
        out_specs=pl.BlockSpec((1, n_chunk, HEAD_DIM), lambda i: (i, 0, 0)),
        out_shape=jax.ShapeDtypeStruct((n, n_chunk, HEAD_DIM), BF16),
        compiler_params=_params("parallel"),
        name="nsa_compress",
    )(x, pos, w1, w2)


_CMP_CHUNK = 256
_CMP_UNROLL = 4


def _split3(x):
    hi = x.astype(BF16)
    r1 = x - hi.astype(F32)
    mid = r1.astype(BF16)
    lo = (r1 - mid.astype(F32)).astype(BF16)
    return hi, mid, lo


def _nsa_cmp_kernel(slope_ref, q_ref, kc_ref, vc_ref, o_ref, sel_ref, need_ref,
                    imp_ref, qs_ref, os_ref, *, g_per, n_slc):
    h = pl.program_id(0)
    i = pl.program_id(1)
    n_cmp = kc_ref.shape[1]
    _stack_heads(q_ref, qs_ref, g_per)
    imp_ref[...] = jnp.zeros_like(imp_ref)

    def attend(width):
        kc = kc_ref[0, :width, :]
        vc = vc_ref[0, :width, :]
        t = i * BLK + lax.broadcasted_iota(jnp.int32, (BLK, width), 0)
        c_end = (lax.broadcasted_iota(jnp.int32, (BLK, width), 1) * NSA_CMP_STRIDE
                 + (NSA_CMP_LEN - 1))
        dist = t - c_end
        mask = dist >= 0
        dist_f = dist.astype(F32)

        def group(gg, carry):
            for u in range(_CMP_UNROLL):
                g = gg * _CMP_UNROLL + u
                rows = pl.ds(pl.multiple_of(g * BLK, BLK), BLK)
                s = _dot_nt(qs_ref[rows, :], kc) * SCALE - slope_ref[h * g_per + g] * dist_f
                s = jnp.where(mask, s, NEG)
                m = jnp.max(s, axis=-1, keepdims=True)
                e = jnp.where(mask, jnp.exp(s - m), 0.0)
                p = e / jnp.maximum(jnp.sum(e, axis=-1, keepdims=True), 1e-30)
                os_ref[rows, :] = _dot(p.astype(BF16), vc)
                imp_ref[:, :width] += p
            return carry

        lax.fori_loop(0, g_per // _CMP_UNROLL, group, 0)

    n_visible = (BLK * i + BLK - NSA_CMP_LEN) // NSA_CMP_STRIDE + 1
    widths = sorted({min(w, n_cmp) for w in range(_CMP_CHUNK, n_cmp + _CMP_CHUNK, _CMP_CHUNK)})
    case = jnp.minimum((n_visible - 1) // _CMP_CHUNK, len(widths) - 1)
    for idx, width in enumerate(widths):
        pl.when(case == idx)(functools.partial(attend, width))
    for g in range(g_per):
        o_ref[:, g * HEAD_DIM:(g + 1) * HEAD_DIM] = os_ref[g * BLK:(g + 1) * BLK, :]

    r = NSA_SEL_LEN // NSA_CMP_STRIDE
    ci = lax.broadcasted_iota(jnp.int32, (n_cmp, n_slc), 0)
    cj = lax.broadcasted_iota(jnp.int32, (n_cmp, n_slc), 1)
    pool = jnp.where((ci >= r * cj - 1) & (ci <= r * cj + r - 1), 1.0, 0.0).astype(BF16)
    hi, mid, lo = _split3(imp_ref[...])
    imp = _dot(hi, pool) + (_dot(mid, pool) + _dot(lo, pool))

    blk = lax.broadcasted_iota(jnp.int32, (BLK, n_slc), 1)
    cur = (i * BLK + lax.broadcasted_iota(jnp.int32, (BLK, n_slc), 0)) // NSA_SEL_LEN
    valid = blk <= cur
    forced = (blk == 0) | (blk == cur) | (blk == cur - 1)
    score = jnp.where(valid, imp + jnp.where(forced, NSA_FORCE, 0.0), NEG)
    blk_f = blk.astype(F32)
    chosen = jnp.zeros((BLK, n_slc), F32)
    for _ in range(min(NSA_TOP_N, n_slc)):
        best = jnp.max(score, axis=-1, keepdims=True)
        first = jnp.min(jnp.where(score == best, blk_f, float(n_slc)), axis=-1, keepdims=True)
        pick = blk_f == first
        chosen = jnp.where(pick, 1.0, chosen)
        score = jnp.where(pick, -jnp.inf, score)
    chosen = jnp.where(valid, chosen, 0.0)
    sel_ref[0] = chosen.astype(sel_ref.dtype)

    n_kb = n_slc * NSA_SEL_LEN // BLK
    any_q = jnp.max(chosen, axis=0, keepdims=True)
    any_q = jnp.broadcast_to(any_q, (8, n_slc)).astype(BF16)
    pi = lax.broadcasted_iota(jnp.int32, (n_slc, n_kb), 0)
    pj = lax.broadcasted_iota(jnp.int32, (n_slc, n_kb), 1)
    per_kb = _dot(any_q, jnp.where(pi * NSA_SEL_LEN // BLK == pj, 1.0, 0.0).astype(BF16))
    per_kb = jnp.where(per_kb > 0.5, 1.0, 0.0).astype(BF16)
    n_word = need_ref.shape[-1]
    bi = lax.broadcasted_iota(jnp.int32, (n_kb, n_word), 0)
    bw = lax.broadcasted_iota(jnp.int32, (n_kb, n_word), 1)
    weight = jnp.where(bi // 16 == bw, jnp.left_shift(1, bi % 16), 0).astype(F32).astype(BF16)
    words = _dot(per_kb, weight)
    need_ref[0, 0] = words.astype(jnp.int32)


def _nsa_compressed(z, k_cmp, v_cmp, slopes):
    s = z.shape[0]
    n_slc = s // NSA_SEL_LEN
    g_per = H_NSA // KV_NSA
    qw = g_per * HEAD_DIM
    n_cmp = k_cmp.shape[1]
    n_word = 128
    assert (s // BLK) <= 16 * n_word
    return pl.pallas_call(
        functools.partial(_nsa_cmp_kernel, g_per=g_per, n_slc=n_slc),
        grid=(KV_NSA, s // BLK),
        in_specs=[pl.BlockSpec(memory_space=pltpu.SMEM),
                  pl.BlockSpec((BLK, qw), lambda h, i: (i, h)),
                  pl.BlockSpec((1, n_cmp, HEAD_DIM), lambda h, i: (h, 0, 0)),
                  pl.BlockSpec((1, n_cmp, HEAD_DIM), lambda h, i: (h, 0, 0))],
        out_specs=[pl.BlockSpec((BLK, qw), lambda h, i: (i, h)),
                   pl.BlockSpec((1, BLK, n_slc), lambda h, i: (h, i, 0)),
                   pl.BlockSpec((1, 1, 8, n_word), lambda h, i: (h, i, 0, 0))],
        out_shape=[jax.ShapeDtypeStruct((s, H_NSA * HEAD_DIM), F32),
                   jax.ShapeDtypeStruct((KV_NSA, s, n_slc), BF16),
                   jax.ShapeDtypeStruct((KV_NSA, s // BLK, 8, n_word), jnp.int32)],
        scratch_shapes=[pltpu.VMEM((BLK, n_cmp), F32),
                        pltpu.VMEM((g_per * BLK, HEAD_DIM), BF16),
                        pltpu.VMEM((g_per * BLK, HEAD_DIM), F32)],
        compiler_params=_params("parallel", "arbitrary"),
        name="nsa_compressed",
    )(slopes, z, k_cmp, v_cmp)


def _stack_heads(q_ref, qs_ref, g_per):
    for g in range(g_per):
        qs_ref[g * BLK:(g + 1) * BLK, :] = q_ref[:, g * HEAD_DIM:(g + 1) * HEAD_DIM]


_ROW_SPLIT = 2


def _online_visit(i, blocks, qs_ref, k_ref, v_ref, slope_ref, m_ref, l_ref, acc_ref, g_per):
    nb = len(blocks)
    starts = [pl.multiple_of(j * BLK, BLK) for j, _ in blocks]
    k = jnp.concatenate([k_ref[pl.ds(st, BLK), :] for st in starts], axis=0)
    v = jnp.concatenate([v_ref[pl.ds(st, BLK), :] for st in starts], axis=0)
    v_ones = jnp.concatenate([v, jnp.ones((nb * BLK, HEAD_DIM), BF16)], axis=1)
    qq = lax.broadcasted_iota(jnp.int32, (BLK, BLK), 0) - lax.broadcasted_iota(jnp.int32, (BLK, BLK), 1)
    dists = [(i - j) * BLK + qq for j, _ in blocks]
    ok1 = jnp.concatenate([key_ok(d) for (_, key_ok), d in zip(blocks, dists)], axis=1)
    dist1 = jnp.concatenate([d.astype(F32) for d in dists], axis=1)
    g_sub = g_per // _ROW_SPLIT
    ok = jnp.concatenate([ok1] * g_sub, axis=0) > 0.5
    dist_f = jnp.concatenate([dist1] * g_sub, axis=0)

    def rows_update(r, carry):
        rows = pl.ds(pl.multiple_of(r * (g_sub * BLK), g_sub * BLK), g_sub * BLK)
        slope = jnp.concatenate([slope_ref[0, rows, :]] * nb, axis=1)
        s = _dot_nt(qs_ref[rows, :], k) * SCALE - slope * dist_f
        s = jnp.where(ok, s, NEG)
        m_prev = m_ref[rows, :]
        m_new = jnp.maximum(m_prev, jnp.max(s, axis=-1, keepdims=True))
        alpha = jnp.exp(m_prev - m_new)
        p = jnp.where(ok, jnp.exp(s - jnp.concatenate([m_new] * nb, axis=1)), 0.0)
        pv = _dot(p.astype(BF16), v_ones)
        l_ref[rows, :] = alpha * l_ref[rows, :] + pv[:, HEAD_DIM:]
        acc_ref[rows, :] = alpha * acc_ref[rows, :] + pv[:, :HEAD_DIM]
        m_ref[rows, :] = m_new
        return carry

    lax.fori_loop(0, _ROW_SPLIT, rows_update, 0)


def _online_init(m_ref, l_ref, acc_ref):
    m_ref[...] = jnp.full_like(m_ref, NEG)
    l_ref[...] = jnp.zeros_like(l_ref)
    acc_ref[...] = jnp.zeros_like(acc_ref)


_SEL_GROUP = 4


def _nsa_sel_kernel(need_ref, q_ref, sel_ref, slope_ref, k_ref, v_ref, o_ref,
                    qs_ref, m_ref, l_ref, acc_ref, list_ref, *, g_per, n_qblk, n_word):
    h = pl.program_id(0)
    i = pl.program_id(1)
    _stack_heads(q_ref, qs_ref, g_per)
    _online_init(m_ref, l_ref, acc_ref)
    sel = sel_ref[0]
    n_slc = sel.shape[1]
    per_blk = BLK // NSA_SEL_LEN
    eb = lax.broadcasted_iota(jnp.int32, (n_slc, BLK), 0)
    el = lax.broadcasted_iota(jnp.int32, (n_slc, BLK), 1) // NSA_SEL_LEN
    base = (h * n_qblk + i) * n_word

    def scan(j, n):
        word = need_ref[base + jnp.right_shift(j, 4)]
        needed = jnp.bitwise_and(jnp.right_shift(word, jnp.bitwise_and(j, 15)), 1)

        @pl.when(needed == 1)
        def _():
            list_ref[n] = j
        return n + needed

    n_needed = lax.fori_loop(0, i + 1, scan, 0)
    for u in range(_SEL_GROUP):
        list_ref[n_needed + u] = -1

    def group(t, carry):
        blocks = []
        for u in range(_SEL_GROUP):
            j = list_ref[t * _SEL_GROUP + u]
            present = jnp.where(j >= 0, 1.0, 0.0)
            j = jnp.maximum(j, 0)
            expand = jnp.where(eb == per_blk * j + el, present, 0.0).astype(BF16)
            picked = _dot(sel, expand)
            blocks.append((j, functools.partial(
                lambda dist, picked: jnp.where(dist >= 0, picked, 0.0), picked=picked)))
        _online_visit(i, blocks, qs_ref, k_ref, v_ref, slope_ref, m_ref, l_ref, acc_ref, g_per)
        return carry

    lax.fori_loop(0, (n_needed + _SEL_GROUP - 1) // _SEL_GROUP, group, 0)
    out = acc_ref[...] / jnp.maximum(l_ref[...], 1e-30)
    for g in range(g_per):
        o_ref[:, g * HEAD_DIM:(g + 1) * HEAD_DIM] = out[g * BLK:(g + 1) * BLK, :]


def _nsa_selected(z, sel, need, slope_rows, *, k_col, v_col):
    s = z.shape[0]
    g_per = H_NSA // KV_NSA
    qw = g_per * HEAD_DIM
    rows = g_per * BLK
    n_slc = sel.shape[-1]
    n_qblk = s // BLK
    n_word = -(-n_qblk // 16)
    need = need[:, :, 0, :n_word]
    return pl.pallas_call(
        functools.partial(_nsa_sel_kernel, g_per=g_per, n_qblk=n_qblk, n_word=n_word),
        grid=(KV_NSA, n_qblk),
        in_specs=[pl.BlockSpec(memory_space=pltpu.SMEM),
                  pl.BlockSpec((BLK, qw), lambda h, i: (i, h)),
                  pl.BlockSpec((1, BLK, n_slc), lambda h, i: (h, i, 0)),
                  pl.BlockSpec((1, rows, HEAD_DIM), lambda h, i: (h, 0, 0)),
                  pl.BlockSpec((s, HEAD_DIM), lambda h, i: (0, k_col + h)),
                  pl.BlockSpec((s, HEAD_DIM), lambda h, i: (0, v_col + h))],
        out_specs=pl.BlockSpec((BLK, qw), lambda h, i: (i, h)),
        out_shape=jax.ShapeDtypeStruct((s, H_NSA * HEAD_DIM), F32),
        scratch_shapes=[pltpu.VMEM((rows, HEAD_DIM), BF16),
                        pltpu.VMEM((rows, HEAD_DIM), F32),
                        pltpu.VMEM((rows, HEAD_DIM), F32),
                        pltpu.VMEM((rows, HEAD_DIM), F32),
                        pltpu.SMEM((n_qblk + _SEL_GROUP,), jnp.int32)],
        compiler_params=_params("parallel", "arbitrary"),
        name="nsa_selected",
    )(need.reshape(-1), z, sel, slope_rows, z, z)


def _nsa_win_kernel(q_ref, slope_ref, k_ref, v_ref, gate_ref, ocmp_ref, oslc_ref, o_ref,
                    qs_ref, m_ref, l_ref, acc_ref, *, g_per):
    i = pl.program_id(1)
    _stack_heads(q_ref, qs_ref, g_per)
    _online_init(m_ref, l_ref, acc_ref)
    n_prev = -(-(NSA_WINDOW - 1) // BLK)

    blocks = []
    for step in range(n_prev + 1):
        j = i - n_prev + step
        present = jnp.where(j >= 0, 1.0, 0.0)
        blocks.append((jnp.maximum(j, 0), functools.partial(
            lambda dist, present: jnp.where((dist >= 0) & (dist < NSA_WINDOW), present, 0.0),
            present=present)))
    _online_visit(i, blocks, qs_ref, k_ref, v_ref, slope_ref, m_ref, l_ref, acc_ref, g_per)
    o_win = acc_ref[...] / jnp.maximum(l_ref[...], 1e-30)
    gates = jax.nn.sigmoid(gate_ref[...])
    for g in range(g_per):
        hs = slice(g * HEAD_DIM, (g + 1) * HEAD_DIM)
        o = (gates[:, 3 * g:3 * g + 1] * ocmp_ref[:, hs]
             + gates[:, 3 * g + 1:3 * g + 2] * oslc_ref[:, hs]
             + gates[:, 3 * g + 2:3 * g + 3] * o_win[g * BLK:(g + 1) * BLK, :])
        o_ref[:, hs] = o.astype(o_ref.dtype)


def _nsa_window_combine(z, gate_logits, o_cmp, o_slc, slope_rows, *, k_col, v_col):
    s = z.shape[0]
    g_per = H_NSA // KV_NSA
    qw = g_per * HEAD_DIM
    rows = g_per * BLK
    head_blk = lambda h, i: (i, h)
    return pl.pallas_call(
        functools.partial(_nsa_win_kernel, g_per=g_per),
        grid=(KV_NSA, s // BLK),
        in_specs=[pl.BlockSpec((BLK, qw), head_blk),
                  pl.BlockSpec((1, rows, HEAD_DIM), lambda h, i: (h, 0, 0)),
                  pl.BlockSpec((s, HEAD_DIM), lambda h, i: (0, k_col + h)),
                  pl.BlockSpec((s, HEAD_DIM), lambda h, i: (0, v_col + h)),
                  pl.BlockSpec((BLK, 128), head_blk),
                  pl.BlockSpec((BLK, qw), head_blk),
                  pl.BlockSpec((BLK, qw), head_blk)],
        out_specs=pl.BlockSpec((BLK, qw), head_blk),
        out_shape=jax.ShapeDtypeStruct((s, H_NSA * HEAD_DIM), BF16),
        scratch_shapes=[pltpu.VMEM((rows, HEAD_DIM), BF16),
                        pltpu.VMEM((rows, HEAD_DIM), F32),
                        pltpu.VMEM((rows, HEAD_DIM), F32),
                        pltpu.VMEM((rows, HEAD_DIM), F32)],
        compiler_params=_params("parallel", "arbitrary"),
        name="nsa_window_combine",
    )(z, slope_rows, z, z, gate_logits, o_cmp, o_slc)


def _ab_mixer(h, w_in, sinks, w_out, x_res):
    z = _matmul(h, w_in.astype(BF16), BF16, tm=1024, tn=768)
    d_sb = H_SB * HEAD_DIM
    o_a = _stick_breaking(z, n_heads=H_SB, q_col=0, k_col=H_SB, v_col=2 * H_SB)
    q_b = 3 * d_sb
    k_b = q_b + H_SW * HEAD_DIM
    v_b = k_b + KV_SW * HEAD_DIM
    o_b = _window_sink_attention(z, sinks, q_col=q_b, k_col=k_b, v_col=v_b)
    o = jnp.concatenate([o_a, o_b], axis=-1)
    return _matmul(o, w_out.astype(BF16), F32, tm=1024, tn=512, residual=x_res)


def _nsa_mixer(h, w_in, cmp_pos, wk1, wk2, wv1, wv2, w_out, x_res):
    s = h.shape[0]
    dq = H_NSA * HEAD_DIM
    dkv = KV_NSA * HEAD_DIM
    g_per = H_NSA // KV_NSA
    main = dq + 6 * dkv
    z = _matmul(h, w_in[:, :main].astype(BF16), BF16, tm=1024, tn=512)
    w_gate = w_in[:, main:].reshape(-1, KV_NSA, 3 * g_per)
    w_gate = jnp.pad(w_gate, ((0, 0), (0, 0), (0, 128 - 3 * g_per))).reshape(-1, KV_NSA * 128)
    gate_logits = _matmul(h, w_gate.astype(BF16), F32, tm=1024, tn=KV_NSA * 128)

    n_chunk = s // NSA_CMP_STRIDE
    to_chunks = lambda c: (z[:, c:c + dkv].reshape(n_chunk, NSA_CMP_STRIDE, KV_NSA, HEAD_DIM)
                           .transpose(2, 0, 1, 3).reshape(KV_NSA, n_chunk, NSA_CMP_STRIDE * HEAD_DIM))
    chunks = jnp.concatenate([to_chunks(dq), to_chunks(dq + dkv)], axis=0)
    kv_cmp = _compress(chunks, cmp_pos.reshape(1, -1).astype(F32),
                       jnp.stack([wk1, wv1]).astype(BF16), jnp.stack([wk2, wv2]).astype(BF16))
    k_cmp, v_cmp = kv_cmp[:KV_NSA], kv_cmp[KV_NSA:]

    slopes = _alibi_slopes(H_NSA, KV_NSA)
    slope_rows = jnp.asarray(np.repeat(np.repeat(slopes, BLK, axis=1)[..., None], HEAD_DIM, axis=2))
    o_cmp, sel, need = _nsa_compressed(z, k_cmp, v_cmp, jnp.asarray(slopes.reshape(-1)))
    col = lambda off: (dq + off * dkv) // HEAD_DIM
    o_slc = _nsa_selected(z, sel, need, slope_rows, k_col=col(2), v_col=col(3))
    o = _nsa_window_combine(z, gate_logits, o_cmp, o_slc, slope_rows, k_col=col(4), v_col=col(5))
    return _matmul(o, w_out.astype(BF16), F32, tm=1024, tn=512, residual=x_res)


def _swiglu(h, w_gate_up, w_down, x_res):
    act = _gate_up(h, w_gate_up.astype(BF16), tm=1024, tn=256)
    return _matmul(act, w_down.astype(BF16), F32, tm=512, tn=256, residual=x_res)


def kernel(x, attn_norm, ffn_norm, w_gate_up, w_down, ab_w_in, ab_sinks, ab_w_out, nsa_w_in,
           nsa_cmp_pos, nsa_cmp_wk1, nsa_cmp_wk2, nsa_cmp_wv1, nsa_cmp_wv2, nsa_w_out, final_norm):
    b, s, d = x.shape
    outs = []
    for bi in range(b):
        xb = x[bi]
        for layer in range(attn_norm.shape[0]):
            h = _rmsnorm(xb, attn_norm[layer], BF16)
            if layer % 2 == 0:
                e = layer // 2
                xb = _ab_mixer(h, ab_w_in[e], ab_sinks[e], ab_w_out[e], xb)
            else:
                o = layer // 2
                xb = _nsa_mixer(h, nsa_w_in[o], nsa_cmp_pos[o], nsa_cmp_wk1[o], nsa_cmp_wk2[o],
                                nsa_cmp_wv1[o], nsa_cmp_wv2[o], nsa_w_out[o], xb)
            h = _rmsnorm(xb, ffn_norm[layer], BF16)
            xb = _swiglu(h, w_gate_up[layer], w_down[layer], xb)
        outs.append(_rmsnorm(xb, final_norm, F32))
    return jnp.stack(outs, axis=0)
```

```python
import functools

import numpy as np
import jax
import jax.numpy as jnp
from jax import lax
from jax.experimental import pallas as pl
from jax.experimental.pallas import tpu as pltpu

F32 = jnp.float32
BF16 = jnp.bfloat16

HEAD_DIM = 128
BLK = 128
EPS = 1e-5
NEG = -1e30
SCALE = HEAD_DIM ** -0.5
LOG2E = 1.4426950408889634

H_SB = 8
H_SW = 24
KV_SW = 3
SW_WINDOW = 128

H_NSA = 32
KV_NSA = 2
NSA_CMP_LEN = 32
NSA_CMP_STRIDE = 16
NSA_SEL_LEN = 64
NSA_TOP_N = 8
NSA_WINDOW = 512
NSA_FORCE = 1e4

VMEM_LIMIT_BYTES = 52 * 2 ** 20

_NT = (((1,), (1,)), ((), ()))


def _params(*sem):
    return pltpu.CompilerParams(dimension_semantics=sem,
                                vmem_limit_bytes=VMEM_LIMIT_BYTES)


def _dot(a, b):
    return jnp.dot(a, b, preferred_element_type=F32)


def _dot_nt(a, b):
    return lax.dot_general(a, b, _NT, preferred_element_type=F32)


def _alibi_slopes(n_heads, n_kv):
    s = np.exp2(np.float32(-8.0) * np.arange(1, n_heads + 1, dtype=np.float32)
                / np.float32(n_heads)).astype(np.float32)
    return s.reshape(n_heads // n_kv, n_kv).T


def _rmsnorm_kernel(x_ref, g_ref, o_ref):
    x = x_ref[...]
    ms = jnp.mean(x * x, axis=-1, keepdims=True)
    o_ref[...] = (x * lax.rsqrt(ms + EPS) * g_ref[...]).astype(o_ref.dtype)


def _rmsnorm(x, g, out_dtype, tm=256):
    m, d = x.shape
    return pl.pallas_call(
        _rmsnorm_kernel,
        grid=(m // tm,),
        in_specs=[pl.BlockSpec((tm, d), lambda i: (i, 0)),
                  pl.BlockSpec((1, d), lambda i: (0, 0))],
        out_specs=pl.BlockSpec((tm, d), lambda i: (i, 0)),
        out_shape=jax.ShapeDtypeStruct((m, d), out_dtype),
        compiler_params=_params("parallel"),
        name="rmsnorm",
    )(x, g.reshape(1, d).astype(F32))


def _mm_kernel(*refs, nk, has_res):
    if has_res:
        a_ref, b_ref, r_ref, o_ref = refs[:4]
        scratch = refs[4:]
    else:
        a_ref, b_ref, o_ref = refs[:3]
        r_ref = None
        scratch = refs[3:]

    def finish(acc):
        if has_res:
            acc = acc + r_ref[...]
        o_ref[...] = acc.astype(o_ref.dtype)

    part = _dot(a_ref[...], b_ref[...])
    if nk == 1:
        finish(part)
    else:
        acc_ref, = scratch
        k = pl.program_id(2)

        @pl.when(k == 0)
        def _():
            acc_ref[...] = part

        @pl.when(k > 0)
        def _():
            acc_ref[...] += part

        @pl.when(k == nk - 1)
        def _():
            finish(acc_ref[...])


def _matmul(a, b, out_dtype, *, tm, tn, tk=None, residual=None):
    m, kd = a.shape
    _, n = b.shape
    tk = kd if tk is None else tk
    nk = kd // tk
    assert m % tm == 0 and n % tn == 0 and kd % tk == 0
    in_specs = [pl.BlockSpec((tm, tk), lambda i, j, k: (i, k)),
                pl.BlockSpec((tk, tn), lambda i, j, k: (k, j))]
    args = [a, b]
    if residual is not None:
        in_specs.append(pl.BlockSpec((tm, tn), lambda i, j, k: (i, j)))
        args.append(residual)
    return pl.pallas_call(
        functools.partial(_mm_kernel, nk=nk, has_res=residual is not None),
        grid=(m // tm, n // tn, nk),
        in_specs=in_specs,
        out_specs=pl.BlockSpec((tm, tn), lambda i, j, k: (i, j)),
        out_shape=jax.ShapeDtypeStruct((m, n), out_dtype),
        scratch_shapes=[pltpu.VMEM((tm, tn), F32)] if nk > 1 else [],
        compiler_params=_params("parallel", "parallel", "arbitrary"),
        name="matmul",
    )(*args)


def _gate_up_kernel(a_ref, wg_ref, wu_ref, o_ref):
    a = a_ref[...]
    gate = _dot(a, wg_ref[...])
    up = _dot(a, wu_ref[...])
    o_ref[...] = (gate * jax.nn.sigmoid(gate) * up).astype(o_ref.dtype)


def _gate_up(a, w_gate_up, *, tm, tn):
    m, kd = a.shape
    f = w_gate_up.shape[1] // 2
    nf = f // tn
    assert m % tm == 0 and f % tn == 0
    return pl.pallas_call(
        _gate_up_kernel,
        grid=(m // tm, nf),
        in_specs=[pl.BlockSpec((tm, kd), lambda i, j: (i, 0)),
                  pl.BlockSpec((kd, tn), lambda i, j: (0, j)),
                  pl.BlockSpec((kd, tn), lambda i, j: (0, j + nf))],
        out_specs=pl.BlockSpec((tm, tn), lambda i, j: (i, j)),
        out_shape=jax.ShapeDtypeStruct((m, f), BF16),
        compiler_params=_params("parallel", "parallel"),
        name="gate_up",
    )(a, w_gate_up, w_gate_up)


def _sb_kernel(q_ref, k_ref, v_ref, o_ref, acc_ref, later_ref, z_ref, w_ref, *, tq, tk):
    qi = pl.program_id(1)
    ratio = tq // tk
    assert ratio % 2 == 0
    q = q_ref[...]
    row = lax.broadcasted_iota(jnp.int32, (tk, tk), 0)
    col = lax.broadcasted_iota(jnp.int32, (tk, tk), 1)
    suffix = jnp.where(row >= col, 1.0, 0.0).astype(BF16)
    acc_ref[...] = jnp.zeros_like(acc_ref)
    later_ref[...] = jnp.zeros_like(later_ref)

    def logits(j, r0=0):
        start = pl.multiple_of(j * tk, tk)
        return _dot_nt(q[r0:], k_ref[pl.ds(start, tk), :]) * (-SCALE * LOG2E)

    def weights(j, nz, masked, r0=0):
        neg_abs = lax.bitcast_convert_type(
            lax.bitcast_convert_type(nz, jnp.uint32) | jnp.uint32(0x80000000), F32)
        a = jnp.minimum(nz, 0.0) - jnp.log2(1.0 + jnp.exp2(neg_abs))
        if masked:
            t_pos = qi * tq + r0 + lax.broadcasted_iota(jnp.int32, nz.shape, 0)
            s_pos = j * tk + lax.broadcasted_iota(jnp.int32, nz.shape, 1)
            before = s_pos < t_pos
            a = jnp.where(before, a, 0.0)
        within = _dot(a.astype(BF16), suffix)
        later = later_ref[r0:, :]
        w = jnp.exp2(within + jnp.concatenate([later] * (tk // HEAD_DIM), axis=1) - nz)
        later_ref[r0:, :] = later + within[:, 0:1]
        if masked:
            w = jnp.where(before, w, 0.0)
        return w.astype(BF16)

    def accumulate(j, w, r0=0):
        start = pl.multiple_of(j * tk, tk)
        acc_ref[r0:, :] += _dot(w, v_ref[pl.ds(start, tk), :])

    n_full = ratio * qi

    for b in reversed(range(ratio)):
        j = n_full + b
        accumulate(j, weights(j, logits(j, b * tk), True, b * tk), b * tk)

    def step(j, slot):
        accumulate(j + 1, w_ref[1 - slot])
        z_ref[1 - slot] = logits(jnp.maximum(j - 1, 0))
        w_ref[slot] = weights(j, z_ref[slot], False)

    def two_steps(s, carry):
        j = n_full - 1 - 2 * s
        step(j, 0)
        step(j - 1, 1)
        return carry

    z_ref[0] = logits(jnp.maximum(n_full - 1, 0))
    w_ref[1] = jnp.zeros((tq, tk), BF16)
    lax.fori_loop(0, n_full // 2, two_steps, 0)
    accumulate(0, w_ref[1])
    o_ref[...] = acc_ref[...].astype(o_ref.dtype)


def _stick_breaking(z, *, n_heads, q_col, k_col, v_col, tq=1024, tk=256):
    s = z.shape[0]
    return pl.pallas_call(
        functools.partial(_sb_kernel, tq=tq, tk=tk),
        grid=(n_heads, s // tq),
        in_specs=[pl.BlockSpec((tq, HEAD_DIM), lambda h, i: (i, q_col + h)),
                  pl.BlockSpec((s, HEAD_DIM), lambda h, i: (0, k_col + h)),
                  pl.BlockSpec((s, HEAD_DIM), lambda h, i: (0, v_col + h))],
        out_specs=pl.BlockSpec((tq, HEAD_DIM), lambda h, i: (i, h)),
        out_shape=jax.ShapeDtypeStruct((s, n_heads * HEAD_DIM), BF16),
        scratch_shapes=[pltpu.VMEM((tq, HEAD_DIM), F32),
                        pltpu.VMEM((tq, HEAD_DIM), F32),
                        pltpu.VMEM((2, tq, tk), F32),
                        pltpu.VMEM((2, tq, tk), BF16)],
        compiler_params=_params("parallel", "arbitrary"),
        name="stick_breaking",
    )(z, z, z)


def _swa_kernel(sink_ref, q_ref, kp_ref, kc_ref, vp_ref, vc_ref, o_ref, *, slopes):
    i = pl.program_id(0)
    n_kv, g_per = slopes.shape
    qq = lax.broadcasted_iota(jnp.int32, (BLK, 2 * BLK), 0)
    kk = lax.broadcasted_iota(jnp.int32, (BLK, 2 * BLK), 1)
    dist = qq + BLK - kk
    first_key = jnp.where(i > 0, 0, BLK)
    mask = (dist >= 0) & (dist < SW_WINDOW) & (kk >= first_key)
    dist_f = dist.astype(F32)
    for h in range(n_kv):
        cs = slice(h * HEAD_DIM, (h + 1) * HEAD_DIM)
        k = jnp.concatenate([kp_ref[:, cs], kc_ref[:, cs]], axis=0)
        v = jnp.concatenate([vp_ref[:, cs], vc_ref[:, cs]], axis=0)
        for g in range(g_per):
            head = h * g_per + g
            hs = slice(head * HEAD_DIM, (head + 1) * HEAD_DIM)
            s = _dot_nt(q_ref[:, hs], k) * SCALE - float(slopes[h, g]) * dist_f
            s = jnp.where(mask, s, NEG)
            sink = sink_ref[head]
            m = jnp.maximum(jnp.max(s, axis=-1, keepdims=True), sink)
            e = jnp.where(mask, jnp.exp(s - m), 0.0)
            denom = jnp.sum(e, axis=-1, keepdims=True) + jnp.exp(sink - m)
            p = e / denom
            o_ref[:, hs] = _dot(p.astype(BF16), v).astype(o_ref.dtype)


def _window_sink_attention(z, sinks, *, q_col, k_col, v_col):
    s = z.shape[0]
    qw = H_SW * HEAD_DIM
    kw = KV_SW * HEAD_DIM
    assert q_col % qw == 0 and k_col % kw == 0 and v_col % kw == 0
    qb, kb, vb = q_col // qw, k_col // kw, v_col // kw
    prev = lambda i: jnp.maximum(i - 1, 0)
    return pl.pallas_call(
        functools.partial(_swa_kernel, slopes=_alibi_slopes(H_SW, KV_SW)),
        grid=(s // BLK,),
        in_specs=[pl.BlockSpec(memory_space=pltpu.SMEM),
                  pl.BlockSpec((BLK, qw), lambda i: (i, qb)),
                  pl.BlockSpec((BLK, kw), lambda i: (prev(i), kb)),
                  pl.BlockSpec((BLK, kw), lambda i: (i, kb)),
                  pl.BlockSpec((BLK, kw), lambda i: (prev(i), vb)),
                  pl.BlockSpec((BLK, kw), lambda i: (i, vb))],
        out_specs=pl.BlockSpec((BLK, qw), lambda i: (i, 0)),
        out_shape=jax.ShapeDtypeStruct((s, qw), BF16),
        compiler_params=_params("parallel"),
        name="window_sink_attention",
    )(sinks.astype(F32), z, z, z, z, z)


def _compress_kernel(x_ref, pos_ref, w1_ref, w2_ref, o_ref):
    x = x_ref[0].astype(F32)
    half = x.shape[1]
    n_chunk = x.shape[0]
    lo = _dot((x + pos_ref[:, :half]).astype(BF16), w1_ref[0, :half, :])
    hi = _dot((x + pos_ref[:, half:]).astype(BF16), w1_ref[0, half:, :])
    pre = lo + pltpu.roll(hi, n_chunk - 1, 0)
    act = jax.nn.gelu(pre, approximate=True)
    out = _dot(act.astype(BF16), w2_ref[0])
    last = lax.broadcasted_iota(jnp.int32, out.shape, 0) == n_chunk - 1
    o_ref[0] = jnp.where(last, 0.0, out).astype(o_ref.dtype)


def _compress(x, pos, w1, w2):
    n, n_chunk, width = x.shape
    hid = w1.shape[-1]
    return pl.pallas_call(
        _compress_kernel,
        grid=(n,),
        in_specs=[pl.BlockSpec((1, n_chunk, width), lambda i: (i, 0, 0)),
                  pl.BlockSpec((1, 2 * width), lambda i: (0, 0)),
                  pl.BlockSpec((1, 2 * width, hid), lambda i: (i // KV_NSA, 0, 0)),
                  pl.BlockSpec((1, hid, HEAD_DIM), lambda i: (i // KV_NSA, 0, 0))],
        out_specs=pl.BlockSpec((1, n_chunk, HEAD_DIM), lambda i: (i, 0, 0)),
        out_shape=jax.ShapeDtypeStruct((n, n_chunk, HEAD_DIM), BF16),
        compiler_params=_params("parallel"),
        name="nsa_compress",
    )(x, pos, w1, w2)


_CMP_CHUNK = 256
_CMP_UNROLL = 4


def _split3(x):
    hi = x.astype(BF16)
    r1 = x - hi.astype(F32)
    mid = r1.astype(BF16)
    lo = (r1 - mid.astype(F32)).astype(BF16)
    return hi, mid, lo


def _nsa_cmp_kernel(slope_ref, q_ref, kc_ref, vc_ref, o_ref, sel_ref, need_ref,
                    imp_ref, qs_ref, os_ref, *, g_per, n_slc):
    h = pl.program_id(0)
    i = pl.program_id(1)
    n_cmp = kc_ref.shape[1]
    _stack_heads(q_ref, qs_ref, g_per)
    imp_ref[...] = jnp.zeros_like(imp_ref)

    def attend(width):
        kc = kc_ref[0, :width, :]
        vc = vc_ref[0, :width, :]
        t = i * BLK + lax.broadcasted_iota(jnp.int32, (BLK, width), 0)
        c_end = (lax.broadcasted_iota(jnp.int32, (BLK, width), 1) * NSA_CMP_STRIDE
                 + (NSA_CMP_LEN - 1))
        dist = t - c_end
        mask = dist >= 0
        dist_f = dist.astype(F32)

        def group(gg, carry):
            heads = [gg * _CMP_UNROLL + u for u in range(_CMP_UNROLL)]
            rows = [pl.ds(pl.multiple_of(g * BLK, BLK), BLK) for g in heads]
            s = [_dot_nt(qs_ref[r, :], kc) for r in rows]
            s = [jnp.where(mask, x * SCALE - slope_ref[h * g_per + g] * dist_f, NEG)
                 for x, g in zip(s, heads)]
            m = [jnp.max(x, axis=-1, keepdims=True) for x in s]
            e = [jnp.where(mask, jnp.exp(x - mx), 0.0) for x, mx in zip(s, m)]
            p = [x / jnp.maximum(jnp.sum(x, axis=-1, keepdims=True), 1e-30) for x in e]
            for r, x in zip(rows, p):
                os_ref[r, :] = _dot(x.astype(BF16), vc)
            total = p[0]
            for x in p[1:]:
                total = total + x
            imp_ref[:, :width] += total
            return carry

        lax.fori_loop(0, g_per // _CMP_UNROLL, group, 0)

    n_visible = (BLK * i + BLK - NSA_CMP_LEN) // NSA_CMP_STRIDE + 1
    widths = sorted({min(w, n_cmp) for w in range(_CMP_CHUNK, n_cmp + _CMP_CHUNK, _CMP_CHUNK)})
    case = jnp.minimum((n_visible - 1) // _CMP_CHUNK, len(widths) - 1)
    for idx, width in enumerate(widths):
        pl.when(case == idx)(functools.partial(attend, width))
    for g in range(g_per):
        o_ref[:, g * HEAD_DIM:(g + 1) * HEAD_DIM] = os_ref[g * BLK:(g + 1) * BLK, :]

    r = NSA_SEL_LEN // NSA_CMP_STRIDE
    ci = lax.broadcasted_iota(jnp.int32, (n_cmp, n_slc), 0)
    cj = lax.broadcasted_iota(jnp.int32, (n_cmp, n_slc), 1)
    pool = jnp.where((ci >= r * cj - 1) & (ci <= r * cj + r - 1), 1.0, 0.0).astype(BF16)
    hi, mid, lo = _split3(imp_ref[...])
    imp = _dot(hi, pool) + (_dot(mid, pool) + _dot(lo, pool))

    blk = lax.broadcasted_iota(jnp.int32, (BLK, n_slc), 1)
    cur = (i * BLK + lax.broadcasted_iota(jnp.int32, (BLK, n_slc), 0)) // NSA_SEL_LEN
    valid = blk <= cur
    forced = (blk == 0) | (blk == cur) | (blk == cur - 1)
    score = jnp.where(valid, imp + jnp.where(forced, NSA_FORCE, 0.0), NEG)
    blk_f = blk.astype(F32)
    chosen = jnp.zeros((BLK, n_slc), F32)
    for _ in range(min(NSA_TOP_N, n_slc)):
        best = jnp.max(score, axis=-1, keepdims=True)
        first = jnp.min(jnp.where(score == best, blk_f, float(n_slc)), axis=-1, keepdims=True)
        pick = blk_f == first
        chosen = jnp.where(pick, 1.0, chosen)
        score = jnp.where(pick, -jnp.inf, score)
    chosen = jnp.where(valid, chosen, 0.0)
    sel_ref[0] = chosen.astype(sel_ref.dtype)

    n_kb = n_slc * NSA_SEL_LEN // BLK
    any_q = jnp.max(chosen, axis=0, keepdims=True)
    any_q = jnp.broadcast_to(any_q, (8, n_slc)).astype(BF16)
    pi = lax.broadcasted_iota(jnp.int32, (n_slc, n_kb), 0)
    pj = lax.broadcasted_iota(jnp.int32, (n_slc, n_kb), 1)
    per_kb = _dot(any_q, jnp.where(pi * NSA_SEL_LEN // BLK == pj, 1.0, 0.0).astype(BF16))
    per_kb = jnp.where(per_kb > 0.5, 1.0, 0.0).astype(BF16)
    n_word = need_ref.shape[-1]
    bi = lax.broadcasted_iota(jnp.int32, (n_kb, n_word), 0)
    bw = lax.broadcasted_iota(jnp.int32, (n_kb, n_word), 1)
    weight = jnp.where(bi // 16 == bw, jnp.left_shift(1, bi % 16), 0).astype(F32).astype(BF16)
    words = _dot(per_kb, weight)
    need_ref[0, 0] = words.astype(jnp.int32)


def _nsa_compressed(z, k_cmp, v_cmp, slopes):
    s = z.shape[0]
    n_slc = s // NSA_SEL_LEN
    g_per = H_NSA // KV_NSA
    qw = g_per * HEAD_DIM
    n_cmp = k_cmp.shape[1]
    n_word = 128
    assert (s // BLK) <= 16 * n_word
    return pl.pallas_call(
        functools.partial(_nsa_cmp_kernel, g_per=g_per, n_slc=n_slc),
        grid=(KV_NSA, s // BLK),
        in_specs=[pl.BlockSpec(memory_space=pltpu.SMEM),
                  pl.BlockSpec((BLK, qw), lambda h, i: (i, h)),
                  pl.BlockSpec((1, n_cmp, HEAD_DIM), lambda h, i: (h, 0, 0)),
                  pl.BlockSpec((1, n_cmp, HEAD_DIM), lambda h, i: (h, 0, 0))],
        out_specs=[pl.BlockSpec((BLK, qw), lambda h, i: (i, h)),
                   pl.BlockSpec((1, BLK, n_slc), lambda h, i: (h, i, 0)),
                   pl.BlockSpec((1, 1, 8, n_word), lambda h, i: (h, i, 0, 0))],
        out_shape=[jax.ShapeDtypeStruct((s, H_NSA * HEAD_DIM), F32),
                   jax.ShapeDtypeStruct((KV_NSA, s, n_slc), BF16),
                   jax.ShapeDtypeStruct((KV_NSA, s // BLK, 8, n_word), jnp.int32)],
        scratch_shapes=[pltpu.VMEM((BLK, n_cmp), F32),
                        pltpu.VMEM((g_per * BLK, HEAD_DIM), BF16),
                        pltpu.VMEM((g_per * BLK, HEAD_DIM), F32)],
        compiler_params=_params("parallel", "arbitrary"),
        name="nsa_compressed",
    )(slopes, z, k_cmp, v_cmp)


def _stack_heads(q_ref, qs_ref, g_per):
    for g in range(g_per):
        qs_ref[g * BLK:(g + 1) * BLK, :] = q_ref[:, g * HEAD_DIM:(g + 1) * HEAD_DIM]


_ROW_SPLIT = 2


def _online_visit(i, blocks, qs_ref, k_ref, v_ref, slope_ref, m_ref, l_ref, acc_ref, g_per):
    nb = len(blocks)
    starts = [pl.multiple_of(j * BLK, BLK) for j, _ in blocks]
    k = jnp.concatenate([k_ref[pl.ds(st, BLK), :] for st in starts], axis=0)
    v = jnp.concatenate([v_ref[pl.ds(st, BLK), :] for st in starts], axis=0)
    v_ones = jnp.concatenate([v, jnp.ones((nb * BLK, HEAD_DIM), BF16)], axis=1)
    qq = lax.broadcasted_iota(jnp.int32, (BLK, BLK), 0) - lax.broadcasted_iota(jnp.int32, (BLK, BLK), 1)
    dists = [(i - j) * BLK + qq for j, _ in blocks]
    ok1 = jnp.concatenate([key_ok(d) for (_, key_ok), d in zip(blocks, dists)], axis=1)
    dist1 = jnp.concatenate([d.astype(F32) for d in dists], axis=1)
    g_sub = g_per // _ROW_SPLIT
    ok = jnp.concatenate([ok1] * g_sub, axis=0) > 0.5
    dist_f = jnp.concatenate([dist1] * g_sub, axis=0)

    groups = [slice(r * g_sub * BLK, (r + 1) * g_sub * BLK) for r in range(_ROW_SPLIT)]
    s = [_dot_nt(qs_ref[rows, :], k) for rows in groups]
    s = [jnp.where(ok, x * SCALE - jnp.concatenate([slope_ref[0, rows, :]] * nb, axis=1) * dist_f, NEG)
         for x, rows in zip(s, groups)]
    m_prev = [m_ref[rows, :] for rows in groups]
    m_new = [jnp.maximum(mp, jnp.max(x, axis=-1, keepdims=True)) for mp, x in zip(m_prev, s)]
    alpha = [jnp.exp(mp - mn) for mp, mn in zip(m_prev, m_new)]
    p = [jnp.where(ok, jnp.exp(x - jnp.concatenate([mn] * nb, axis=1)), 0.0) for x, mn in zip(s, m_new)]
    pv = [_dot(x.astype(BF16), v_ones) for x in p]
    for rows, a, y, mn in zip(groups, alpha, pv, m_new):
        l_ref[rows, :] = a * l_ref[rows, :] + y[:, HEAD_DIM:]
        acc_ref[rows, :] = a * acc_ref[rows, :] + y[:, :HEAD_DIM]
        m_ref[rows, :] = mn


def _online_init(m_ref, l_ref, acc_ref):
    m_ref[...] = jnp.full_like(m_ref, NEG)
    l_ref[...] = jnp.zeros_like(l_ref)
    acc_ref[...] = jnp.zeros_like(acc_ref)


_SEL_GROUP = 4


def _nsa_sel_kernel(need_ref, q_ref, sel_ref, slope_ref, k_ref, v_ref, o_ref,
                    qs_ref, m_ref, l_ref, acc_ref, list_ref, *, g_per, n_qblk, n_word):
    h = pl.program_id(0)
    i = pl.program_id(1)
    _stack_heads(q_ref, qs_ref, g_per)
    _online_init(m_ref, l_ref, acc_ref)
    sel = sel_ref[0]
    n_slc = sel.shape[1]
    per_blk = BLK // NSA_SEL_LEN
    eb = lax.broadcasted_iota(jnp.int32, (n_slc, BLK), 0)
    el = lax.broadcasted_iota(jnp.int32, (n_slc, BLK), 1) // NSA_SEL_LEN
    base = (h * n_qblk + i) * n_word

    def scan(j, n):
        word = need_ref[base + jnp.right_shift(j, 4)]
        needed = jnp.bitwise_and(jnp.right_shift(word, jnp.bitwise_and(j, 15)), 1)

        @pl.when(needed == 1)
        def _():
            list_ref[n] = j
        return n + needed

    n_needed = lax.fori_loop(0, i + 1, scan, 0)
    for u in range(_SEL_GROUP):
        list_ref[n_needed + u] = -1

    def group(t, carry):
        blocks = []
        for u in range(_SEL_GROUP):
            j = list_ref[t * _SEL_GROUP + u]
            present = jnp.where(j >= 0, 1.0, 0.0)
            j = jnp.maximum(j, 0)
            expand = jnp.where(eb == per_blk * j + el, present, 0.0).astype(BF16)
            picked = _dot(sel, expand)
            blocks.append((j, functools.partial(
                lambda dist, picked: jnp.where(dist >= 0, picked, 0.0), picked=picked)))
        _online_visit(i, blocks, qs_ref, k_ref, v_ref, slope_ref, m_ref, l_ref, acc_ref, g_per)
        return carry

    lax.fori_loop(0, (n_needed + _SEL_GROUP - 1) // _SEL_GROUP, group, 0)
    out = acc_ref[...] / jnp.maximum(l_ref[...], 1e-30)
    for g in range(g_per):
        o_ref[:, g * HEAD_DIM:(g + 1) * HEAD_DIM] = out[g * BLK:(g + 1) * BLK, :]


def _nsa_selected(z, sel, need, slope_rows, *, k_col, v_col):
    s = z.shape[0]
    g_per = H_NSA // KV_NSA
    qw = g_per * HEAD_DIM
    rows = g_per * BLK
    n_slc = sel.shape[-1]
    n_qblk = s // BLK
    n_word = -(-n_qblk // 16)
    need = need[:, :, 0, :n_word]
    return pl.pallas_call(
        functools.partial(_nsa_sel_kernel, g_per=g_per, n_qblk=n_qblk, n_word=n_word),
        grid=(KV_NSA, n_qblk),
        in_specs=[pl.BlockSpec(memory_space=pltpu.SMEM),
                  pl.BlockSpec((BLK, qw), lambda h, i: (i, h)),
                  pl.BlockSpec((1, BLK, n_slc), lambda h, i: (h, i, 0)),
                  pl.BlockSpec((1, rows, HEAD_DIM), lambda h, i: (h, 0, 0)),
                  pl.BlockSpec((s, HEAD_DIM), lambda h, i: (0, k_col + h)),
                  pl.BlockSpec((s, HEAD_DIM), lambda h, i: (0, v_col + h))],
        out_specs=pl.BlockSpec((BLK, qw), lambda h, i: (i, h)),
        out_shape=jax.ShapeDtypeStruct((s, H_NSA * HEAD_DIM), F32),
        scratch_shapes=[pltpu.VMEM((rows, HEAD_DIM), BF16),
                        pltpu.VMEM((rows, HEAD_DIM), F32),
                        pltpu.VMEM((rows, HEAD_DIM), F32),
                        pltpu.VMEM((rows, HEAD_DIM), F32),
                        pltpu.SMEM((n_qblk + _SEL_GROUP,), jnp.int32)],
        compiler_params=_params("parallel", "arbitrary"),
        name="nsa_selected",
    )(need.reshape(-1), z, sel, slope_rows, z, z)


def _nsa_win_kernel(q_ref, slope_ref, k_ref, v_ref, gate_ref, ocmp_ref, oslc_ref, o_ref,
                    qs_ref, m_ref, l_ref, acc_ref, *, g_per):
    i = pl.program_id(1)
    _stack_heads(q_ref, qs_ref, g_per)
    _online_init(m_ref, l_ref, acc_ref)
    n_prev = -(-(NSA_WINDOW - 1) // BLK)

    blocks = []
    for step in range(n_prev + 1):
        j = i - n_prev + step
        present = jnp.where(j >= 0, 1.0, 0.0)
        blocks.append((jnp.maximum(j, 0), functools.partial(
            lambda dist, present: jnp.where((dist >= 0) & (dist < NSA_WINDOW), present, 0.0),
            present=present)))
    _online_visit(i, blocks, qs_ref, k_ref, v_ref, slope_ref, m_ref, l_ref, acc_ref, g_per)
    o_win = acc_ref[...] / jnp.maximum(l_ref[...], 1e-30)
    gates = jax.nn.sigmoid(gate_ref[...])
    for g in range(g_per):
        hs = slice(g * HEAD_DIM, (g + 1) * HEAD_DIM)
        o = (gates[:, 3 * g:3 * g + 1] * ocmp_ref[:, hs]
             + gates[:, 3 * g + 1:3 * g + 2] * oslc_ref[:, hs]
             + gates[:, 3 * g + 2:3 * g + 3] * o_win[g * BLK:(g + 1) * BLK, :])
        o_ref[:, hs] = o.astype(o_ref.dtype)


def _nsa_window_combine(z, gate_logits, o_cmp, o_slc, slope_rows, *, k_col, v_col):
    s = z.shape[0]
    g_per = H_NSA // KV_NSA
    qw = g_per * HEAD_DIM
    rows = g_per * BLK
    head_blk = lambda h, i: (i, h)
    return pl.pallas_call(
        functools.partial(_nsa_win_kernel, g_per=g_per),
        grid=(KV_NSA, s // BLK),
        in_specs=[pl.BlockSpec((BLK, qw), head_blk),
                  pl.BlockSpec((1, rows, HEAD_DIM), lambda h, i: (h, 0, 0)),
                  pl.BlockSpec((s, HEAD_DIM), lambda h, i: (0, k_col + h)),
                  pl.BlockSpec((s, HEAD_DIM), lambda h, i: (0, v_col + h)),
                  pl.BlockSpec((BLK, 128), head_blk),
                  pl.BlockSpec((BLK, qw), head_blk),
                  pl.BlockSpec((BLK, qw), head_blk)],
        out_specs=pl.BlockSpec((BLK, qw), head_blk),
        out_shape=jax.ShapeDtypeStruct((s, H_NSA * HEAD_DIM), BF16),
        scratch_shapes=[pltpu.VMEM((rows, HEAD_DIM), BF16),
                        pltpu.VMEM((rows, HEAD_DIM), F32),
                        pltpu.VMEM((rows, HEAD_DIM), F32),
                        pltpu.VMEM((rows, HEAD_DIM), F32)],
        compiler_params=_params("parallel", "arbitrary"),
        name="nsa_window_combine",
    )(z, slope_rows, z, z, gate_logits, o_cmp, o_slc)


def _ab_mixer(h, w_in, sinks, w_out, x_res):
    z = _matmul(h, w_in.astype(BF16), BF16, tm=1024, tn=768)
    d_sb = H_SB * HEAD_DIM
    o_a = _stick_breaking(z, n_heads=H_SB, q_col=0, k_col=H_SB, v_col=2 * H_SB)
    q_b = 3 * d_sb
    k_b = q_b + H_SW * HEAD_DIM
    v_b = k_b + KV_SW * HEAD_DIM
    o_b = _window_sink_attention(z, sinks, q_col=q_b, k_col=k_b, v_col=v_b)
    o = jnp.concatenate([o_a, o_b], axis=-1)
    return _matmul(o, w_out.astype(BF16), F32, tm=1024, tn=512, residual=x_res)


def _nsa_mixer(h, w_in, cmp_pos, wk1, wk2, wv1, wv2, w_out, x_res):
    s = h.shape[0]
    dq = H_NSA * HEAD_DIM
    dkv = KV_NSA * HEAD_DIM
    g_per = H_NSA // KV_NSA
    main = dq + 6 * dkv
    z = _matmul(h, w_in[:, :main].astype(BF16), BF16, tm=1024, tn=512)
    w_gate = w_in[:, main:].reshape(-1, KV_NSA, 3 * g_per)
    w_gate = jnp.pad(w_gate, ((0, 0), (0, 0), (0, 128 - 3 * g_per))).reshape(-1, KV_NSA * 128)
    gate_logits = _matmul(h, w_gate.astype(BF16), F32, tm=1024, tn=KV_NSA * 128)

    n_chunk = s // NSA_CMP_STRIDE
    to_chunks = lambda c: (z[:, c:c + dkv].reshape(n_chunk, NSA_CMP_STRIDE, KV_NSA, HEAD_DIM)
                           .transpose(2, 0, 1, 3).reshape(KV_NSA, n_chunk, NSA_CMP_STRIDE * HEAD_DIM))
    chunks = jnp.concatenate([to_chunks(dq), to_chunks(dq + dkv)], axis=0)
    kv_cmp = _compress(chunks, cmp_pos.reshape(1, -1).astype(F32),
                       jnp.stack([wk1, wv1]).astype(BF16), jnp.stack([wk2, wv2]).astype(BF16))
    k_cmp, v_cmp = kv_cmp[:KV_NSA], kv_cmp[KV_NSA:]

    slopes = _alibi_slopes(H_NSA, KV_NSA)
    slope_rows = jnp.asarray(np.repeat(np.repeat(slopes, BLK, axis=1)[..., None], HEAD_DIM, axis=2))
    o_cmp, sel, need = _nsa_compressed(z, k_cmp, v_cmp, jnp.asarray(slopes.reshape(-1)))
    col = lambda off: (dq + off * dkv) // HEAD_DIM
    o_slc = _nsa_selected(z, sel, need, slope_rows, k_col=col(2), v_col=col(3))
    o = _nsa_window_combine(z, gate_logits, o_cmp, o_slc, slope_rows, k_col=col(4), v_col=col(5))
    return _matmul(o, w_out.astype(BF16), F32, tm=1024, tn=512, residual=x_res)


def _swiglu(h, w_gate_up, w_down, x_res):
    act = _gate_up(h, w_gate_up.astype(BF16), tm=1024, tn=256)
    return _matmul(act, w_down.astype(BF16), F32, tm=512, tn=256, residual=x_res)


def kernel(x, attn_norm, ffn_norm, w_gate_up, w_down, ab_w_in, ab_sinks, ab_w_out, nsa_w_in,
           nsa_cmp_pos, nsa_cmp_wk1, nsa_cmp_wk2, nsa_cmp_wv1, nsa_cmp_wv2, nsa_w_out, final_norm):
    b, s, d = x.shape
    outs = []
    for bi in range(b):
        xb = x[bi]
        for layer in range(attn_norm.shape[0]):
            h = _rmsnorm(xb, attn_norm[layer], BF16)
            if layer % 2 == 0:
                e = layer // 2
                xb = _ab_mixer(h, ab_w_in[e], ab_sinks[e], ab_w_out[e], xb)
            else:
                o = layer // 2
                xb = _nsa_mixer(h, nsa_w_in[o], nsa_cmp_pos[o], nsa_cmp_wk1[o], nsa_cmp_wk2[o],
                                nsa_cmp_wv1[o], nsa_cmp_wv2[o], nsa_w_out[o], xb)
            h = _rmsnorm(xb, ffn_norm[layer], BF16)
            xb = _swiglu(h, w_gate_up[layer], w_down[layer], xb)
        outs.append(_rmsnorm(xb, final_norm, F32))
    return jnp.stack(outs, axis=0)
```

```python
import functools

import numpy as np
import jax
import jax.numpy as jnp
from jax import lax
from jax.experimental import pallas as pl
from jax.experimental.pallas import tpu as pltpu

F32 = jnp.float32
BF16 = jnp.bfloat16

HEAD_DIM = 128
BLK = 128
EPS = 1e-5
NEG = -1e30
SCALE = HEAD_DIM ** -0.5
LOG2E = 1.4426950408889634

H_SB = 8
H_SW = 24
KV_SW = 3
SW_WINDOW = 128

H_NSA = 32
KV_NSA = 2
NSA_CMP_LEN = 32
NSA_CMP_STRIDE = 16
NSA_SEL_LEN = 64
NSA_TOP_N = 8
NSA_WINDOW = 512
NSA_FORCE = 1e4

VMEM_LIMIT_BYTES = 52 * 2 ** 20

_NT = (((1,), (1,)), ((), ()))


def _params(*sem):
    return pltpu.CompilerParams(dimension_semantics=sem,
                                vmem_limit_bytes=VMEM_LIMIT_BYTES)


def _dot(a, b):
    return jnp.dot(a, b, preferred_element_type=F32)


def _dot_nt(a, b):
    return lax.dot_general(a, b, _NT, preferred_element_type=F32)


def _alibi_slopes(n_heads, n_kv):
    s = np.exp2(np.float32(-8.0) * np.arange(1, n_heads + 1, dtype=np.float32)
                / np.float32(n_heads)).astype(np.float32)
    return s.reshape(n_heads // n_kv, n_kv).T


def _rmsnorm_kernel(x_ref, g_ref, o_ref):
    x = x_ref[...]
    ms = jnp.mean(x * x, axis=-1, keepdims=True)
    o_ref[...] = (x * lax.rsqrt(ms + EPS) * g_ref[...]).astype(o_ref.dtype)


def _rmsnorm(x, g, out_dtype, tm=256):
    m, d = x.shape
    return pl.pallas_call(
        _rmsnorm_kernel,
        grid=(m // tm,),
        in_specs=[pl.BlockSpec((tm, d), lambda i: (i, 0)),
                  pl.BlockSpec((1, d), lambda i: (0, 0))],
        out_specs=pl.BlockSpec((tm, d), lambda i: (i, 0)),
        out_shape=jax.ShapeDtypeStruct((m, d), out_dtype),
        compiler_params=_params("parallel"),
        name="rmsnorm",
    )(x, g.reshape(1, d).astype(F32))


def _mm_kernel(*refs, nk, has_res):
    if has_res:
        a_ref, b_ref, r_ref, o_ref = refs[:4]
        scratch = refs[4:]
    else:
        a_ref, b_ref, o_ref = refs[:3]
        r_ref = None
        scratch = refs[3:]

    def finish(acc):
        if has_res:
            acc = acc + r_ref[...]
        o_ref[...] = acc.astype(o_ref.dtype)

    part = _dot(a_ref[...], b_ref[...])
    if nk == 1:
        finish(part)
    else:
        acc_ref, = scratch
        k = pl.program_id(2)

        @pl.when(k == 0)
        def _():
            acc_ref[...] = part

        @pl.when(k > 0)
        def _():
            acc_ref[...] += part

        @pl.when(k == nk - 1)
        def _():
            finish(acc_ref[...])


def _matmul(a, b, out_dtype, *, tm, tn, tk=None, residual=None):
    m, kd = a.shape
    _, n = b.shape
    tk = kd if tk is None else tk
    nk = kd // tk
    assert m % tm == 0 and n % tn == 0 and kd % tk == 0
    in_specs = [pl.BlockSpec((tm, tk), lambda i, j, k: (i, k)),
                pl.BlockSpec((tk, tn), lambda i, j, k: (k, j))]
    args = [a, b]
    if residual is not None:
        in_specs.append(pl.BlockSpec((tm, tn), lambda i, j, k: (i, j)))
        args.append(residual)
    return pl.pallas_call(
        functools.partial(_mm_kernel, nk=nk, has_res=residual is not None),
        grid=(m // tm, n // tn, nk),
        in_specs=in_specs,
        out_specs=pl.BlockSpec((tm, tn), lambda i, j, k: (i, j)),
        out_shape=jax.ShapeDtypeStruct((m, n), out_dtype),
        scratch_shapes=[pltpu.VMEM((tm, tn), F32)] if nk > 1 else [],
        compiler_params=_params("parallel", "parallel", "arbitrary"),
        name="matmul",
    )(*args)


def _gate_up_kernel(a_ref, wg_ref, wu_ref, o_ref):
    a = a_ref[...]
    gate = _dot(a, wg_ref[...])
    up = _dot(a, wu_ref[...])
    o_ref[...] = (gate * jax.nn.sigmoid(gate) * up).astype(o_ref.dtype)


def _gate_up(a, w_gate_up, *, tm, tn):
    m, kd = a.shape
    f = w_gate_up.shape[1] // 2
    nf = f // tn
    assert m % tm == 0 and f % tn == 0
    return pl.pallas_call(
        _gate_up_kernel,
        grid=(m // tm, nf),
        in_specs=[pl.BlockSpec((tm, kd), lambda i, j: (i, 0)),
                  pl.BlockSpec((kd, tn), lambda i, j: (0, j)),
                  pl.BlockSpec((kd, tn), lambda i, j: (0, j + nf))],
        out_specs=pl.BlockSpec((tm, tn), lambda i, j: (i, j)),
        out_shape=jax.ShapeDtypeStruct((m, f), BF16),
        compiler_params=_params("parallel", "parallel"),
        name="gate_up",
    )(a, w_gate_up, w_gate_up)


def _sb_kernel(q_ref, k_ref, v_ref, o_ref, acc_ref, later_ref, z_ref, w_ref, *, tq, tk):
    qi = pl.program_id(1)
    ratio = tq // tk
    assert ratio % 2 == 0
    q = q_ref[...]
    row = lax.broadcasted_iota(jnp.int32, (tk, tk), 0)
    col = lax.broadcasted_iota(jnp.int32, (tk, tk), 1)
    suffix = jnp.where(row >= col, 1.0, 0.0).astype(BF16)
    acc_ref[...] = jnp.zeros_like(acc_ref)
    later_ref[...] = jnp.zeros_like(later_ref)

    def logits(j, r0=0):
        start = pl.multiple_of(j * tk, tk)
        return _dot_nt(q[r0:], k_ref[pl.ds(start, tk), :]) * (-SCALE * LOG2E)

    def weights(j, nz, masked, r0=0):
        neg_abs = lax.bitcast_convert_type(
            lax.bitcast_convert_type(nz, jnp.uint32) | jnp.uint32(0x80000000), F32)
        a = jnp.minimum(nz, 0.0) - jnp.log2(1.0 + jnp.exp2(neg_abs))
        if masked:
            t_pos = qi * tq + r0 + lax.broadcasted_iota(jnp.int32, nz.shape, 0)
            s_pos = j * tk + lax.broadcasted_iota(jnp.int32, nz.shape, 1)
            before = s_pos < t_pos
            a = jnp.where(before, a, 0.0)
        within = _dot(a.astype(BF16), suffix)
        later = later_ref[r0:, :]
        w = jnp.exp2(within + jnp.concatenate([later] * (tk // HEAD_DIM), axis=1) - nz)
        later_ref[r0:, :] = later + within[:, 0:1]
        if masked:
            w = jnp.where(before, w, 0.0)
        return w.astype(BF16)

    def accumulate(j, w, r0=0):
        start = pl.multiple_of(j * tk, tk)
        acc_ref[r0:, :] += _dot(w, v_ref[pl.ds(start, tk), :])

    n_full = ratio * qi

    for b in reversed(range(ratio)):
        j = n_full + b
        accumulate(j, weights(j, logits(j, b * tk), True, b * tk), b * tk)

    def step(j, slot):
        accumulate(j + 1, w_ref[1 - slot])
        z_ref[1 - slot] = logits(jnp.maximum(j - 1, 0))
        w_ref[slot] = weights(j, z_ref[slot], False)

    def two_steps(s, carry):
        j = n_full - 1 - 2 * s
        step(j, 0)
        step(j - 1, 1)
        return carry

    z_ref[0] = logits(jnp.maximum(n_full - 1, 0))
    w_ref[1] = jnp.zeros((tq, tk), BF16)
    lax.fori_loop(0, n_full // 2, two_steps, 0)
    accumulate(0, w_ref[1])
    o_ref[...] = acc_ref[...].astype(o_ref.dtype)


def _stick_breaking(z, *, n_heads, q_col, k_col, v_col, tq=1024, tk=256):
    s = z.shape[0]
    return pl.pallas_call(
        functools.partial(_sb_kernel, tq=tq, tk=tk),
        grid=(n_heads, s // tq),
        in_specs=[pl.BlockSpec((tq, HEAD_DIM), lambda h, i: (i, q_col + h)),
                  pl.BlockSpec((s, HEAD_DIM), lambda h, i: (0, k_col + h)),
                  pl.BlockSpec((s, HEAD_DIM), lambda h, i: (0, v_col + h))],
        out_specs=pl.BlockSpec((tq, HEAD_DIM), lambda h, i: (i, h)),
        out_shape=jax.ShapeDtypeStruct((s, n_heads * HEAD_DIM), BF16),
        scratch_shapes=[pltpu.VMEM((tq, HEAD_DIM), F32),
                        pltpu.VMEM((tq, HEAD_DIM), F32),
                        pltpu.VMEM((2, tq, tk), F32),
                        pltpu.VMEM((2, tq, tk), BF16)],
        compiler_params=_params("parallel", "arbitrary"),
        name="stick_breaking",
    )(z, z, z)


def _swa_kernel(sink_ref, q_ref, kp_ref, kc_ref, vp_ref, vc_ref, o_ref, *, slopes):
    i = pl.program_id(0)
    n_kv, g_per = slopes.shape
    qq = lax.broadcasted_iota(jnp.int32, (BLK, 2 * BLK), 0)
    kk = lax.broadcasted_iota(jnp.int32, (BLK, 2 * BLK), 1)
    dist = qq + BLK - kk
    first_key = jnp.where(i > 0, 0, BLK)
    mask = (dist >= 0) & (dist < SW_WINDOW) & (kk >= first_key)
    dist_f = dist.astype(F32)
    for h in range(n_kv):
        cs = slice(h * HEAD_DIM, (h + 1) * HEAD_DIM)
        k = jnp.concatenate([kp_ref[:, cs], kc_ref[:, cs]], axis=0)
        v = jnp.concatenate([vp_ref[:, cs], vc_ref[:, cs]], axis=0)
        for g in range(g_per):
            head = h * g_per + g
            hs = slice(head * HEAD_DIM, (head + 1) * HEAD_DIM)
            s = _dot_nt(q_ref[:, hs], k) * SCALE - float(slopes[h, g]) * dist_f
            s = jnp.where(mask, s, NEG)
            sink = sink_ref[head]
            m = jnp.maximum(jnp.max(s, axis=-1, keepdims=True), sink)
            e = jnp.where(mask, jnp.exp(s - m), 0.0)
            denom = jnp.sum(e, axis=-1, keepdims=True) + jnp.exp(sink - m)
            p = e / denom
            o_ref[:, hs] = _dot(p.astype(BF16), v).astype(o_ref.dtype)


def _window_sink_attention(z, sinks, *, q_col, k_col, v_col):
    s = z.shape[0]
    qw = H_SW * HEAD_DIM
    kw = KV_SW * HEAD_DIM
    assert q_col % qw == 0 and k_col % kw == 0 and v_col % kw == 0
    qb, kb, vb = q_col // qw, k_col // kw, v_col // kw
    prev = lambda i: jnp.maximum(i - 1, 0)
    return pl.pallas_call(
        functools.partial(_swa_kernel, slopes=_alibi_slopes(H_SW, KV_SW)),
        grid=(s // BLK,),
        in_specs=[pl.BlockSpec(memory_space=pltpu.SMEM),
                  pl.BlockSpec((BLK, qw), lambda i: (i, qb)),
                  pl.BlockSpec((BLK, kw), lambda i: (prev(i), kb)),
                  pl.BlockSpec((BLK, kw), lambda i: (i, kb)),
                  pl.BlockSpec((BLK, kw), lambda i: (prev(i), vb)),
                  pl.BlockSpec((BLK, kw), lambda i: (i, vb))],
        out_specs=pl.BlockSpec((BLK, qw), lambda i: (i, 0)),
        out_shape=jax.ShapeDtypeStruct((s, qw), BF16),
        compiler_params=_params("parallel"),
        name="window_sink_attention",
    )(sinks.astype(F32), z, z, z, z, z)


def _compress_kernel(x_ref, pos_ref, w1_ref, w2_ref, o_ref):
    x = x_ref[0].astype(F32)
    half = x.shape[1]
    n_chunk = x.shape[0]
    lo = _dot((x + pos_ref[:, :half]).astype(BF16), w1_ref[0, :half, :])
    hi = _dot((x + pos_ref[:, half:]).astype(BF16), w1_ref[0, half:, :])
    pre = lo + pltpu.roll(hi, n_chunk - 1, 0)
    act = jax.nn.gelu(pre, approximate=True)
    out = _dot(act.astype(BF16), w2_ref[0])
    last = lax.broadcasted_iota(jnp.int32, out.shape, 0) == n_chunk - 1
    o_ref[0] = jnp.where(last, 0.0, out).astype(o_ref.dtype)


def _compress(x, pos, w1, w2):
    n, n_chunk, width = x.shape
    hid = w1.shape[-1]
    return pl.pallas_call(
        _compress_kernel,
        grid=(n,),
        in_specs=[pl.BlockSpec((1, n_chunk, width), lambda i: (i, 0, 0)),
                  pl.BlockSpec((1, 2 * width), lambda i: (0, 0)),
                  pl.BlockSpec((1, 2 * width, hid), lambda i: (i // KV_NSA, 0, 0)),
                  pl.BlockSpec((1, hid, HEAD_DIM), lambda i: (i // KV_NSA, 0, 0))],
        out_specs=pl.BlockSpec((1, n_chunk, HEAD_DIM), lambda i: (i, 0, 0)),
        out_shape=jax.ShapeDtypeStruct((n, n_chunk, HEAD_DIM), BF16),
        compiler_params=_params("parallel"),
        name="nsa_compress",
    )(x, pos, w1, w2)


_CMP_CHUNK = 256
_CMP_UNROLL = 4


def _split3(x):
    hi = x.astype(BF16)
    r1 = x - hi.astype(F32)
    mid = r1.astype(BF16)
    lo = (r1 - mid.astype(F32)).astype(BF16)
    return hi, mid, lo


def _nsa_cmp_kernel(slope_ref, q_ref, kc_ref, vc_ref, o_ref, sel_ref, need_ref,
                    imp_ref, qs_ref, os_ref, *, g_per, n_slc):
    h = pl.program_id(0)
    i = pl.program_id(1)
    n_cmp = kc_ref.shape[1]
    _stack_heads(q_ref, qs_ref, g_per)
    imp_ref[...] = jnp.zeros_like(imp_ref)

    def attend(width):
        kc = kc_ref[0, :width, :]
        vc = vc_ref[0, :width, :]
        t = i * BLK + lax.broadcasted_iota(jnp.int32, (BLK, width), 0)
        c_end = (lax.broadcasted_iota(jnp.int32, (BLK, width), 1) * NSA_CMP_STRIDE
                 + (NSA_CMP_LEN - 1))
        dist = t - c_end
        mask = dist >= 0
        dist_f = dist.astype(F32)

        def group(gg, carry):
            heads = [gg * _CMP_UNROLL + u for u in range(_CMP_UNROLL)]
            rows = [pl.ds(pl.multiple_of(g * BLK, BLK), BLK) for g in heads]
            s = [_dot_nt(qs_ref[r, :], kc) for r in rows]
            s = [jnp.where(mask, x * SCALE - slope_ref[h * g_per + g] * dist_f, NEG)
                 for x, g in zip(s, heads)]
            m = [jnp.max(x, axis=-1, keepdims=True) for x in s]
            e = [jnp.where(mask, jnp.exp(x - mx), 0.0) for x, mx in zip(s, m)]
            p = [x / jnp.maximum(jnp.sum(x, axis=-1, keepdims=True), 1e-30) for x in e]
            for r, x in zip(rows, p):
                os_ref[r, :] = _dot(x.astype(BF16), vc)
            total = p[0]
            for x in p[1:]:
                total = total + x
            imp_ref[:, :width] += total
            return carry

        lax.fori_loop(0, g_per // _CMP_UNROLL, group, 0)

    n_visible = (BLK * i + BLK - NSA_CMP_LEN) // NSA_CMP_STRIDE + 1
    widths = sorted({min(w, n_cmp) for w in range(_CMP_CHUNK, n_cmp + _CMP_CHUNK, _CMP_CHUNK)})
    case = jnp.minimum((n_visible - 1) // _CMP_CHUNK, len(widths) - 1)
    for idx, width in enumerate(widths):
        pl.when(case == idx)(functools.partial(attend, width))
    for g in range(g_per):
        o_ref[:, g * HEAD_DIM:(g + 1) * HEAD_DIM] = os_ref[g * BLK:(g + 1) * BLK, :]

    r = NSA_SEL_LEN // NSA_CMP_STRIDE
    ci = lax.broadcasted_iota(jnp.int32, (n_cmp, n_slc), 0)
    cj = lax.broadcasted_iota(jnp.int32, (n_cmp, n_slc), 1)
    pool = jnp.where((ci >= r * cj - 1) & (ci <= r * cj + r - 1), 1.0, 0.0).astype(BF16)
    hi, mid, lo = _split3(imp_ref[...])
    imp = _dot(hi, pool) + (_dot(mid, pool) + _dot(lo, pool))

    blk = lax.broadcasted_iota(jnp.int32, (BLK, n_slc), 1)
    cur = (i * BLK + lax.broadcasted_iota(jnp.int32, (BLK, n_slc), 0)) // NSA_SEL_LEN
    valid = blk <= cur
    forced = (blk == 0) | (blk == cur) | (blk == cur - 1)
    score = jnp.where(valid, imp + jnp.where(forced, NSA_FORCE, 0.0), NEG)
    blk_f = blk.astype(F32)
    chosen = jnp.zeros((BLK, n_slc), F32)
    for _ in range(min(NSA_TOP_N, n_slc)):
        best = jnp.max(score, axis=-1, keepdims=True)
        first = jnp.min(jnp.where(score == best, blk_f, float(n_slc)), axis=-1, keepdims=True)
        pick = blk_f == first
        chosen = jnp.where(pick, 1.0, chosen)
        score = jnp.where(pick, -jnp.inf, score)
    chosen = jnp.where(valid, chosen, 0.0)
    sel_ref[0] = chosen.astype(sel_ref.dtype)

    n_kb = n_slc * NSA_SEL_LEN // BLK
    any_q = jnp.max(chosen, axis=0, keepdims=True)
    any_q = jnp.broadcast_to(any_q, (8, n_slc)).astype(BF16)
    pi = lax.broadcasted_iota(jnp.int32, (n_slc, n_kb), 0)
    pj = lax.broadcasted_iota(jnp.int32, (n_slc, n_kb), 1)
    per_kb = _dot(any_q, jnp.where(pi * NSA_SEL_LEN // BLK == pj, 1.0, 0.0).astype(BF16))
    per_kb = jnp.where(per_kb > 0.5, 1.0, 0.0).astype(BF16)
    n_word = need_ref.shape[-1]
    bi = lax.broadcasted_iota(jnp.int32, (n_kb, n_word), 0)
    bw = lax.broadcasted_iota(jnp.int32, (n_kb, n_word), 1)
    weight = jnp.where(bi // 16 == bw, jnp.left_shift(1, bi % 16), 0).astype(F32).astype(BF16)
    words = _dot(per_kb, weight)
    need_ref[0, 0] = words.astype(jnp.int32)


def _nsa_compressed(z, k_cmp, v_cmp, slopes):
    s = z.shape[0]
    n_slc = s // NSA_SEL_LEN
    g_per = H_NSA // KV_NSA
    qw = g_per * HEAD_DIM
    n_cmp = k_cmp.shape[1]
    n_word = 128
    assert (s // BLK) <= 16 * n_word
    return pl.pallas_call(
        functools.partial(_nsa_cmp_kernel, g_per=g_per, n_slc=n_slc),
        grid=(KV_NSA, s // BLK),
        in_specs=[pl.BlockSpec(memory_space=pltpu.SMEM),
                  pl.BlockSpec((BLK, qw), lambda h, i: (i, h)),
                  pl.BlockSpec((1, n_cmp, HEAD_DIM), lambda h, i: (h, 0, 0)),
                  pl.BlockSpec((1, n_cmp, HEAD_DIM), lambda h, i: (h, 0, 0))],
        out_specs=[pl.BlockSpec((BLK, qw), lambda h, i: (i, h)),
                   pl.BlockSpec((1, BLK, n_slc), lambda h, i: (h, i, 0)),
                   pl.BlockSpec((1, 1, 8, n_word), lambda h, i: (h, i, 0, 0))],
        out_shape=[jax.ShapeDtypeStruct((s, H_NSA * HEAD_DIM), F32),
                   jax.ShapeDtypeStruct((KV_NSA, s, n_slc), BF16),
                   jax.ShapeDtypeStruct((KV_NSA, s // BLK, 8, n_word), jnp.int32)],
        scratch_shapes=[pltpu.VMEM((BLK, n_cmp), F32),
                        pltpu.VMEM((g_per * BLK, HEAD_DIM), BF16),
                        pltpu.VMEM((g_per * BLK, HEAD_DIM), F32)],
        compiler_params=_params("parallel", "arbitrary"),
        name="nsa_compressed",
    )(slopes, z, k_cmp, v_cmp)


def _stack_heads(q_ref, qs_ref, g_per):
    for g in range(g_per):
        qs_ref[g * BLK:(g + 1) * BLK, :] = q_ref[:, g * HEAD_DIM:(g + 1) * HEAD_DIM]


_ROW_SPLIT = 2


def _online_visit(i, blocks, qs_ref, k_ref, v_ref, slope_ref, m_ref, l_ref, acc_ref, g_per):
    nb = len(blocks)
    starts = [pl.multiple_of(j * BLK, BLK) for j, _ in blocks]
    k = jnp.concatenate([k_ref[pl.ds(st, BLK), :] for st in starts], axis=0)
    v = jnp.concatenate([v_ref[pl.ds(st, BLK), :] for st in starts], axis=0)
    v_ones = jnp.concatenate([v, jnp.ones((nb * BLK, HEAD_DIM), BF16)], axis=1)
    qq = lax.broadcasted_iota(jnp.int32, (BLK, BLK), 0) - lax.broadcasted_iota(jnp.int32, (BLK, BLK), 1)
    dists = [(i - j) * BLK + qq for j, _ in blocks]
    ok1 = jnp.concatenate([key_ok(d) for (_, key_ok), d in zip(blocks, dists)], axis=1)
    dist1 = jnp.concatenate([d.astype(F32) for d in dists], axis=1)
    g_sub = g_per // _ROW_SPLIT
    ok = jnp.concatenate([ok1] * g_sub, axis=0) > 0.5
    dist_f = jnp.concatenate([dist1] * g_sub, axis=0)

    groups = [slice(r * g_sub * BLK, (r + 1) * g_sub * BLK) for r in range(_ROW_SPLIT)]
    s = [_dot_nt(qs_ref[rows, :], k) for rows in groups]
    s = [jnp.where(ok, x * SCALE - jnp.concatenate([slope_ref[0, rows, :]] * nb, axis=1) * dist_f, NEG)
         for x, rows in zip(s, groups)]
    m_prev = [m_ref[rows, :] for rows in groups]
    m_new = [jnp.maximum(mp, jnp.max(x, axis=-1, keepdims=True)) for mp, x in zip(m_prev, s)]
    alpha = [jnp.exp(mp - mn) for mp, mn in zip(m_prev, m_new)]
    p = [jnp.where(ok, jnp.exp(x - jnp.concatenate([mn] * nb, axis=1)), 0.0) for x, mn in zip(s, m_new)]
    pv = [_dot(x.astype(BF16), v_ones) for x in p]
    for rows, a, y, mn in zip(groups, alpha, pv, m_new):
        l_ref[rows, :] = a * l_ref[rows, :] + y[:, HEAD_DIM:]
        acc_ref[rows, :] = a * acc_ref[rows, :] + y[:, :HEAD_DIM]
        m_ref[rows, :] = mn


def _online_init(m_ref, l_ref, acc_ref):
    m_ref[...] = jnp.full_like(m_ref, NEG)
    l_ref[...] = jnp.zeros_like(l_ref)
    acc_ref[...] = jnp.zeros_like(acc_ref)


_SEL_GROUP = 4


def _nsa_sel_kernel(need_ref, q_ref, sel_ref, slope_ref, k_ref, v_ref, o_ref,
                    qs_ref, m_ref, l_ref, acc_ref, list_ref, *, g_per, n_qblk, n_word):
    h = pl.program_id(0)
    i = pl.program_id(1)
    _stack_heads(q_ref, qs_ref, g_per)
    _online_init(m_ref, l_ref, acc_ref)
    sel = sel_ref[0]
    n_slc = sel.shape[1]
    per_blk = BLK // NSA_SEL_LEN
    eb = lax.broadcasted_iota(jnp.int32, (n_slc, BLK), 0)
    el = lax.broadcasted_iota(jnp.int32, (n_slc, BLK), 1) // NSA_SEL_LEN
    base = (h * n_qblk + i) * n_word

    def scan(j, n):
        word = need_ref[base + jnp.right_shift(j, 4)]
        needed = jnp.bitwise_and(jnp.right_shift(word, jnp.bitwise_and(j, 15)), 1)

        @pl.when(needed == 1)
        def _():
            list_ref[n] = j
        return n + needed

    n_needed = lax.fori_loop(0, i + 1, scan, 0)

    def visit(first, count):
        blocks = []
        for u in range(count):
            j = list_ref[first + u]
            expand = jnp.where(eb == per_blk * j + el, 1.0, 0.0).astype(BF16)
            picked = _dot(sel, expand)
            blocks.append((j, functools.partial(
                lambda dist, picked: jnp.where(dist >= 0, picked, 0.0), picked=picked)))
        _online_visit(i, blocks, qs_ref, k_ref, v_ref, slope_ref, m_ref, l_ref, acc_ref, g_per)

    def group(t, carry):
        visit(t * _SEL_GROUP, _SEL_GROUP)
        return carry

    n_groups = n_needed // _SEL_GROUP
    lax.fori_loop(0, n_groups, group, 0)
    done = n_groups * _SEL_GROUP
    size = _SEL_GROUP // 2
    while size >= 1:
        has = jnp.bitwise_and(n_needed, size) != 0
        pl.when(has)(functools.partial(visit, done, size))
        done = done + jnp.where(has, size, 0)
        size //= 2
    out = acc_ref[...] / jnp.maximum(l_ref[...], 1e-30)
    for g in range(g_per):
        o_ref[:, g * HEAD_DIM:(g + 1) * HEAD_DIM] = out[g * BLK:(g + 1) * BLK, :]


def _nsa_selected(z, sel, need, slope_rows, *, k_col, v_col):
    s = z.shape[0]
    g_per = H_NSA // KV_NSA
    qw = g_per * HEAD_DIM
    rows = g_per * BLK
    n_slc = sel.shape[-1]
    n_qblk = s // BLK
    n_word = -(-n_qblk // 16)
    need = need[:, :, 0, :n_word]
    return pl.pallas_call(
        functools.partial(_nsa_sel_kernel, g_per=g_per, n_qblk=n_qblk, n_word=n_word),
        grid=(KV_NSA, n_qblk),
        in_specs=[pl.BlockSpec(memory_space=pltpu.SMEM),
                  pl.BlockSpec((BLK, qw), lambda h, i: (i, h)),
                  pl.BlockSpec((1, BLK, n_slc), lambda h, i: (h, i, 0)),
                  pl.BlockSpec((1, rows, HEAD_DIM), lambda h, i: (h, 0, 0)),
                  pl.BlockSpec((s, HEAD_DIM), lambda h, i: (0, k_col + h)),
                  pl.BlockSpec((s, HEAD_DIM), lambda h, i: (0, v_col + h))],
        out_specs=pl.BlockSpec((BLK, qw), lambda h, i: (i, h)),
        out_shape=jax.ShapeDtypeStruct((s, H_NSA * HEAD_DIM), F32),
        scratch_shapes=[pltpu.VMEM((rows, HEAD_DIM), BF16),
                        pltpu.VMEM((rows, HEAD_DIM), F32),
                        pltpu.VMEM((rows, HEAD_DIM), F32),
                        pltpu.VMEM((rows, HEAD_DIM), F32),
                        pltpu.SMEM((n_qblk + _SEL_GROUP,), jnp.int32)],
        compiler_params=_params("parallel", "arbitrary"),
        name="nsa_selected",
    )(need.reshape(-1), z, sel, slope_rows, z, z)


def _nsa_win_kernel(q_ref, slope_ref, k_ref, v_ref, gate_ref, ocmp_ref, oslc_ref, o_ref,
                    qs_ref, m_ref, l_ref, acc_ref, *, g_per):
    i = pl.program_id(1)
    _stack_heads(q_ref, qs_ref, g_per)
    _online_init(m_ref, l_ref, acc_ref)
    n_prev = -(-(NSA_WINDOW - 1) // BLK)

    blocks = []
    for step in range(n_prev + 1):
        j = i - n_prev + step
        present = jnp.where(j >= 0, 1.0, 0.0)
        blocks.append((jnp.maximum(j, 0), functools.partial(
            lambda dist, present: jnp.where((dist >= 0) & (dist < NSA_WINDOW), present, 0.0),
            present=present)))
    _online_visit(i, blocks, qs_ref, k_ref, v_ref, slope_ref, m_ref, l_ref, acc_ref, g_per)
    o_win = acc_ref[...] / jnp.maximum(l_ref[...], 1e-30)
    gates = jax.nn.sigmoid(gate_ref[...])
    for g in range(g_per):
        hs = slice(g * HEAD_DIM, (g + 1) * HEAD_DIM)
        o = (gates[:, 3 * g:3 * g + 1] * ocmp_ref[:, hs]
             + gates[:, 3 * g + 1:3 * g + 2] * oslc_ref[:, hs]
             + gates[:, 3 * g + 2:3 * g + 3] * o_win[g * BLK:(g + 1) * BLK, :])
        o_ref[:, hs] = o.astype(o_ref.dtype)


def _nsa_window_combine(z, gate_logits, o_cmp, o_slc, slope_rows, *, k_col, v_col):
    s = z.shape[0]
    g_per = H_NSA // KV_NSA
    qw = g_per * HEAD_DIM
    rows = g_per * BLK
    head_blk = lambda h, i: (i, h)
    return pl.pallas_call(
        functools.partial(_nsa_win_kernel, g_per=g_per),
        grid=(KV_NSA, s // BLK),
        in_specs=[pl.BlockSpec((BLK, qw), head_blk),
                  pl.BlockSpec((1, rows, HEAD_DIM), lambda h, i: (h, 0, 0)),
                  pl.BlockSpec((s, HEAD_DIM), lambda h, i: (0, k_col + h)),
                  pl.BlockSpec((s, HEAD_DIM), lambda h, i: (0, v_col + h)),
                  pl.BlockSpec((BLK, 128), head_blk),
                  pl.BlockSpec((BLK, qw), head_blk),
                  pl.BlockSpec((BLK, qw), head_blk)],
        out_specs=pl.BlockSpec((BLK, qw), head_blk),
        out_shape=jax.ShapeDtypeStruct((s, H_NSA * HEAD_DIM), BF16),
        scratch_shapes=[pltpu.VMEM((rows, HEAD_DIM), BF16),
                        pltpu.VMEM((rows, HEAD_DIM), F32),
                        pltpu.VMEM((rows, HEAD_DIM), F32),
                        pltpu.VMEM((rows, HEAD_DIM), F32)],
        compiler_params=_params("parallel", "arbitrary"),
        name="nsa_window_combine",
    )(z, slope_rows, z, z, gate_logits, o_cmp, o_slc)


def _ab_mixer(h, w_in, sinks, w_out, x_res):
    z = _matmul(h, w_in.astype(BF16), BF16, tm=1024, tn=768)
    d_sb = H_SB * HEAD_DIM
    o_a = _stick_breaking(z, n_heads=H_SB, q_col=0, k_col=H_SB, v_col=2 * H_SB)
    q_b = 3 * d_sb
    k_b = q_b + H_SW * HEAD_DIM
    v_b = k_b + KV_SW * HEAD_DIM
    o_b = _window_sink_attention(z, sinks, q_col=q_b, k_col=k_b, v_col=v_b)
    o = jnp.concatenate([o_a, o_b], axis=-1)
    return _matmul(o, w_out.astype(BF16), F32, tm=1024, tn=512, residual=x_res)


def _nsa_mixer(h, w_in, cmp_pos, wk1, wk2, wv1, wv2, w_out, x_res):
    s = h.shape[0]
    dq = H_NSA * HEAD_DIM
    dkv = KV_NSA * HEAD_DIM
    g_per = H_NSA // KV_NSA
    main = dq + 6 * dkv
    z = _matmul(h, w_in[:, :main].astype(BF16), BF16, tm=1024, tn=512)
    w_gate = w_in[:, main:].reshape(-1, KV_NSA, 3 * g_per)
    w_gate = jnp.pad(w_gate, ((0, 0), (0, 0), (0, 128 - 3 * g_per))).reshape(-1, KV_NSA * 128)
    gate_logits = _matmul(h, w_gate.astype(BF16), F32, tm=1024, tn=KV_NSA * 128)

    n_chunk = s // NSA_CMP_STRIDE
    to_chunks = lambda c: (z[:, c:c + dkv].reshape(n_chunk, NSA_CMP_STRIDE, KV_NSA, HEAD_DIM)
                           .transpose(2, 0, 1, 3).reshape(KV_NSA, n_chunk, NSA_CMP_STRIDE * HEAD_DIM))
    chunks = jnp.concatenate([to_chunks(dq), to_chunks(dq + dkv)], axis=0)
    kv_cmp = _compress(chunks, cmp_pos.reshape(1, -1).astype(F32),
                       jnp.stack([wk1, wv1]).astype(BF16), jnp.stack([wk2, wv2]).astype(BF16))
    k_cmp, v_cmp = kv_cmp[:KV_NSA], kv_cmp[KV_NSA:]

    slopes = _alibi_slopes(H_NSA, KV_NSA)
    slope_rows = jnp.asarray(np.repeat(np.repeat(slopes, BLK, axis=1)[..., None], HEAD_DIM, axis=2))
    o_cmp, sel, need = _nsa_compressed(z, k_cmp, v_cmp, jnp.asarray(slopes.reshape(-1)))
    col = lambda off: (dq + off * dkv) // HEAD_DIM
    o_slc = _nsa_selected(z, sel, need, slope_rows, k_col=col(2), v_col=col(3))
    o = _nsa_window_combine(z, gate_logits, o_cmp, o_slc, slope_rows, k_col=col(4), v_col=col(5))
    return _matmul(o, w_out.astype(BF16), F32, tm=1024, tn=512, residual=x_res)


def _swiglu(h, w_gate_up, w_down, x_res):
    act = _gate_up(h, w_gate_up.astype(BF16), tm=2048, tn=256)
    return _matmul(act, w_down.astype(BF16), F32, tm=512, tn=512, residual=x_res)


def kernel(x, attn_norm, ffn_norm, w_gate_up, w_down, ab_w_in, ab_sinks, ab_w_out, nsa_w_in,
           nsa_cmp_pos, nsa_cmp_wk1, nsa_cmp_wk2, nsa_cmp_wv1, nsa_cmp_wv2, nsa_w_out, final_norm):
    b, s, d = x.shape
    outs = []
    for bi in range(b):
        xb = x[bi]
        for layer in range(attn_norm.shape[0]):
            h = _rmsnorm(xb, attn_norm[layer], BF16)
            if layer % 2 == 0:
                e = layer // 2
                xb = _ab_mixer(h, ab_w_in[e], ab_sinks[e], ab_w_out[e], xb)
            else:
                o = layer // 2
                xb = _nsa_mixer(h, nsa_w_in[o], nsa_cmp_pos[o], nsa_cmp_wk1[o], nsa_cmp_wk2[o],
                                nsa_cmp_wv1[o], nsa_cmp_wv2[o], nsa_w_out[o], xb)
            h = _rmsnorm(xb, ffn_norm[layer], BF16)
            xb = _swiglu(h, w_gate_up[layer], w_down[layer], xb)
        outs.append(_rmsnorm(xb, final_norm, F32))
    return jnp.stack(outs, axis=0)
```

```python
import functools

import numpy as np
import jax
import jax.numpy as jnp
from jax import lax
from jax.experimental import pallas as pl
from jax.experimental.pallas import tpu as pltpu

F32 = jnp.float32
BF16 = jnp.bfloat16

HEAD_DIM = 128
BLK = 128
EPS = 1e-5
NEG = -1e30
SCALE = HEAD_DIM ** -0.5
LOG2E = 1.4426950408889634

H_SB = 8
H_SW = 24
KV_SW = 3
SW_WINDOW = 128

H_NSA = 32
KV_NSA = 2
NSA_CMP_LEN = 32
NSA_CMP_STRIDE = 16
NSA_SEL_LEN = 64
NSA_TOP_N = 8
NSA_WINDOW = 512
NSA_FORCE = 1e4

VMEM_LIMIT_BYTES = 52 * 2 ** 20

_NT = (((1,), (1,)), ((), ()))


def _params(*sem):
    return pltpu.CompilerParams(dimension_semantics=sem,
                                vmem_limit_bytes=VMEM_LIMIT_BYTES)


def _dot(a, b):
    return jnp.dot(a, b, preferred_element_type=F32)


def _dot_nt(a, b):
    return lax.dot_general(a, b, _NT, preferred_element_type=F32)


def _alibi_slopes(n_heads, n_kv):
    s = np.exp2(np.float32(-8.0) * np.arange(1, n_heads + 1, dtype=np.float32)
                / np.float32(n_heads)).astype(np.float32)
    return s.reshape(n_heads // n_kv, n_kv).T


def _rmsnorm_kernel(x_ref, g_ref, o_ref):
    x = x_ref[...]
    ms = jnp.mean(x * x, axis=-1, keepdims=True)
    o_ref[...] = (x * lax.rsqrt(ms + EPS) * g_ref[...]).astype(o_ref.dtype)


def _rmsnorm(x, g, out_dtype, tm=256):
    m, d = x.shape
    return pl.pallas_call(
        _rmsnorm_kernel,
        grid=(m // tm,),
        in_specs=[pl.BlockSpec((tm, d), lambda i: (i, 0)),
                  pl.BlockSpec((1, d), lambda i: (0, 0))],
        out_specs=pl.BlockSpec((tm, d), lambda i: (i, 0)),
        out_shape=jax.ShapeDtypeStruct((m, d), out_dtype),
        compiler_params=_params("parallel"),
        name="rmsnorm",
    )(x, g.reshape(1, d).astype(F32))


def _mm_kernel(*refs, nk, has_res):
    if has_res:
        a_ref, b_ref, r_ref, o_ref = refs[:4]
        scratch = refs[4:]
    else:
        a_ref, b_ref, o_ref = refs[:3]
        r_ref = None
        scratch = refs[3:]

    def finish(acc):
        if has_res:
            acc = acc + r_ref[...]
        o_ref[...] = acc.astype(o_ref.dtype)

    part = _dot(a_ref[...], b_ref[...])
    if nk == 1:
        finish(part)
    else:
        acc_ref, = scratch
        k = pl.program_id(2)

        @pl.when(k == 0)
        def _():
            acc_ref[...] = part

        @pl.when(k > 0)
        def _():
            acc_ref[...] += part

        @pl.when(k == nk - 1)
        def _():
            finish(acc_ref[...])


def _matmul(a, b, out_dtype, *, tm, tn, tk=None, residual=None, b_index=None):
    m, kd = a.shape
    n = b.shape[-1]
    tk = kd if tk is None else tk
    nk = kd // tk
    assert m % tm == 0 and n % tn == 0 and kd % tk == 0
    if b_index is None:
        b_spec = pl.BlockSpec((tk, tn), lambda i, j, k: (k, j))
    else:
        b_spec = pl.BlockSpec((None, tk, tn), lambda i, j, k: (b_index, k, j))
    in_specs = [pl.BlockSpec((tm, tk), lambda i, j, k: (i, k)), b_spec]
    args = [a, b]
    if residual is not None:
        in_specs.append(pl.BlockSpec((tm, tn), lambda i, j, k: (i, j)))
        args.append(residual)
    return pl.pallas_call(
        functools.partial(_mm_kernel, nk=nk, has_res=residual is not None),
        grid=(m // tm, n // tn, nk),
        in_specs=in_specs,
        out_specs=pl.BlockSpec((tm, tn), lambda i, j, k: (i, j)),
        out_shape=jax.ShapeDtypeStruct((m, n), out_dtype),
        scratch_shapes=[pltpu.VMEM((tm, tn), F32)] if nk > 1 else [],
        compiler_params=_params("parallel", "parallel", "arbitrary"),
        name="matmul",
    )(*args)


def _gate_up_kernel(a_ref, wg_ref, wu_ref, o_ref):
    a = a_ref[...]
    gate = _dot(a, wg_ref[...])
    up = _dot(a, wu_ref[...])
    o_ref[...] = (gate * jax.nn.sigmoid(gate) * up).astype(o_ref.dtype)


def _gate_up(a, w_gate_up, layer, *, tm, tn):
    m, kd = a.shape
    f = w_gate_up.shape[-1] // 2
    nf = f // tn
    assert m % tm == 0 and f % tn == 0
    return pl.pallas_call(
        _gate_up_kernel,
        grid=(m // tm, nf),
        in_specs=[pl.BlockSpec((tm, kd), lambda i, j: (i, 0)),
                  pl.BlockSpec((None, kd, tn), lambda i, j: (layer, 0, j)),
                  pl.BlockSpec((None, kd, tn), lambda i, j: (layer, 0, j + nf))],
        out_specs=pl.BlockSpec((tm, tn), lambda i, j: (i, j)),
        out_shape=jax.ShapeDtypeStruct((m, f), BF16),
        compiler_params=_params("parallel", "parallel"),
        name="gate_up",
    )(a, w_gate_up, w_gate_up)


def _sb_kernel(q_ref, k_ref, v_ref, o_ref, acc_ref, later_ref, z_ref, w_ref, *, tq, tk):
    qi = pl.program_id(1)
    ratio = tq // tk
    assert ratio % 2 == 0
    q = q_ref[...]
    row = lax.broadcasted_iota(jnp.int32, (tk, tk), 0)
    col = lax.broadcasted_iota(jnp.int32, (tk, tk), 1)
    suffix = jnp.where(row >= col, 1.0, 0.0).astype(BF16)
    acc_ref[...] = jnp.zeros_like(acc_ref)
    later_ref[...] = jnp.zeros_like(later_ref)

    def logits(j, r0=0):
        start = pl.multiple_of(j * tk, tk)
        return _dot_nt(q[r0:], k_ref[pl.ds(start, tk), :]) * (-SCALE * LOG2E)

    def weights(j, nz, masked, r0=0):
        neg_abs = lax.bitcast_convert_type(
            lax.bitcast_convert_type(nz, jnp.uint32) | jnp.uint32(0x80000000), F32)
        a = jnp.minimum(nz, 0.0) - jnp.log2(1.0 + jnp.exp2(neg_abs))
        if masked:
            t_pos = qi * tq + r0 + lax.broadcasted_iota(jnp.int32, nz.shape, 0)
            s_pos = j * tk + lax.broadcasted_iota(jnp.int32, nz.shape, 1)
            before = s_pos < t_pos
            a = jnp.where(before, a, 0.0)
        within = _dot(a.astype(BF16), suffix)
        later = later_ref[r0:, :]
        w = jnp.exp2(within + jnp.concatenate([later] * (tk // HEAD_DIM), axis=1) - nz)
        later_ref[r0:, :] = later + within[:, 0:1]
        if masked:
            w = jnp.where(before, w, 0.0)
        return w.astype(BF16)

    def accumulate(j, w, r0=0):
        start = pl.multiple_of(j * tk, tk)
        acc_ref[r0:, :] += _dot(w, v_ref[pl.ds(start, tk), :])

    n_full = ratio * qi

    for b in reversed(range(ratio)):
        j = n_full + b
        accumulate(j, weights(j, logits(j, b * tk), True, b * tk), b * tk)

    def step(j, slot):
        accumulate(j + 1, w_ref[1 - slot])
        z_ref[1 - slot] = logits(jnp.maximum(j - 1, 0))
        w_ref[slot] = weights(j, z_ref[slot], False)

    def two_steps(s, carry):
        j = n_full - 1 - 2 * s
        step(j, 0)
        step(j - 1, 1)
        return carry

    z_ref[0] = logits(jnp.maximum(n_full - 1, 0))
    w_ref[1] = jnp.zeros((tq, tk), BF16)
    lax.fori_loop(0, n_full // 2, two_steps, 0)
    accumulate(0, w_ref[1])
    o_ref[...] = acc_ref[...].astype(o_ref.dtype)


def _stick_breaking(z, *, n_heads, q_col, k_col, v_col, tq=1024, tk=256):
    s = z.shape[0]
    return pl.pallas_call(
        functools.partial(_sb_kernel, tq=tq, tk=tk),
        grid=(n_heads, s // tq),
        in_specs=[pl.BlockSpec((tq, HEAD_DIM), lambda h, i: (i, q_col + h)),
                  pl.BlockSpec((s, HEAD_DIM), lambda h, i: (0, k_col + h)),
                  pl.BlockSpec((s, HEAD_DIM), lambda h, i: (0, v_col + h))],
        out_specs=pl.BlockSpec((tq, HEAD_DIM), lambda h, i: (i, h)),
        out_shape=jax.ShapeDtypeStruct((s, n_heads * HEAD_DIM), BF16),
        scratch_shapes=[pltpu.VMEM((tq, HEAD_DIM), F32),
                        pltpu.VMEM((tq, HEAD_DIM), F32),
                        pltpu.VMEM((2, tq, tk), F32),
                        pltpu.VMEM((2, tq, tk), BF16)],
        compiler_params=_params("parallel", "arbitrary"),
        name="stick_breaking",
    )(z, z, z)


def _swa_kernel(sink_ref, q_ref, kp_ref, kc_ref, vp_ref, vc_ref, o_ref, *, slopes):
    i = pl.program_id(0)
    n_kv, g_per = slopes.shape
    qq = lax.broadcasted_iota(jnp.int32, (BLK, 2 * BLK), 0)
    kk = lax.broadcasted_iota(jnp.int32, (BLK, 2 * BLK), 1)
    dist = qq + BLK - kk
    first_key = jnp.where(i > 0, 0, BLK)
    mask = (dist >= 0) & (dist < SW_WINDOW) & (kk >= first_key)
    dist_f = dist.astype(F32)
    for h in range(n_kv):
        cs = slice(h * HEAD_DIM, (h + 1) * HEAD_DIM)
        k = jnp.concatenate([kp_ref[:, cs], kc_ref[:, cs]], axis=0)
        v = jnp.concatenate([vp_ref[:, cs], vc_ref[:, cs]], axis=0)
        for g in range(g_per):
            head = h * g_per + g
            hs = slice(head * HEAD_DIM, (head + 1) * HEAD_DIM)
            s = _dot_nt(q_ref[:, hs], k) * SCALE - float(slopes[h, g]) * dist_f
            s = jnp.where(mask, s, NEG)
            sink = sink_ref[head]
            m = jnp.maximum(jnp.max(s, axis=-1, keepdims=True), sink)
            e = jnp.where(mask, jnp.exp(s - m), 0.0)
            denom = jnp.sum(e, axis=-1, keepdims=True) + jnp.exp(sink - m)
            p = e / denom
            o_ref[:, hs] = _dot(p.astype(BF16), v).astype(o_ref.dtype)


def _window_sink_attention(z, sinks, *, q_col, k_col, v_col):
    s = z.shape[0]
    qw = H_SW * HEAD_DIM
    kw = KV_SW * HEAD_DIM
    assert q_col % qw == 0 and k_col % kw == 0 and v_col % kw == 0
    qb, kb, vb = q_col // qw, k_col // kw, v_col // kw
    prev = lambda i: jnp.maximum(i - 1, 0)
    return pl.pallas_call(
        functools.partial(_swa_kernel, slopes=_alibi_slopes(H_SW, KV_SW)),
        grid=(s // BLK,),
        in_specs=[pl.BlockSpec(memory_space=pltpu.SMEM),
                  pl.BlockSpec((BLK, qw), lambda i: (i, qb)),
                  pl.BlockSpec((BLK, kw), lambda i: (prev(i), kb)),
                  pl.BlockSpec((BLK, kw), lambda i: (i, kb)),
                  pl.BlockSpec((BLK, kw), lambda i: (prev(i), vb)),
                  pl.BlockSpec((BLK, kw), lambda i: (i, vb))],
        out_specs=pl.BlockSpec((BLK, qw), lambda i: (i, 0)),
        out_shape=jax.ShapeDtypeStruct((s, qw), BF16),
        compiler_params=_params("parallel"),
        name="window_sink_attention",
    )(sinks.astype(F32), z, z, z, z, z)


def _compress_kernel(x_ref, pos_ref, w1_ref, w2_ref, o_ref):
    x = x_ref[0].astype(F32)
    half = x.shape[1]
    n_chunk = x.shape[0]
    lo = _dot((x + pos_ref[:, :half]).astype(BF16), w1_ref[0, :half, :])
    hi = _dot((x + pos_ref[:, half:]).astype(BF16), w1_ref[0, half:, :])
    pre = lo + pltpu.roll(hi, n_chunk - 1, 0)
    act = jax.nn.gelu(pre, approximate=True)
    out = _dot(act.astype(BF16), w2_ref[0])
    last = lax.broadcasted_iota(jnp.int32, out.shape, 0) == n_chunk - 1
    o_ref[0] = jnp.where(last, 0.0, out).astype(o_ref.dtype)


def _compress(x, pos, w1, w2):
    n, n_chunk, width = x.shape
    hid = w1.shape[-1]
    return pl.pallas_call(
        _compress_kernel,
        grid=(n,),
        in_specs=[pl.BlockSpec((1, n_chunk, width), lambda i: (i, 0, 0)),
                  pl.BlockSpec((1, 2 * width), lambda i: (0, 0)),
                  pl.BlockSpec((1, 2 * width, hid), lambda i: (i // KV_NSA, 0, 0)),
                  pl.BlockSpec((1, hid, HEAD_DIM), lambda i: (i // KV_NSA, 0, 0))],
        out_specs=pl.BlockSpec((1, n_chunk, HEAD_DIM), lambda i: (i, 0, 0)),
        out_shape=jax.ShapeDtypeStruct((n, n_chunk, HEAD_DIM), BF16),
        compiler_params=_params("parallel"),
        name="nsa_compress",
    )(x, pos, w1, w2)


_CMP_CHUNK = 256
_CMP_UNROLL = 4


def _split3(x):
    hi = x.astype(BF16)
    r1 = x - hi.astype(F32)
    mid = r1.astype(BF16)
    lo = (r1 - mid.astype(F32)).astype(BF16)
    return hi, mid, lo


def _nsa_cmp_kernel(slope_ref, q_ref, kc_ref, vc_ref, o_ref, sel_ref, need_ref,
                    imp_ref, qs_ref, os_ref, *, g_per, n_slc):
    h = pl.program_id(0)
    i = pl.program_id(1)
    n_cmp = kc_ref.shape[1]
    _stack_heads(q_ref, qs_ref, g_per)
    imp_ref[...] = jnp.zeros_like(imp_ref)

    def attend(width):
        kc = kc_ref[0, :width, :]
        vc = vc_ref[0, :width, :]
        t = i * BLK + lax.broadcasted_iota(jnp.int32, (BLK, width), 0)
        c_end = (lax.broadcasted_iota(jnp.int32, (BLK, width), 1) * NSA_CMP_STRIDE
                 + (NSA_CMP_LEN - 1))
        dist = t - c_end
        mask = dist >= 0
        dist_f = dist.astype(F32)

        def group(gg, carry):
            heads = [gg * _CMP_UNROLL + u for u in range(_CMP_UNROLL)]
            rows = [pl.ds(pl.multiple_of(g * BLK, BLK), BLK) for g in heads]
            s = [_dot_nt(qs_ref[r, :], kc) for r in rows]
            s = [jnp.where(mask, x * (SCALE * LOG2E) - slope_ref[h * g_per + g] * dist_f, NEG)
                 for x, g in zip(s, heads)]
            m = [jnp.max(x, axis=-1, keepdims=True) for x in s]
            e = [jnp.where(mask, jnp.exp2(x - mx), 0.0) for x, mx in zip(s, m)]
            p = [x / jnp.maximum(jnp.sum(x, axis=-1, keepdims=True), 1e-30) for x in e]
            for r, x in zip(rows, p):
                os_ref[r, :] = _dot(x.astype(BF16), vc)
            total = p[0]
            for x in p[1:]:
                total = total + x
            imp_ref[:, :width] += total
            return carry

        lax.fori_loop(0, g_per // _CMP_UNROLL, group, 0)

    n_visible = (BLK * i + BLK - NSA_CMP_LEN) // NSA_CMP_STRIDE + 1
    widths = sorted({min(w, n_cmp) for w in range(_CMP_CHUNK, n_cmp + _CMP_CHUNK, _CMP_CHUNK)})
    case = jnp.minimum((n_visible - 1) // _CMP_CHUNK, len(widths) - 1)
    for idx, width in enumerate(widths):
        pl.when(case == idx)(functools.partial(attend, width))
    for g in range(g_per):
        o_ref[:, g * HEAD_DIM:(g + 1) * HEAD_DIM] = os_ref[g * BLK:(g + 1) * BLK, :]

    r = NSA_SEL_LEN // NSA_CMP_STRIDE
    ci = lax.broadcasted_iota(jnp.int32, (n_cmp, n_slc), 0)
    cj = lax.broadcasted_iota(jnp.int32, (n_cmp, n_slc), 1)
    pool = jnp.where((ci >= r * cj - 1) & (ci <= r * cj + r - 1), 1.0, 0.0).astype(BF16)
    hi, mid, lo = _split3(imp_ref[...])
    imp = _dot(hi, pool) + (_dot(mid, pool) + _dot(lo, pool))

    blk = lax.broadcasted_iota(jnp.int32, (BLK, n_slc), 1)
    cur = (i * BLK + lax.broadcasted_iota(jnp.int32, (BLK, n_slc), 0)) // NSA_SEL_LEN
    valid = blk <= cur
    forced = (blk == 0) | (blk == cur) | (blk == cur - 1)
    score = jnp.where(valid, imp + jnp.where(forced, NSA_FORCE, 0.0), NEG)
    blk_f = blk.astype(F32)
    chosen = jnp.zeros((BLK, n_slc), F32)
    for _ in range(min(NSA_TOP_N, n_slc)):
        best = jnp.max(score, axis=-1, keepdims=True)
        first = jnp.min(jnp.where(score == best, blk_f, float(n_slc)), axis=-1, keepdims=True)
        pick = blk_f == first
        chosen = jnp.where(pick, 1.0, chosen)
        score = jnp.where(pick, -jnp.inf, score)
    chosen = jnp.where(valid, chosen, 0.0)
    sel_ref[0] = chosen.astype(sel_ref.dtype)

    n_kb = n_slc * NSA_SEL_LEN // BLK
    any_q = jnp.max(chosen, axis=0, keepdims=True)
    any_q = jnp.broadcast_to(any_q, (8, n_slc)).astype(BF16)
    pi = lax.broadcasted_iota(jnp.int32, (n_slc, n_kb), 0)
    pj = lax.broadcasted_iota(jnp.int32, (n_slc, n_kb), 1)
    per_kb = _dot(any_q, jnp.where(pi * NSA_SEL_LEN // BLK == pj, 1.0, 0.0).astype(BF16))
    per_kb = jnp.where(per_kb > 0.5, 1.0, 0.0).astype(BF16)
    n_word = need_ref.shape[-1]
    bi = lax.broadcasted_iota(jnp.int32, (n_kb, n_word), 0)
    bw = lax.broadcasted_iota(jnp.int32, (n_kb, n_word), 1)
    weight = jnp.where(bi // 16 == bw, jnp.left_shift(1, bi % 16), 0).astype(F32).astype(BF16)
    words = _dot(per_kb, weight)
    need_ref[0, 0] = words.astype(jnp.int32)


def _nsa_compressed(z, k_cmp, v_cmp, slopes):
    s = z.shape[0]
    n_slc = s // NSA_SEL_LEN
    g_per = H_NSA // KV_NSA
    qw = g_per * HEAD_DIM
    n_cmp = k_cmp.shape[1]
    n_word = 128
    assert (s // BLK) <= 16 * n_word
    return pl.pallas_call(
        functools.partial(_nsa_cmp_kernel, g_per=g_per, n_slc=n_slc),
        grid=(KV_NSA, s // BLK),
        in_specs=[pl.BlockSpec(memory_space=pltpu.SMEM),
                  pl.BlockSpec((BLK, qw), lambda h, i: (i, h)),
                  pl.BlockSpec((1, n_cmp, HEAD_DIM), lambda h, i: (h, 0, 0)),
                  pl.BlockSpec((1, n_cmp, HEAD_DIM), lambda h, i: (h, 0, 0))],
        out_specs=[pl.BlockSpec((BLK, qw), lambda h, i: (i, h)),
                   pl.BlockSpec((1, BLK, n_slc), lambda h, i: (h, i, 0)),
                   pl.BlockSpec((1, 1, 8, n_word), lambda h, i: (h, i, 0, 0))],
        out_shape=[jax.ShapeDtypeStruct((s, H_NSA * HEAD_DIM), F32),
                   jax.ShapeDtypeStruct((KV_NSA, s, n_slc), BF16),
                   jax.ShapeDtypeStruct((KV_NSA, s // BLK, 8, n_word), jnp.int32)],
        scratch_shapes=[pltpu.VMEM((BLK, n_cmp), F32),
                        pltpu.VMEM((g_per * BLK, HEAD_DIM), BF16),
                        pltpu.VMEM((g_per * BLK, HEAD_DIM), F32)],
        compiler_params=_params("parallel", "arbitrary"),
        name="nsa_compressed",
    )(slopes, z, k_cmp, v_cmp)


def _stack_heads(q_ref, qs_ref, g_per):
    for g in range(g_per):
        qs_ref[g * BLK:(g + 1) * BLK, :] = q_ref[:, g * HEAD_DIM:(g + 1) * HEAD_DIM]


_ROW_SPLIT = 2


def _online_visit(i, blocks, qs_ref, k_ref, v_ref, slope_ref, m_ref, l_ref, acc_ref, g_per):
    nb = len(blocks)
    starts = [pl.multiple_of(j * BLK, BLK) for j, _ in blocks]
    k = jnp.concatenate([k_ref[pl.ds(st, BLK), :] for st in starts], axis=0)
    v = jnp.concatenate([v_ref[pl.ds(st, BLK), :] for st in starts], axis=0)
    v_ones = jnp.concatenate([v, jnp.ones((nb * BLK, HEAD_DIM), BF16)], axis=1)
    qq = lax.broadcasted_iota(jnp.int32, (BLK, BLK), 0) - lax.broadcasted_iota(jnp.int32, (BLK, BLK), 1)
    dists = [(i - j) * BLK + qq for j, _ in blocks]
    ok1 = jnp.concatenate([key_ok(d) for (_, key_ok), d in zip(blocks, dists)], axis=1)
    dist1 = jnp.concatenate([d.astype(F32) for d in dists], axis=1)
    g_sub = g_per // _ROW_SPLIT
    ok = jnp.concatenate([ok1] * g_sub, axis=0) > 0.5
    dist_f = jnp.concatenate([dist1] * g_sub, axis=0)

    groups = [slice(r * g_sub * BLK, (r + 1) * g_sub * BLK) for r in range(_ROW_SPLIT)]
    s = [_dot_nt(qs_ref[rows, :], k) for rows in groups]
    s = [jnp.where(ok, x * (SCALE * LOG2E) - jnp.concatenate([slope_ref[0, rows, :]] * nb, axis=1) * dist_f, NEG)
         for x, rows in zip(s, groups)]
    m_prev = [m_ref[rows, :] for rows in groups]
    m_new = [jnp.maximum(mp, jnp.max(x, axis=-1, keepdims=True)) for mp, x in zip(m_prev, s)]
    alpha = [jnp.exp2(mp - mn) for mp, mn in zip(m_prev, m_new)]
    p = [jnp.where(ok, jnp.exp2(x - jnp.concatenate([mn] * nb, axis=1)), 0.0) for x, mn in zip(s, m_new)]
    pv = [_dot(x.astype(BF16), v_ones) for x in p]
    for rows, a, y, mn in zip(groups, alpha, pv, m_new):
        l_ref[rows, :] = a * l_ref[rows, :] + y[:, HEAD_DIM:]
        acc_ref[rows, :] = a * acc_ref[rows, :] + y[:, :HEAD_DIM]
        m_ref[rows, :] = mn


def _online_init(m_ref, l_ref, acc_ref):
    m_ref[...] = jnp.full_like(m_ref, NEG)
    l_ref[...] = jnp.zeros_like(l_ref)
    acc_ref[...] = jnp.zeros_like(acc_ref)


_SEL_GROUP = 4


def _nsa_sel_kernel(need_ref, q_ref, sel_ref, slope_ref, k_ref, v_ref, o_ref,
                    qs_ref, m_ref, l_ref, acc_ref, list_ref, *, g_per, n_qblk, n_word):
    h = pl.program_id(0)
    i = pl.program_id(1)
    _stack_heads(q_ref, qs_ref, g_per)
    _online_init(m_ref, l_ref, acc_ref)
    sel = sel_ref[0]
    n_slc = sel.shape[1]
    per_blk = BLK // NSA_SEL_LEN
    eb = lax.broadcasted_iota(jnp.int32, (n_slc, BLK), 0)
    el = lax.broadcasted_iota(jnp.int32, (n_slc, BLK), 1) // NSA_SEL_LEN
    base = (h * n_qblk + i) * n_word

    def scan(j, n):
        word = need_ref[base + jnp.right_shift(j, 4)]
        needed = jnp.bitwise_and(jnp.right_shift(word, jnp.bitwise_and(j, 15)), 1)

        @pl.when(needed == 1)
        def _():
            list_ref[n] = j
        return n + needed

    n_needed = lax.fori_loop(0, i + 1, scan, 0)

    def visit(first, count):
        blocks = []
        for u in range(count):
            j = list_ref[first + u]
            expand = jnp.where(eb == per_blk * j + el, 1.0, 0.0).astype(BF16)
            picked = _dot(sel, expand)
            blocks.append((j, functools.partial(
                lambda dist, picked: jnp.where(dist >= 0, picked, 0.0), picked=picked)))
        _online_visit(i, blocks, qs_ref, k_ref, v_ref, slope_ref, m_ref, l_ref, acc_ref, g_per)

    def group(t, carry):
        visit(t * _SEL_GROUP, _SEL_GROUP)
        return carry

    n_groups = n_needed // _SEL_GROUP
    lax.fori_loop(0, n_groups, group, 0)
    done = n_groups * _SEL_GROUP
    size = _SEL_GROUP // 2
    while size >= 1:
        has = jnp.bitwise_and(n_needed, size) != 0
        pl.when(has)(functools.partial(visit, done, size))
        done = done + jnp.where(has, size, 0)
        size //= 2
    out = acc_ref[...] / jnp.maximum(l_ref[...], 1e-30)
    for g in range(g_per):
        o_ref[:, g * HEAD_DIM:(g + 1) * HEAD_DIM] = out[g * BLK:(g + 1) * BLK, :]


def _nsa_selected(z, sel, need, slope_rows, *, k_col, v_col):
    s = z.shape[0]
    g_per = H_NSA // KV_NSA
    qw = g_per * HEAD_DIM
    rows = g_per * BLK
    n_slc = sel.shape[-1]
    n_qblk = s // BLK
    n_word = -(-n_qblk // 16)
    need = need[:, :, 0, :n_word]
    return pl.pallas_call(
        functools.partial(_nsa_sel_kernel, g_per=g_per, n_qblk=n_qblk, n_word=n_word),
        grid=(KV_NSA, n_qblk),
        in_specs=[pl.BlockSpec(memory_space=pltpu.SMEM),
                  pl.BlockSpec((BLK, qw), lambda h, i: (i, h)),
                  pl.BlockSpec((1, BLK, n_slc), lambda h, i: (h, i, 0)),
                  pl.BlockSpec((1, rows, HEAD_DIM), lambda h, i: (h, 0, 0)),
                  pl.BlockSpec((s, HEAD_DIM), lambda h, i: (0, k_col + h)),
                  pl.BlockSpec((s, HEAD_DIM), lambda h, i: (0, v_col + h))],
        out_specs=pl.BlockSpec((BLK, qw), lambda h, i: (i, h)),
        out_shape=jax.ShapeDtypeStruct((s, H_NSA * HEAD_DIM), F32),
        scratch_shapes=[pltpu.VMEM((rows, HEAD_DIM), BF16),
                        pltpu.VMEM((rows, HEAD_DIM), F32),
                        pltpu.VMEM((rows, HEAD_DIM), F32),
                        pltpu.VMEM((rows, HEAD_DIM), F32),
                        pltpu.SMEM((n_qblk + _SEL_GROUP,), jnp.int32)],
        compiler_params=_params("parallel", "arbitrary"),
        name="nsa_selected",
    )(need.reshape(-1), z, sel, slope_rows, z, z)


def _nsa_win_kernel(q_ref, slope_ref, k_ref, v_ref, gate_ref, ocmp_ref, oslc_ref, o_ref,
                    qs_ref, m_ref, l_ref, acc_ref, *, g_per):
    i = pl.program_id(1)
    _stack_heads(q_ref, qs_ref, g_per)
    _online_init(m_ref, l_ref, acc_ref)
    n_prev = -(-(NSA_WINDOW - 1) // BLK)

    blocks = []
    for step in range(n_prev + 1):
        j = i - n_prev + step
        present = jnp.where(j >= 0, 1.0, 0.0)
        blocks.append((jnp.maximum(j, 0), functools.partial(
            lambda dist, present: jnp.where((dist >= 0) & (dist < NSA_WINDOW), present, 0.0),
            present=present)))
    _online_visit(i, blocks, qs_ref, k_ref, v_ref, slope_ref, m_ref, l_ref, acc_ref, g_per)
    o_win = acc_ref[...] / jnp.maximum(l_ref[...], 1e-30)
    gates = jax.nn.sigmoid(gate_ref[...])
    for g in range(g_per):
        hs = slice(g * HEAD_DIM, (g + 1) * HEAD_DIM)
        o = (gates[:, 3 * g:3 * g + 1] * ocmp_ref[:, hs]
             + gates[:, 3 * g + 1:3 * g + 2] * oslc_ref[:, hs]
             + gates[:, 3 * g + 2:3 * g + 3] * o_win[g * BLK:(g + 1) * BLK, :])
        o_ref[:, hs] = o.astype(o_ref.dtype)


def _nsa_window_combine(z, gate_logits, o_cmp, o_slc, slope_rows, *, k_col, v_col):
    s = z.shape[0]
    g_per = H_NSA // KV_NSA
    qw = g_per * HEAD_DIM
    rows = g_per * BLK
    head_blk = lambda h, i: (i, h)
    return pl.pallas_call(
        functools.partial(_nsa_win_kernel, g_per=g_per),
        grid=(KV_NSA, s // BLK),
        in_specs=[pl.BlockSpec((BLK, qw), head_blk),
                  pl.BlockSpec((1, rows, HEAD_DIM), lambda h, i: (h, 0, 0)),
                  pl.BlockSpec((s, HEAD_DIM), lambda h, i: (0, k_col + h)),
                  pl.BlockSpec((s, HEAD_DIM), lambda h, i: (0, v_col + h)),
                  pl.BlockSpec((BLK, 128), head_blk),
                  pl.BlockSpec((BLK, qw), head_blk),
                  pl.BlockSpec((BLK, qw), head_blk)],
        out_specs=pl.BlockSpec((BLK, qw), head_blk),
        out_shape=jax.ShapeDtypeStruct((s, H_NSA * HEAD_DIM), BF16),
        scratch_shapes=[pltpu.VMEM((rows, HEAD_DIM), BF16),
                        pltpu.VMEM((rows, HEAD_DIM), F32),
                        pltpu.VMEM((rows, HEAD_DIM), F32),
                        pltpu.VMEM((rows, HEAD_DIM), F32)],
        compiler_params=_params("parallel", "arbitrary"),
        name="nsa_window_combine",
    )(z, slope_rows, z, z, gate_logits, o_cmp, o_slc)


def _ab_mixer(h, w_in, sinks, w_out, x_res):
    z = _matmul(h, w_in.astype(BF16), BF16, tm=1024, tn=768)
    d_sb = H_SB * HEAD_DIM
    o_a = _stick_breaking(z, n_heads=H_SB, q_col=0, k_col=H_SB, v_col=2 * H_SB)
    q_b = 3 * d_sb
    k_b = q_b + H_SW * HEAD_DIM
    v_b = k_b + KV_SW * HEAD_DIM
    o_b = _window_sink_attention(z, sinks, q_col=q_b, k_col=k_b, v_col=v_b)
    o = jnp.concatenate([o_a, o_b], axis=-1)
    return _matmul(o, w_out.astype(BF16), F32, tm=1024, tn=512, residual=x_res)


def _nsa_mixer(h, w_in, cmp_pos, wk1, wk2, wv1, wv2, w_out, x_res):
    s = h.shape[0]
    dq = H_NSA * HEAD_DIM
    dkv = KV_NSA * HEAD_DIM
    g_per = H_NSA // KV_NSA
    main = dq + 6 * dkv
    z = _matmul(h, w_in[:, :main].astype(BF16), BF16, tm=1024, tn=512)
    w_gate = w_in[:, main:].reshape(-1, KV_NSA, 3 * g_per)
    w_gate = jnp.pad(w_gate, ((0, 0), (0, 0), (0, 128 - 3 * g_per))).reshape(-1, KV_NSA * 128)
    gate_logits = _matmul(h, w_gate.astype(BF16), F32, tm=1024, tn=KV_NSA * 128)

    n_chunk = s // NSA_CMP_STRIDE
    to_chunks = lambda c: (z[:, c:c + dkv].reshape(n_chunk, NSA_CMP_STRIDE, KV_NSA, HEAD_DIM)
                           .transpose(2, 0, 1, 3).reshape(KV_NSA, n_chunk, NSA_CMP_STRIDE * HEAD_DIM))
    chunks = jnp.concatenate([to_chunks(dq), to_chunks(dq + dkv)], axis=0)
    kv_cmp = _compress(chunks, cmp_pos.reshape(1, -1).astype(F32),
                       jnp.stack([wk1, wv1]).astype(BF16), jnp.stack([wk2, wv2]).astype(BF16))
    k_cmp, v_cmp = kv_cmp[:KV_NSA], kv_cmp[KV_NSA:]

    slopes = _alibi_slopes(H_NSA, KV_NSA) * np.float32(LOG2E)
    slope_rows = jnp.asarray(np.repeat(np.repeat(slopes, BLK, axis=1)[..., None], HEAD_DIM, axis=2))
    o_cmp, sel, need = _nsa_compressed(z, k_cmp, v_cmp, jnp.asarray(slopes.reshape(-1)))
    col = lambda off: (dq + off * dkv) // HEAD_DIM
    o_slc = _nsa_selected(z, sel, need, slope_rows, k_col=col(2), v_col=col(3))
    o = _nsa_window_combine(z, gate_logits, o_cmp, o_slc, slope_rows, k_col=col(4), v_col=col(5))
    return _matmul(o, w_out.astype(BF16), F32, tm=1024, tn=512, residual=x_res)


def _swiglu(h, w_gate_up, w_down, layer, x_res):
    act = _gate_up(h, w_gate_up, layer, tm=2048, tn=256)
    return _matmul(act, w_down, F32, tm=512, tn=512, residual=x_res, b_index=layer)


def kernel(x, attn_norm, ffn_norm, w_gate_up, w_down, ab_w_in, ab_sinks, ab_w_out, nsa_w_in,
           nsa_cmp_pos, nsa_cmp_wk1, nsa_cmp_wk2, nsa_cmp_wv1, nsa_cmp_wv2, nsa_w_out, final_norm):
    b, s, d = x.shape
    w_gate_up = w_gate_up.astype(BF16)
    w_down = w_down.astype(BF16)
    outs = []
    for bi in range(b):
        xb = x[bi]
        for layer in range(attn_norm.shape[0]):
            h = _rmsnorm(xb, attn_norm[layer], BF16)
            if layer % 2 == 0:
                e = layer // 2
                xb = _ab_mixer(h, ab_w_in[e], ab_sinks[e], ab_w_out[e], xb)
            else:
                o = layer // 2
                xb = _nsa_mixer(h, nsa_w_in[o], nsa_cmp_pos[o], nsa_cmp_wk1[o], nsa_cmp_wk2[o],
                                nsa_cmp_wv1[o], nsa_cmp_wv2[o], nsa_w_out[o], xb)
            h = _rmsnorm(xb, ffn_norm[layer], BF16)
            xb = _swiglu(h, w_gate_up, w_down, layer, xb)
        outs.append(_rmsnorm(xb, final_norm, F32))
    return jnp.stack(outs, axis=0)
```

```python
import functools

import numpy as np
import jax
import jax.numpy as jnp
from jax import lax
from jax.experimental import pallas as pl
from jax.experimental.pallas import tpu as pltpu

F32 = jnp.float32
BF16 = jnp.bfloat16

HEAD_DIM = 128
BLK = 128
EPS = 1e-5
NEG = -1e30
SCALE = HEAD_DIM ** -0.5
LOG2E = 1.4426950408889634

H_SB = 8
H_SW = 24
KV_SW = 3
SW_WINDOW = 128

H_NSA = 32
KV_NSA = 2
NSA_CMP_LEN = 32
NSA_CMP_STRIDE = 16
NSA_SEL_LEN = 64
NSA_TOP_N = 8
NSA_WINDOW = 512
NSA_FORCE = 1e4

VMEM_LIMIT_BYTES = 52 * 2 ** 20

_NT = (((1,), (1,)), ((), ()))


def _params(*sem):
    return pltpu.CompilerParams(dimension_semantics=sem,
                                vmem_limit_bytes=VMEM_LIMIT_BYTES)


def _dot(a, b):
    return jnp.dot(a, b, preferred_element_type=F32)


def _dot_nt(a, b):
    return lax.dot_general(a, b, _NT, preferred_element_type=F32)


def _alibi_slopes(n_heads, n_kv):
    s = np.exp2(np.float32(-8.0) * np.arange(1, n_heads + 1, dtype=np.float32)
                / np.float32(n_heads)).astype(np.float32)
    return s.reshape(n_heads // n_kv, n_kv).T


def _rmsnorm_kernel(x_ref, g_ref, o_ref):
    x = x_ref[...]
    ms = jnp.mean(x * x, axis=-1, keepdims=True)
    o_ref[...] = (x * lax.rsqrt(ms + EPS) * g_ref[...]).astype(o_ref.dtype)


def _rmsnorm(x, g, out_dtype, tm=256):
    m, d = x.shape
    return pl.pallas_call(
        _rmsnorm_kernel,
        grid=(m // tm,),
        in_specs=[pl.BlockSpec((tm, d), lambda i: (i, 0)),
                  pl.BlockSpec((1, d), lambda i: (0, 0))],
        out_specs=pl.BlockSpec((tm, d), lambda i: (i, 0)),
        out_shape=jax.ShapeDtypeStruct((m, d), out_dtype),
        compiler_params=_params("parallel"),
        name="rmsnorm",
    )(x, g.reshape(1, d).astype(F32))


def _mm_kernel(*refs, nk, has_res):
    if has_res:
        a_ref, b_ref, r_ref, o_ref = refs[:4]
        scratch = refs[4:]
    else:
        a_ref, b_ref, o_ref = refs[:3]
        r_ref = None
        scratch = refs[3:]

    def finish(acc):
        if has_res:
            acc = acc + r_ref[...]
        o_ref[...] = acc.astype(o_ref.dtype)

    part = _dot(a_ref[...], b_ref[...])
    if nk == 1:
        finish(part)
    else:
        acc_ref, = scratch
        k = pl.program_id(2)

        @pl.when(k == 0)
        def _():
            acc_ref[...] = part

        @pl.when(k > 0)
        def _():
            acc_ref[...] += part

        @pl.when(k == nk - 1)
        def _():
            finish(acc_ref[...])


def _matmul(a, b, out_dtype, *, tm, tn, tk=None, residual=None, b_index=None):
    m, kd = a.shape
    n = b.shape[-1]
    tk = kd if tk is None else tk
    nk = kd // tk
    assert m % tm == 0 and n % tn == 0 and kd % tk == 0
    if b_index is None:
        b_spec = pl.BlockSpec((tk, tn), lambda i, j, k: (k, j))
    else:
        b_spec = pl.BlockSpec((None, tk, tn), lambda i, j, k: (b_index, k, j))
    in_specs = [pl.BlockSpec((tm, tk), lambda i, j, k: (i, k)), b_spec]
    args = [a, b]
    if residual is not None:
        in_specs.append(pl.BlockSpec((tm, tn), lambda i, j, k: (i, j)))
        args.append(residual)
    return pl.pallas_call(
        functools.partial(_mm_kernel, nk=nk, has_res=residual is not None),
        grid=(m // tm, n // tn, nk),
        in_specs=in_specs,
        out_specs=pl.BlockSpec((tm, tn), lambda i, j, k: (i, j)),
        out_shape=jax.ShapeDtypeStruct((m, n), out_dtype),
        scratch_shapes=[pltpu.VMEM((tm, tn), F32)] if nk > 1 else [],
        compiler_params=_params("parallel", "parallel", "arbitrary"),
        name="matmul",
    )(*args)


def _gate_up_kernel(a_ref, wg_ref, wu_ref, o_ref):
    a = a_ref[...]
    gate = _dot(a, wg_ref[...])
    up = _dot(a, wu_ref[...])
    o_ref[...] = (gate * jax.nn.sigmoid(gate) * up).astype(o_ref.dtype)


def _gate_up(a, w_gate_up, layer, *, tm, tn):
    m, kd = a.shape
    f = w_gate_up.shape[-1] // 2
    nf = f // tn
    assert m % tm == 0 and f % tn == 0
    return pl.pallas_call(
        _gate_up_kernel,
        grid=(m // tm, nf),
        in_specs=[pl.BlockSpec((tm, kd), lambda i, j: (i, 0)),
                  pl.BlockSpec((None, kd, tn), lambda i, j: (layer, 0, j)),
                  pl.BlockSpec((None, kd, tn), lambda i, j: (layer, 0, j + nf))],
        out_specs=pl.BlockSpec((tm, tn), lambda i, j: (i, j)),
        out_shape=jax.ShapeDtypeStruct((m, f), BF16),
        compiler_params=_params("parallel", "parallel"),
        name="gate_up",
    )(a, w_gate_up, w_gate_up)


_SB_DEAD = -200.0


def _sb_kernel(q_ref, k_ref, v_ref, o_ref, acc_ref, later_ref, z_ref, w_ref, *, tq, tk):
    qi = pl.program_id(1)
    ratio = tq // tk
    assert ratio % 2 == 0
    q = q_ref[...]
    row = lax.broadcasted_iota(jnp.int32, (tk, tk), 0)
    col = lax.broadcasted_iota(jnp.int32, (tk, tk), 1)
    suffix = jnp.where(row >= col, 1.0, 0.0).astype(BF16)
    acc_ref[...] = jnp.zeros_like(acc_ref)
    later_ref[...] = jnp.zeros_like(later_ref)

    def logits(j, r0=0):
        start = pl.multiple_of(j * tk, tk)
        return _dot_nt(q[r0:], k_ref[pl.ds(start, tk), :]) * (-SCALE * LOG2E)

    def weights(j, nz, masked, r0=0):
        neg_abs = lax.bitcast_convert_type(
            lax.bitcast_convert_type(nz, jnp.uint32) | jnp.uint32(0x80000000), F32)
        a = jnp.minimum(nz, 0.0) - jnp.log2(1.0 + jnp.exp2(neg_abs))
        if masked:
            t_pos = qi * tq + r0 + lax.broadcasted_iota(jnp.int32, nz.shape, 0)
            s_pos = j * tk + lax.broadcasted_iota(jnp.int32, nz.shape, 1)
            before = s_pos < t_pos
            a = jnp.where(before, a, 0.0)
        within = _dot(a.astype(BF16), suffix)
        later = later_ref[r0:, :]
        w = jnp.exp2(within + jnp.concatenate([later] * (tk // HEAD_DIM), axis=1) - nz)
        later_ref[r0:, :] = later + within[:, 0:1]
        if masked:
            w = jnp.where(before, w, 0.0)
        return w.astype(BF16)

    def accumulate(j, w, r0=0):
        start = pl.multiple_of(j * tk, tk)
        acc_ref[r0:, :] += _dot(w, v_ref[pl.ds(start, tk), :])

    n_full = ratio * qi

    for b in reversed(range(ratio)):
        j = n_full + b
        accumulate(j, weights(j, logits(j, b * tk), True, b * tk), b * tk)

    def step(j, slot):
        accumulate(j + 1, w_ref[1 - slot])
        z_ref[1 - slot] = logits(jnp.maximum(j - 1, 0))
        w_ref[slot] = weights(j, z_ref[slot], False)

    def remaining_mass():
        return jnp.max(later_ref[...])

    def two_steps(carry):
        s, _ = carry
        j = n_full - 1 - 2 * s
        step(j, 0)
        step(j - 1, 1)
        return s + 1, remaining_mass()

    def unfinished(carry):
        s, mass = carry
        return jnp.logical_and(s < n_full // 2, mass > _SB_DEAD)

    z_ref[0] = logits(jnp.maximum(n_full - 1, 0))
    w_ref[1] = jnp.zeros((tq, tk), BF16)
    trips, _ = lax.while_loop(unfinished, two_steps, (jnp.int32(0), remaining_mass()))
    accumulate(n_full - 2 * trips, w_ref[1])
    o_ref[...] = acc_ref[...].astype(o_ref.dtype)


def _stick_breaking(z, *, n_heads, q_col, k_col, v_col, tq=1024, tk=256):
    s = z.shape[0]
    return pl.pallas_call(
        functools.partial(_sb_kernel, tq=tq, tk=tk),
        grid=(n_heads, s // tq),
        in_specs=[pl.BlockSpec((tq, HEAD_DIM), lambda h, i: (i, q_col + h)),
                  pl.BlockSpec((s, HEAD_DIM), lambda h, i: (0, k_col + h)),
                  pl.BlockSpec((s, HEAD_DIM), lambda h, i: (0, v_col + h))],
        out_specs=pl.BlockSpec((tq, HEAD_DIM), lambda h, i: (i, h)),
        out_shape=jax.ShapeDtypeStruct((s, n_heads * HEAD_DIM), BF16),
        scratch_shapes=[pltpu.VMEM((tq, HEAD_DIM), F32),
                        pltpu.VMEM((tq, HEAD_DIM), F32),
                        pltpu.VMEM((2, tq, tk), F32),
                        pltpu.VMEM((2, tq, tk), BF16)],
        compiler_params=_params("parallel", "arbitrary"),
        name="stick_breaking",
    )(z, z, z)


def _swa_kernel(sink_ref, q_ref, kp_ref, kc_ref, vp_ref, vc_ref, o_ref, *, slopes):
    i = pl.program_id(0)
    n_kv, g_per = slopes.shape
    qq = lax.broadcasted_iota(jnp.int32, (BLK, 2 * BLK), 0)
    kk = lax.broadcasted_iota(jnp.int32, (BLK, 2 * BLK), 1)
    dist = qq + BLK - kk
    first_key = jnp.where(i > 0, 0, BLK)
    mask = (dist >= 0) & (dist < SW_WINDOW) & (kk >= first_key)
    dist_f = dist.astype(F32)
    for h in range(n_kv):
        cs = slice(h * HEAD_DIM, (h + 1) * HEAD_DIM)
        k = jnp.concatenate([kp_ref[:, cs], kc_ref[:, cs]], axis=0)
        v = jnp.concatenate([vp_ref[:, cs], vc_ref[:, cs]], axis=0)
        for g in range(g_per):
            head = h * g_per + g
            hs = slice(head * HEAD_DIM, (head + 1) * HEAD_DIM)
            s = _dot_nt(q_ref[:, hs], k) * SCALE - float(slopes[h, g]) * dist_f
            s = jnp.where(mask, s, NEG)
            sink = sink_ref[head]
            m = jnp.maximum(jnp.max(s, axis=-1, keepdims=True), sink)
            e = jnp.where(mask, jnp.exp(s - m), 0.0)
            denom = jnp.sum(e, axis=-1, keepdims=True) + jnp.exp(sink - m)
            p = e / denom
            o_ref[:, hs] = _dot(p.astype(BF16), v).astype(o_ref.dtype)


def _window_sink_attention(z, sinks, *, q_col, k_col, v_col):
    s = z.shape[0]
    qw = H_SW * HEAD_DIM
    kw = KV_SW * HEAD_DIM
    assert q_col % qw == 0 and k_col % kw == 0 and v_col % kw == 0
    qb, kb, vb = q_col // qw, k_col // kw, v_col // kw
    prev = lambda i: jnp.maximum(i - 1, 0)
    return pl.pallas_call(
        functools.partial(_swa_kernel, slopes=_alibi_slopes(H_SW, KV_SW)),
        grid=(s // BLK,),
        in_specs=[pl.BlockSpec(memory_space=pltpu.SMEM),
                  pl.BlockSpec((BLK, qw), lambda i: (i, qb)),
                  pl.BlockSpec((BLK, kw), lambda i: (prev(i), kb)),
                  pl.BlockSpec((BLK, kw), lambda i: (i, kb)),
                  pl.BlockSpec((BLK, kw), lambda i: (prev(i), vb)),
                  pl.BlockSpec((BLK, kw), lambda i: (i, vb))],
        out_specs=pl.BlockSpec((BLK, qw), lambda i: (i, 0)),
        out_shape=jax.ShapeDtypeStruct((s, qw), BF16),
        compiler_params=_params("parallel"),
        name="window_sink_attention",
    )(sinks.astype(F32), z, z, z, z, z)


def _compress_kernel(x_ref, pos_ref, w1_ref, w2_ref, o_ref):
    x = x_ref[0].astype(F32)
    half = x.shape[1]
    n_chunk = x.shape[0]
    lo = _dot((x + pos_ref[:, :half]).astype(BF16), w1_ref[0, :half, :])
    hi = _dot((x + pos_ref[:, half:]).astype(BF16), w1_ref[0, half:, :])
    pre = lo + pltpu.roll(hi, n_chunk - 1, 0)
    act = jax.nn.gelu(pre, approximate=True)
    out = _dot(act.astype(BF16), w2_ref[0])
    last = lax.broadcasted_iota(jnp.int32, out.shape, 0) == n_chunk - 1
    o_ref[0] = jnp.where(last, 0.0, out).astype(o_ref.dtype)


def _compress(x, pos, w1, w2):
    n, n_chunk, width = x.shape
    hid = w1.shape[-1]
    return pl.pallas_call(
        _compress_kernel,
        grid=(n,),
        in_specs=[pl.BlockSpec((1, n_chunk, width), lambda i: (i, 0, 0)),
                  pl.BlockSpec((1, 2 * width), lambda i: (0, 0)),
                  pl.BlockSpec((1, 2 * width, hid), lambda i: (i // KV_NSA, 0, 0)),
                  pl.BlockSpec((1, hid, HEAD_DIM), lambda i: (i // KV_NSA, 0, 0))],
        out_specs=pl.BlockSpec((1, n_chunk, HEAD_DIM), lambda i: (i, 0, 0)),
        out_shape=jax.ShapeDtypeStruct((n, n_chunk, HEAD_DIM), BF16),
        compiler_params=_params("parallel"),
        name="nsa_compress",
    )(x, pos, w1, w2)


_CMP_CHUNK = 256
_CMP_UNROLL = 4


def _split3(x):
    hi = x.astype(BF16)
    r1 = x - hi.astype(F32)
    mid = r1.astype(BF16)
    lo = (r1 - mid.astype(F32)).astype(BF16)
    return hi, mid, lo


def _nsa_cmp_kernel(slope_ref, q_ref, kc_ref, vc_ref, o_ref, sel_ref, need_ref,
                    imp_ref, qs_ref, os_ref, *, g_per, n_slc):
    h = pl.program_id(0)
    i = pl.program_id(1)
    n_cmp = kc_ref.shape[1]
    _stack_heads(q_ref, qs_ref, g_per)
    imp_ref[...] = jnp.zeros_like(imp_ref)

    def attend(width):
        kc = kc_ref[0, :width, :]
        vc = vc_ref[0, :width, :]
        t = i * BLK + lax.broadcasted_iota(jnp.int32, (BLK, width), 0)
        c_end = (lax.broadcasted_iota(jnp.int32, (BLK, width), 1) * NSA_CMP_STRIDE
                 + (NSA_CMP_LEN - 1))
        dist = t - c_end
        mask = dist >= 0
        dist_f = dist.astype(F32)

        def group(gg, carry):
            heads = [gg * _CMP_UNROLL + u for u in range(_CMP_UNROLL)]
            rows = [pl.ds(pl.multiple_of(g * BLK, BLK), BLK) for g in heads]
            s = [_dot_nt(qs_ref[r, :], kc) for r in rows]
            s = [jnp.where(mask, x * (SCALE * LOG2E) - slope_ref[h * g_per + g] * dist_f, NEG)
                 for x, g in zip(s, heads)]
            m = [jnp.max(x, axis=-1, keepdims=True) for x in s]
            e = [jnp.where(mask, jnp.exp2(x - mx), 0.0) for x, mx in zip(s, m)]
            p = [x / jnp.maximum(jnp.sum(x, axis=-1, keepdims=True), 1e-30) for x in e]
            for r, x in zip(rows, p):
                os_ref[r, :] = _dot(x.astype(BF16), vc)
            total = p[0]
            for x in p[1:]:
                total = total + x
            imp_ref[:, :width] += total
            return carry

        lax.fori_loop(0, g_per // _CMP_UNROLL, group, 0)

    n_visible = (BLK * i + BLK - NSA_CMP_LEN) // NSA_CMP_STRIDE + 1
    widths = sorted({min(w, n_cmp) for w in range(_CMP_CHUNK, n_cmp + _CMP_CHUNK, _CMP_CHUNK)})
    case = jnp.minimum((n_visible - 1) // _CMP_CHUNK, len(widths) - 1)
    for idx, width in enumerate(widths):
        pl.when(case == idx)(functools.partial(attend, width))
    for g in range(g_per):
        o_ref[:, g * HEAD_DIM:(g + 1) * HEAD_DIM] = os_ref[g * BLK:(g + 1) * BLK, :]

    r = NSA_SEL_LEN // NSA_CMP_STRIDE
    ci = lax.broadcasted_iota(jnp.int32, (n_cmp, n_slc), 0)
    cj = lax.broadcasted_iota(jnp.int32, (n_cmp, n_slc), 1)
    pool = jnp.where((ci >= r * cj - 1) & (ci <= r * cj + r - 1), 1.0, 0.0).astype(BF16)
    hi, mid, lo = _split3(imp_ref[...])
    imp = _dot(hi, pool) + (_dot(mid, pool) + _dot(lo, pool))

    blk = lax.broadcasted_iota(jnp.int32, (BLK, n_slc), 1)
    cur = (i * BLK + lax.broadcasted_iota(jnp.int32, (BLK, n_slc), 0)) // NSA_SEL_LEN
    valid = blk <= cur
    forced = (blk == 0) | (blk == cur) | (blk == cur - 1)
    score = jnp.where(valid, imp + jnp.where(forced, NSA_FORCE, 0.0), NEG)
    blk_f = blk.astype(F32)
    chosen = jnp.zeros((BLK, n_slc), F32)
    for _ in range(min(NSA_TOP_N, n_slc)):
        best = jnp.max(score, axis=-1, keepdims=True)
        first = jnp.min(jnp.where(score == best, blk_f, float(n_slc)), axis=-1, keepdims=True)
        pick = blk_f == first
        chosen = jnp.where(pick, 1.0, chosen)
        score = jnp.where(pick, -jnp.inf, score)
    chosen = jnp.where(valid, chosen, 0.0)
    sel_ref[0] = chosen.astype(sel_ref.dtype)

    n_kb = n_slc * NSA_SEL_LEN // BLK
    any_q = jnp.max(chosen, axis=0, keepdims=True)
    any_q = jnp.broadcast_to(any_q, (8, n_slc)).astype(BF16)
    pi = lax.broadcasted_iota(jnp.int32, (n_slc, n_kb), 0)
    pj = lax.broadcasted_iota(jnp.int32, (n_slc, n_kb), 1)
    per_kb = _dot(any_q, jnp.where(pi * NSA_SEL_LEN // BLK == pj, 1.0, 0.0).astype(BF16))
    per_kb = jnp.where(per_kb > 0.5, 1.0, 0.0).astype(BF16)
    n_word = need_ref.shape[-1]
    bi = lax.broadcasted_iota(jnp.int32, (n_kb, n_word), 0)
    bw = lax.broadcasted_iota(jnp.int32, (n_kb, n_word), 1)
    weight = jnp.where(bi // 16 == bw, jnp.left_shift(1, bi % 16), 0).astype(F32).astype(BF16)
    words = _dot(per_kb, weight)
    need_ref[0, 0] = words.astype(jnp.int32)


def _nsa_compressed(z, k_cmp, v_cmp, slopes):
    s = z.shape[0]
    n_slc = s // NSA_SEL_LEN
    g_per = H_NSA // KV_NSA
    qw = g_per * HEAD_DIM
    n_cmp = k_cmp.shape[1]
    n_word = 128
    assert (s // BLK) <= 16 * n_word
    return pl.pallas_call(
        functools.partial(_nsa_cmp_kernel, g_per=g_per, n_slc=n_slc),
        grid=(KV_NSA, s // BLK),
        in_specs=[pl.BlockSpec(memory_space=pltpu.SMEM),
                  pl.BlockSpec((BLK, qw), lambda h, i: (i, h)),
                  pl.BlockSpec((1, n_cmp, HEAD_DIM), lambda h, i: (h, 0, 0)),
                  pl.BlockSpec((1, n_cmp, HEAD_DIM), lambda h, i: (h, 0, 0))],
        out_specs=[pl.BlockSpec((BLK, qw), lambda h, i: (i, h)),
                   pl.BlockSpec((1, BLK, n_slc), lambda h, i: (h, i, 0)),
                   pl.BlockSpec((1, 1, 8, n_word), lambda h, i: (h, i, 0, 0))],
        out_shape=[jax.ShapeDtypeStruct((s, H_NSA * HEAD_DIM), F32),
                   jax.ShapeDtypeStruct((KV_NSA, s, n_slc), BF16),
                   jax.ShapeDtypeStruct((KV_NSA, s // BLK, 8, n_word), jnp.int32)],
        scratch_shapes=[pltpu.VMEM((BLK, n_cmp), F32),
                        pltpu.VMEM((g_per * BLK, HEAD_DIM), BF16),
                        pltpu.VMEM((g_per * BLK, HEAD_DIM), F32)],
        compiler_params=_params("parallel", "arbitrary"),
        name="nsa_compressed",
    )(slopes, z, k_cmp, v_cmp)


def _stack_heads(q_ref, qs_ref, g_per):
    for g in range(g_per):
        qs_ref[g * BLK:(g + 1) * BLK, :] = q_ref[:, g * HEAD_DIM:(g + 1) * HEAD_DIM]


_ROW_SPLIT = 2


def _online_visit(i, blocks, qs_ref, k_ref, v_ref, slope_ref, m_ref, l_ref, acc_ref, g_per):
    nb = len(blocks)
    starts = [pl.multiple_of(j * BLK, BLK) for j, _ in blocks]
    k = jnp.concatenate([k_ref[pl.ds(st, BLK), :] for st in starts], axis=0)
    v = jnp.concatenate([v_ref[pl.ds(st, BLK), :] for st in starts], axis=0)
    v_ones = jnp.concatenate([v, jnp.ones((nb * BLK, HEAD_DIM), BF16)], axis=1)
    qq = lax.broadcasted_iota(jnp.int32, (BLK, BLK), 0) - lax.broadcasted_iota(jnp.int32, (BLK, BLK), 1)
    dists = [(i - j) * BLK + qq for j, _ in blocks]
    ok1 = jnp.concatenate([key_ok(d) for (_, key_ok), d in zip(blocks, dists)], axis=1)
    dist1 = jnp.concatenate([d.astype(F32) for d in dists], axis=1)
    g_sub = g_per // _ROW_SPLIT
    ok = jnp.concatenate([ok1] * g_sub, axis=0) > 0.5
    dist_f = jnp.concatenate([dist1] * g_sub, axis=0)

    groups = [slice(r * g_sub * BLK, (r + 1) * g_sub * BLK) for r in range(_ROW_SPLIT)]
    s = [_dot_nt(qs_ref[rows, :], k) for rows in groups]
    s = [jnp.where(ok, x * (SCALE * LOG2E) - jnp.concatenate([slope_ref[0, rows, :]] * nb, axis=1) * dist_f, NEG)
         for x, rows in zip(s, groups)]
    m_prev = [m_ref[rows, :] for rows in groups]
    m_new = [jnp.maximum(mp, jnp.max(x, axis=-1, keepdims=True)) for mp, x in zip(m_prev, s)]
    alpha = [jnp.exp2(mp - mn) for mp, mn in zip(m_prev, m_new)]
    p = [jnp.where(ok, jnp.exp2(x - jnp.concatenate([mn] * nb, axis=1)), 0.0) for x, mn in zip(s, m_new)]
    pv = [_dot(x.astype(BF16), v_ones) for x in p]
    for rows, a, y, mn in zip(groups, alpha, pv, m_new):
        l_ref[rows, :] = a * l_ref[rows, :] + y[:, HEAD_DIM:]
        acc_ref[rows, :] = a * acc_ref[rows, :] + y[:, :HEAD_DIM]
        m_ref[rows, :] = mn


def _online_init(m_ref, l_ref, acc_ref):
    m_ref[...] = jnp.full_like(m_ref, NEG)
    l_ref[...] = jnp.zeros_like(l_ref)
    acc_ref[...] = jnp.zeros_like(acc_ref)


_SEL_GROUP = 4


def _nsa_sel_kernel(need_ref, q_ref, sel_ref, slope_ref, k_ref, v_ref, o_ref,
                    qs_ref, m_ref, l_ref, acc_ref, list_ref, *, g_per, n_qblk, n_word):
    h = pl.program_id(0)
    i = pl.program_id(1)
    _stack_heads(q_ref, qs_ref, g_per)
    _online_init(m_ref, l_ref, acc_ref)
    sel = sel_ref[0]
    n_slc = sel.shape[1]
    per_blk = BLK // NSA_SEL_LEN
    eb = lax.broadcasted_iota(jnp.int32, (n_slc, BLK), 0)
    el = lax.broadcasted_iota(jnp.int32, (n_slc, BLK), 1) // NSA_SEL_LEN
    base = (h * n_qblk + i) * n_word

    def scan(j, n):
        word = need_ref[base + jnp.right_shift(j, 4)]
        needed = jnp.bitwise_and(jnp.right_shift(word, jnp.bitwise_and(j, 15)), 1)

        @pl.when(needed == 1)
        def _():
            list_ref[n] = j
        return n + needed

    n_needed = lax.fori_loop(0, i + 1, scan, 0)

    def visit(first, count):
        blocks = []
        for u in range(count):
            j = list_ref[first + u]
            expand = jnp.where(eb == per_blk * j + el, 1.0, 0.0).astype(BF16)
            picked = _dot(sel, expand)
            blocks.append((j, functools.partial(
                lambda dist, picked: jnp.where(dist >= 0, picked, 0.0), picked=picked)))
        _online_visit(i, blocks, qs_ref, k_ref, v_ref, slope_ref, m_ref, l_ref, acc_ref, g_per)

    def group(t, carry):
        visit(t * _SEL_GROUP, _SEL_GROUP)
        return carry

    n_groups = n_needed // _SEL_GROUP
    lax.fori_loop(0, n_groups, group, 0)
    done = n_groups * _SEL_GROUP
    size = _SEL_GROUP // 2
    while size >= 1:
        has = jnp.bitwise_and(n_needed, size) != 0
        pl.when(has)(functools.partial(visit, done, size))
        done = done + jnp.where(has, size, 0)
        size //= 2
    out = acc_ref[...] / jnp.maximum(l_ref[...], 1e-30)
    for g in range(g_per):
        o_ref[:, g * HEAD_DIM:(g + 1) * HEAD_DIM] = out[g * BLK:(g + 1) * BLK, :]


def _nsa_selected(z, sel, need, slope_rows, *, k_col, v_col):
    s = z.shape[0]
    g_per = H_NSA // KV_NSA
    qw = g_per * HEAD_DIM
    rows = g_per * BLK
    n_slc = sel.shape[-1]
    n_qblk = s // BLK
    n_word = -(-n_qblk // 16)
    need = need[:, :, 0, :n_word]
    return pl.pallas_call(
        functools.partial(_nsa_sel_kernel, g_per=g_per, n_qblk=n_qblk, n_word=n_word),
        grid=(KV_NSA, n_qblk),
        in_specs=[pl.BlockSpec(memory_space=pltpu.SMEM),
                  pl.BlockSpec((BLK, qw), lambda h, i: (i, h)),
                  pl.BlockSpec((1, BLK, n_slc), lambda h, i: (h, i, 0)),
                  pl.BlockSpec((1, rows, HEAD_DIM), lambda h, i: (h, 0, 0)),
                  pl.BlockSpec((s, HEAD_DIM), lambda h, i: (0, k_col + h)),
                  pl.BlockSpec((s, HEAD_DIM), lambda h, i: (0, v_col + h))],
        out_specs=pl.BlockSpec((BLK, qw), lambda h, i: (i, h)),
        out_shape=jax.ShapeDtypeStruct((s, H_NSA * HEAD_DIM), F32),
        scratch_shapes=[pltpu.VMEM((rows, HEAD_DIM), BF16),
                        pltpu.VMEM((rows, HEAD_DIM), F32),
                        pltpu.VMEM((rows, HEAD_DIM), F32),
                        pltpu.VMEM((rows, HEAD_DIM), F32),
                        pltpu.SMEM((n_qblk + _SEL_GROUP,), jnp.int32)],
        compiler_params=_params("parallel", "arbitrary"),
        name="nsa_selected",
    )(need.reshape(-1), z, sel, slope_rows, z, z)


def _nsa_win_kernel(q_ref, slope_ref, k_ref, v_ref, gate_ref, ocmp_ref, oslc_ref, o_ref,
                    qs_ref, m_ref, l_ref, acc_ref, *, g_per):
    i = pl.program_id(1)
    _stack_heads(q_ref, qs_ref, g_per)
    _online_init(m_ref, l_ref, acc_ref)
    n_prev = -(-(NSA_WINDOW - 1) // BLK)

    blocks = []
    for step in range(n_prev + 1):
        j = i - n_prev + step
        present = jnp.where(j >= 0, 1.0, 0.0)
        blocks.append((jnp.maximum(j, 0), functools.partial(
            lambda dist, present: jnp.where((dist >= 0) & (dist < NSA_WINDOW), present, 0.0),
            present=present)))
    _online_visit(i, blocks, qs_ref, k_ref, v_ref, slope_ref, m_ref, l_ref, acc_ref, g_per)
    o_win = acc_ref[...] / jnp.maximum(l_ref[...], 1e-30)
    gates = jax.nn.sigmoid(gate_ref[...])
    for g in range(g_per):
        hs = slice(g * HEAD_DIM, (g + 1) * HEAD_DIM)
        o = (gates[:, 3 * g:3 * g + 1] * ocmp_ref[:, hs]
             + gates[:, 3 * g + 1:3 * g + 2] * oslc_ref[:, hs]
             + gates[:, 3 * g + 2:3 * g + 3] * o_win[g * BLK:(g + 1) * BLK, :])
        o_ref[:, hs] = o.astype(o_ref.dtype)


def _nsa_window_combine(z, gate_logits, o_cmp, o_slc, slope_rows, *, k_col, v_col):
    s = z.shape[0]
    g_per = H_NSA // KV_NSA
    qw = g_per * HEAD_DIM
    rows = g_per * BLK
    head_blk = lambda h, i: (i, h)
    return pl.pallas_call(
        functools.partial(_nsa_win_kernel, g_per=g_per),
        grid=(KV_NSA, s // BLK),
        in_specs=[pl.BlockSpec((BLK, qw), head_blk),
                  pl.BlockSpec((1, rows, HEAD_DIM), lambda h, i: (h, 0, 0)),
                  pl.BlockSpec((s, HEAD_DIM), lambda h, i: (0, k_col + h)),
                  pl.BlockSpec((s, HEAD_DIM), lambda h, i: (0, v_col + h)),
                  pl.BlockSpec((BLK, 128), head_blk),
                  pl.BlockSpec((BLK, qw), head_blk),
                  pl.BlockSpec((BLK, qw), head_blk)],
        out_specs=pl.BlockSpec((BLK, qw), head_blk),
        out_shape=jax.ShapeDtypeStruct((s, H_NSA * HEAD_DIM), BF16),
        scratch_shapes=[pltpu.VMEM((rows, HEAD_DIM), BF16),
                        pltpu.VMEM((rows, HEAD_DIM), F32),
                        pltpu.VMEM((rows, HEAD_DIM), F32),
                        pltpu.VMEM((rows, HEAD_DIM), F32)],
        compiler_params=_params("parallel", "arbitrary"),
        name="nsa_window_combine",
    )(z, slope_rows, z, z, gate_logits, o_cmp, o_slc)


def _ab_mixer(h, w_in, sinks, w_out, x_res):
    z = _matmul(h, w_in.astype(BF16), BF16, tm=1024, tn=768)
    d_sb = H_SB * HEAD_DIM
    o_a = _stick_breaking(z, n_heads=H_SB, q_col=0, k_col=H_SB, v_col=2 * H_SB)
    q_b = 3 * d_sb
    k_b = q_b + H_SW * HEAD_DIM
    v_b = k_b + KV_SW * HEAD_DIM
    o_b = _window_sink_attention(z, sinks, q_col=q_b, k_col=k_b, v_col=v_b)
    o = jnp.concatenate([o_a, o_b], axis=-1)
    return _matmul(o, w_out.astype(BF16), F32, tm=1024, tn=512, residual=x_res)


def _nsa_mixer(h, w_in, cmp_pos, wk1, wk2, wv1, wv2, w_out, x_res):
    s = h.shape[0]
    dq = H_NSA * HEAD_DIM
    dkv = KV_NSA * HEAD_DIM
    g_per = H_NSA // KV_NSA
    main = dq + 6 * dkv
    z = _matmul(h, w_in[:, :main].astype(BF16), BF16, tm=1024, tn=512)
    w_gate = w_in[:, main:].reshape(-1, KV_NSA, 3 * g_per)
    w_gate = jnp.pad(w_gate, ((0, 0), (0, 0), (0, 128 - 3 * g_per))).reshape(-1, KV_NSA * 128)
    gate_logits = _matmul(h, w_gate.astype(BF16), F32, tm=1024, tn=KV_NSA * 128)

    n_chunk = s // NSA_CMP_STRIDE
    to_chunks = lambda c: (z[:, c:c + dkv].reshape(n_chunk, NSA_CMP_STRIDE, KV_NSA, HEAD_DIM)
                           .transpose(2, 0, 1, 3).reshape(KV_NSA, n_chunk, NSA_CMP_STRIDE * HEAD_DIM))
    chunks = jnp.concatenate([to_chunks(dq), to_chunks(dq + dkv)], axis=0)
    kv_cmp = _compress(chunks, cmp_pos.reshape(1, -1).astype(F32),
                       jnp.stack([wk1, wv1]).astype(BF16), jnp.stack([wk2, wv2]).astype(BF16))
    k_cmp, v_cmp = kv_cmp[:KV_NSA], kv_cmp[KV_NSA:]

    slopes = _alibi_slopes(H_NSA, KV_NSA) * np.float32(LOG2E)
    slope_rows = jnp.asarray(np.repeat(np.repeat(slopes, BLK, axis=1)[..., None], HEAD_DIM, axis=2))
    o_cmp, sel, need = _nsa_compressed(z, k_cmp, v_cmp, jnp.asarray(slopes.reshape(-1)))
    col = lambda off: (dq + off * dkv) // HEAD_DIM
    o_slc = _nsa_selected(z, sel, need, slope_rows, k_col=col(2), v_col=col(3))
    o = _nsa_window_combine(z, gate_logits, o_cmp, o_slc, slope_rows, k_col=col(4), v_col=col(5))
    return _matmul(o, w_out.astype(BF16), F32, tm=1024, tn=512, residual=x_res)


def _swiglu(h, w_gate_up, w_down, layer, x_res):
    act = _gate_up(h, w_gate_up, layer, tm=2048, tn=256)
    return _matmul(act, w_down, F32, tm=512, tn=512, residual=x_res, b_index=layer)


def kernel(x, attn_norm, ffn_norm, w_gate_up, w_down, ab_w_in, ab_sinks, ab_w_out, nsa_w_in,
           nsa_cmp_pos, nsa_cmp_wk1, nsa_cmp_wk2, nsa_cmp_wv1, nsa_cmp_wv2, nsa_w_out, final_norm):
    b, s, d = x.shape
    w_gate_up = w_gate_up.astype(BF16)
    w_down = w_down.astype(BF16)
    outs = []
    for bi in range(b):
        xb = x[bi]
        for layer in range(attn_norm.shape[0]):
            h = _rmsnorm(xb, attn_norm[layer], BF16)
            if layer % 2 == 0:
                e = layer // 2
                xb = _ab_mixer(h, ab_w_in[e], ab_sinks[e], ab_w_out[e], xb)
            else:
                o = layer // 2
                xb = _nsa_mixer(h, nsa_w_in[o], nsa_cmp_pos[o], nsa_cmp_wk1[o], nsa_cmp_wk2[o],
                                nsa_cmp_wv1[o], nsa_cmp_wv2[o], nsa_w_out[o], xb)
            h = _rmsnorm(xb, ffn_norm[layer], BF16)
            xb = _swiglu(h, w_gate_up, w_down, layer, xb)
        outs.append(_rmsnorm(xb, final_norm, F32))
    return jnp.stack(outs, axis=0)
```

```python
import functools

import numpy as np
import jax
import jax.numpy as jnp
from jax import lax
from jax.experimental import pallas as pl
from jax.experimental.pallas import tpu as pltpu

F32 = jnp.float32
BF16 = jnp.bfloat16

HEAD_DIM = 128
BLK = 128
EPS = 1e-5
NEG = -1e30
SCALE = HEAD_DIM ** -0.5
LOG2E = 1.4426950408889634

H_SB = 8
H_SW = 24
KV_SW = 3
SW_WINDOW = 128

H_NSA = 32
KV_NSA = 2
NSA_CMP_LEN = 32
NSA_CMP_STRIDE = 16
NSA_SEL_LEN = 64
NSA_TOP_N = 8
NSA_WINDOW = 512
NSA_FORCE = 1e4

VMEM_LIMIT_BYTES = 52 * 2 ** 20
GATE_UP_VMEM_LIMIT_BYTES = 60 * 2 ** 20

_NT = (((1,), (1,)), ((), ()))


def _params(*sem, vmem_limit_bytes=VMEM_LIMIT_BYTES):
    return pltpu.CompilerParams(dimension_semantics=sem, vmem_limit_bytes=vmem_limit_bytes)


def _dot(a, b):
    return jnp.dot(a, b, preferred_element_type=F32)


def _dot_nt(a, b):
    return lax.dot_general(a, b, _NT, preferred_element_type=F32)


def _alibi_slopes(n_heads, n_kv):
    s = np.exp2(np.float32(-8.0) * np.arange(1, n_heads + 1, dtype=np.float32)
                / np.float32(n_heads)).astype(np.float32)
    return s.reshape(n_heads // n_kv, n_kv).T


def _rmsnorm_kernel(x_ref, g_ref, o_ref):
    x = x_ref[...]
    ms = jnp.mean(x * x, axis=-1, keepdims=True)
    o_ref[...] = (x * lax.rsqrt(ms + EPS) * g_ref[...]).astype(o_ref.dtype)


def _rmsnorm(x, g, out_dtype, tm=256):
    m, d = x.shape
    return pl.pallas_call(
        _rmsnorm_kernel,
        grid=(m // tm,),
        in_specs=[pl.BlockSpec((tm, d), lambda i: (i, 0)),
                  pl.BlockSpec((1, d), lambda i: (0, 0))],
        out_specs=pl.BlockSpec((tm, d), lambda i: (i, 0)),
        out_shape=jax.ShapeDtypeStruct((m, d), out_dtype),
        compiler_params=_params("parallel"),
        name="rmsnorm",
    )(x, g.reshape(1, d).astype(F32))


def _mm_kernel(*refs, nk, has_res):
    if has_res:
        a_ref, b_ref, r_ref, o_ref = refs[:4]
        scratch = refs[4:]
    else:
        a_ref, b_ref, o_ref = refs[:3]
        r_ref = None
        scratch = refs[3:]

    def finish(acc):
        if has_res:
            acc = acc + r_ref[...]
        o_ref[...] = acc.astype(o_ref.dtype)

    part = _dot(a_ref[...], b_ref[...])
    if nk == 1:
        finish(part)
    else:
        acc_ref, = scratch
        k = pl.program_id(2)

        @pl.when(k == 0)
        def _():
            acc_ref[...] = part

        @pl.when(k > 0)
        def _():
            acc_ref[...] += part

        @pl.when(k == nk - 1)
        def _():
            finish(acc_ref[...])


def _matmul(a, b, out_dtype, *, tm, tn, tk=None, residual=None, b_index=None):
    m, kd = a.shape
    n = b.shape[-1]
    tk = kd if tk is None else tk
    nk = kd // tk
    assert m % tm == 0 and n % tn == 0 and kd % tk == 0
    if b_index is None:
        b_spec = pl.BlockSpec((tk, tn), lambda i, j, k: (k, j))
    else:
        b_spec = pl.BlockSpec((None, tk, tn), lambda i, j, k: (b_index, k, j))
    in_specs = [pl.BlockSpec((tm, tk), lambda i, j, k: (i, k)), b_spec]
    args = [a, b]
    if residual is not None:
        in_specs.append(pl.BlockSpec((tm, tn), lambda i, j, k: (i, j)))
        args.append(residual)
    return pl.pallas_call(
        functools.partial(_mm_kernel, nk=nk, has_res=residual is not None),
        grid=(m // tm, n // tn, nk),
        in_specs=in_specs,
        out_specs=pl.BlockSpec((tm, tn), lambda i, j, k: (i, j)),
        out_shape=jax.ShapeDtypeStruct((m, n), out_dtype),
        scratch_shapes=[pltpu.VMEM((tm, tn), F32)] if nk > 1 else [],
        compiler_params=_params("parallel", "parallel", "arbitrary"),
        name="matmul",
    )(*args)


def _gate_up_kernel(a_ref, wg_ref, wu_ref, o_ref):
    a = a_ref[...]
    gate = _dot(a, wg_ref[...].astype(BF16))
    up = _dot(a, wu_ref[...].astype(BF16))
    o_ref[...] = (gate * jax.nn.sigmoid(gate) * up).astype(o_ref.dtype)


def _gate_up(a, w_gate_up, layer, *, tm, tn):
    m, kd = a.shape
    f = w_gate_up.shape[-1] // 2
    nf = f // tn
    assert m % tm == 0 and f % tn == 0
    return pl.pallas_call(
        _gate_up_kernel,
        grid=(m // tm, nf),
        in_specs=[pl.BlockSpec((tm, kd), lambda i, j: (i, 0)),
                  pl.BlockSpec((None, kd, tn), lambda i, j: (layer, 0, j)),
                  pl.BlockSpec((None, kd, tn), lambda i, j: (layer, 0, j + nf))],
        out_specs=pl.BlockSpec((tm, tn), lambda i, j: (i, j)),
        out_shape=jax.ShapeDtypeStruct((m, f), BF16),
        compiler_params=_params("parallel", "parallel", vmem_limit_bytes=GATE_UP_VMEM_LIMIT_BYTES),
        name="gate_up",
    )(a, w_gate_up, w_gate_up)


_SB_DEAD = -200.0


def _sb_kernel(q_ref, k_ref, v_ref, o_ref, acc_ref, later_ref, z_ref, w_ref, *, tq, tk):
    qi = pl.program_id(1)
    ratio = tq // tk
    assert ratio % 2 == 0
    q = q_ref[...]
    row = lax.broadcasted_iota(jnp.int32, (tk, tk), 0)
    col = lax.broadcasted_iota(jnp.int32, (tk, tk), 1)
    suffix = jnp.where(row >= col, 1.0, 0.0).astype(BF16)
    acc_ref[...] = jnp.zeros_like(acc_ref)
    later_ref[...] = jnp.zeros_like(later_ref)

    def logits(j, r0=0):
        start = pl.multiple_of(j * tk, tk)
        return _dot_nt(q[r0:], k_ref[pl.ds(start, tk), :]) * (-SCALE * LOG2E)

    def weights(j, nz, masked, r0=0):
        neg_abs = lax.bitcast_convert_type(
            lax.bitcast_convert_type(nz, jnp.uint32) | jnp.uint32(0x80000000), F32)
        a = jnp.minimum(nz, 0.0) - jnp.log2(1.0 + jnp.exp2(neg_abs))
        if masked:
            t_pos = qi * tq + r0 + lax.broadcasted_iota(jnp.int32, nz.shape, 0)
            s_pos = j * tk + lax.broadcasted_iota(jnp.int32, nz.shape, 1)
            before = s_pos < t_pos
            a = jnp.where(before, a, 0.0)
        within = _dot(a.astype(BF16), suffix)
        later = later_ref[r0:, :]
        w = jnp.exp2(within + jnp.concatenate([later] * (tk // HEAD_DIM), axis=1) - nz)
        later_ref[r0:, :] = later + within[:, 0:1]
        if masked:
            w = jnp.where(before, w, 0.0)
        return w.astype(BF16)

    def accumulate(j, w, r0=0):
        start = pl.multiple_of(j * tk, tk)
        acc_ref[r0:, :] += _dot(w, v_ref[pl.ds(start, tk), :])

    n_full = ratio * qi

    for b in reversed(range(ratio)):
        j = n_full + b
        accumulate(j, weights(j, logits(j, b * tk), True, b * tk), b * tk)

    def step(j, slot):
        accumulate(j + 1, w_ref[1 - slot])
        z_ref[1 - slot] = logits(jnp.maximum(j - 1, 0))
        w_ref[slot] = weights(j, z_ref[slot], False)

    def remaining_mass():
        return jnp.max(later_ref[...])

    def two_steps(carry):
        s, _ = carry
        j = n_full - 1 - 2 * s
        step(j, 0)
        step(j - 1, 1)
        return s + 1, remaining_mass()

    def unfinished(carry):
        s, mass = carry
        return jnp.logical_and(s < n_full // 2, mass > _SB_DEAD)

    z_ref[0] = logits(jnp.maximum(n_full - 1, 0))
    w_ref[1] = jnp.zeros((tq, tk), BF16)
    trips, _ = lax.while_loop(unfinished, two_steps, (jnp.int32(0), remaining_mass()))
    accumulate(n_full - 2 * trips, w_ref[1])
    o_ref[...] = acc_ref[...].astype(o_ref.dtype)


def _stick_breaking(z, *, n_heads, q_col, k_col, v_col, tq=1024, tk=256):
    s = z.shape[0]
    return pl.pallas_call(
        functools.partial(_sb_kernel, tq=tq, tk=tk),
        grid=(n_heads, s // tq),
        in_specs=[pl.BlockSpec((tq, HEAD_DIM), lambda h, i: (i, q_col + h)),
                  pl.BlockSpec((s, HEAD_DIM), lambda h, i: (0, k_col + h)),
                  pl.BlockSpec((s, HEAD_DIM), lambda h, i: (0, v_col + h))],
        out_specs=pl.BlockSpec((tq, HEAD_DIM), lambda h, i: (i, h)),
        out_shape=jax.ShapeDtypeStruct((s, n_heads * HEAD_DIM), BF16),
        scratch_shapes=[pltpu.VMEM((tq, HEAD_DIM), F32),
                        pltpu.VMEM((tq, HEAD_DIM), F32),
                        pltpu.VMEM((2, tq, tk), F32),
                        pltpu.VMEM((2, tq, tk), BF16)],
        compiler_params=_params("parallel", "arbitrary"),
        name="stick_breaking",
    )(z, z, z)


def _swa_kernel(sink_ref, q_ref, kp_ref, kc_ref, vp_ref, vc_ref, o_ref, *, slopes):
    i = pl.program_id(0)
    n_kv, g_per = slopes.shape
    qq = lax.broadcasted_iota(jnp.int32, (BLK, 2 * BLK), 0)
    kk = lax.broadcasted_iota(jnp.int32, (BLK, 2 * BLK), 1)
    dist = qq + BLK - kk
    first_key = jnp.where(i > 0, 0, BLK)
    mask = (dist >= 0) & (dist < SW_WINDOW) & (kk >= first_key)
    dist_f = dist.astype(F32)
    for h in range(n_kv):
        cs = slice(h * HEAD_DIM, (h + 1) * HEAD_DIM)
        k = jnp.concatenate([kp_ref[:, cs], kc_ref[:, cs]], axis=0)
        v = jnp.concatenate([vp_ref[:, cs], vc_ref[:, cs]], axis=0)
        for g in range(g_per):
            head = h * g_per + g
            hs = slice(head * HEAD_DIM, (head + 1) * HEAD_DIM)
            s = _dot_nt(q_ref[:, hs], k) * SCALE - float(slopes[h, g]) * dist_f
            s = jnp.where(mask, s, NEG)
            sink = sink_ref[head]
            m = jnp.maximum(jnp.max(s, axis=-1, keepdims=True), sink)
            e = jnp.where(mask, jnp.exp(s - m), 0.0)
            denom = jnp.sum(e, axis=-1, keepdims=True) + jnp.exp(sink - m)
            p = e / denom
            o_ref[:, hs] = _dot(p.astype(BF16), v).astype(o_ref.dtype)


def _window_sink_attention(z, sinks, *, q_col, k_col, v_col):
    s = z.shape[0]
    qw = H_SW * HEAD_DIM
    kw = KV_SW * HEAD_DIM
    assert q_col % qw == 0 and k_col % kw == 0 and v_col % kw == 0
    qb, kb, vb = q_col // qw, k_col // kw, v_col // kw
    prev = lambda i: jnp.maximum(i - 1, 0)
    return pl.pallas_call(
        functools.partial(_swa_kernel, slopes=_alibi_slopes(H_SW, KV_SW)),
        grid=(s // BLK,),
        in_specs=[pl.BlockSpec(memory_space=pltpu.SMEM),
                  pl.BlockSpec((BLK, qw), lambda i: (i, qb)),
                  pl.BlockSpec((BLK, kw), lambda i: (prev(i), kb)),
                  pl.BlockSpec((BLK, kw), lambda i: (i, kb)),
                  pl.BlockSpec((BLK, kw), lambda i: (prev(i), vb)),
                  pl.BlockSpec((BLK, kw), lambda i: (i, vb))],
        out_specs=pl.BlockSpec((BLK, qw), lambda i: (i, 0)),
        out_shape=jax.ShapeDtypeStruct((s, qw), BF16),
        compiler_params=_params("parallel"),
        name="window_sink_attention",
    )(sinks.astype(F32), z, z, z, z, z)


def _compress_kernel(x_ref, pos_ref, w1_ref, w2_ref, o_ref):
    x = x_ref[0].astype(F32)
    half = x.shape[1]
    n_chunk = x.shape[0]
    lo = _dot((x + pos_ref[:, :half]).astype(BF16), w1_ref[0, :half, :])
    hi = _dot((x + pos_ref[:, half:]).astype(BF16), w1_ref[0, half:, :])
    pre = lo + pltpu.roll(hi, n_chunk - 1, 0)
    act = jax.nn.gelu(pre, approximate=True)
    out = _dot(act.astype(BF16), w2_ref[0])
    last = lax.broadcasted_iota(jnp.int32, out.shape, 0) == n_chunk - 1
    o_ref[0] = jnp.where(last, 0.0, out).astype(o_ref.dtype)


def _compress(x, pos, w1, w2):
    n, n_chunk, width = x.shape
    hid = w1.shape[-1]
    return pl.pallas_call(
        _compress_kernel,
        grid=(n,),
        in_specs=[pl.BlockSpec((1, n_chunk, width), lambda i: (i, 0, 0)),
                  pl.BlockSpec((1, 2 * width), lambda i: (0, 0)),
                  pl.BlockSpec((1, 2 * width, hid), lambda i: (i // KV_NSA, 0, 0)),
                  pl.BlockSpec((1, hid, HEAD_DIM), lambda i: (i // KV_NSA, 0, 0))],
        out_specs=pl.BlockSpec((1, n_chunk, HEAD_DIM), lambda i: (i, 0, 0)),
        out_shape=jax.ShapeDtypeStruct((n, n_chunk, HEAD_DIM), BF16),
        compiler_params=_params("parallel"),
        name="nsa_compress",
    )(x, pos, w1, w2)


_CMP_CHUNK = 256
_CMP_UNROLL = 4


def _split3(x):
    hi = x.astype(BF16)
    r1 = x - hi.astype(F32)
    mid = r1.astype(BF16)
    lo = (r1 - mid.astype(F32)).astype(BF16)
    return hi, mid, lo


def _nsa_cmp_kernel(slope_ref, q_ref, kc_ref, vc_ref, o_ref, sel_ref, need_ref,
                    imp_ref, qs_ref, os_ref, *, g_per, n_slc):
    h = pl.program_id(0)
    i = pl.program_id(1)
    n_cmp = kc_ref.shape[1]
    _stack_heads(q_ref, qs_ref, g_per)
    imp_ref[...] = jnp.zeros_like(imp_ref)

    def attend(width):
        kc = kc_ref[0, :width, :]
        vc = vc_ref[0, :width, :]
        t = i * BLK + lax.broadcasted_iota(jnp.int32, (BLK, width), 0)
        c_end = (lax.broadcasted_iota(jnp.int32, (BLK, width), 1) * NSA_CMP_STRIDE
                 + (NSA_CMP_LEN - 1))
        dist = t - c_end
        mask = dist >= 0
        dist_f = dist.astype(F32)

        def group(gg, carry):
            heads = [gg * _CMP_UNROLL + u for u in range(_CMP_UNROLL)]
            rows = [pl.ds(pl.multiple_of(g * BLK, BLK), BLK) for g in heads]
            s = [_dot_nt(qs_ref[r, :], kc) for r in rows]
            s = [jnp.where(mask, x * (SCALE * LOG2E) - slope_ref[h * g_per + g] * dist_f, NEG)
                 for x, g in zip(s, heads)]
            m = [jnp.max(x, axis=-1, keepdims=True) for x in s]
            e = [jnp.where(mask, jnp.exp2(x - mx), 0.0) for x, mx in zip(s, m)]
            p = [x / jnp.maximum(jnp.sum(x, axis=-1, keepdims=True), 1e-30) for x in e]
            for r, x in zip(rows, p):
                os_ref[r, :] = _dot(x.astype(BF16), vc)
            total = p[0]
            for x in p[1:]:
                total = total + x
            imp_ref[:, :width] += total
            return carry

        lax.fori_loop(0, g_per // _CMP_UNROLL, group, 0)

    n_visible = (BLK * i + BLK - NSA_CMP_LEN) // NSA_CMP_STRIDE + 1
    widths = sorted({min(w, n_cmp) for w in range(_CMP_CHUNK, n_cmp + _CMP_CHUNK, _CMP_CHUNK)})
    case = jnp.minimum((n_visible - 1) // _CMP_CHUNK, len(widths) - 1)
    for idx, width in enumerate(widths):
        pl.when(case == idx)(functools.partial(attend, width))
    for g in range(g_per):
        o_ref[:, g * HEAD_DIM:(g + 1) * HEAD_DIM] = os_ref[g * BLK:(g + 1) * BLK, :]

    r = NSA_SEL_LEN // NSA_CMP_STRIDE
    ci = lax.broadcasted_iota(jnp.int32, (n_cmp, n_slc), 0)
    cj = lax.broadcasted_iota(jnp.int32, (n_cmp, n_slc), 1)
    pool = jnp.where((ci >= r * cj - 1) & (ci <= r * cj + r - 1), 1.0, 0.0).astype(BF16)
    hi, mid, lo = _split3(imp_ref[...])
    imp = _dot(hi, pool) + (_dot(mid, pool) + _dot(lo, pool))

    blk = lax.broadcasted_iota(jnp.int32, (BLK, n_slc), 1)
    cur = (i * BLK + lax.broadcasted_iota(jnp.int32, (BLK, n_slc), 0)) // NSA_SEL_LEN
    valid = blk <= cur
    forced = (blk == 0) | (blk == cur) | (blk == cur - 1)
    score = jnp.where(valid, imp + jnp.where(forced, NSA_FORCE, 0.0), NEG)
    blk_f = blk.astype(F32)
    chosen = jnp.zeros((BLK, n_slc), F32)
    for _ in range(min(NSA_TOP_N, n_slc)):
        best = jnp.max(score, axis=-1, keepdims=True)
        first = jnp.min(jnp.where(score == best, blk_f, float(n_slc)), axis=-1, keepdims=True)
        pick = blk_f == first
        chosen = jnp.where(pick, 1.0, chosen)
        score = jnp.where(pick, -jnp.inf, score)
    chosen = jnp.where(valid, chosen, 0.0)
    sel_ref[0] = chosen.astype(sel_ref.dtype)

    n_kb = n_slc * NSA_SEL_LEN // BLK
    any_q = jnp.max(chosen, axis=0, keepdims=True)
    any_q = jnp.broadcast_to(any_q, (8, n_slc)).astype(BF16)
    pi = lax.broadcasted_iota(jnp.int32, (n_slc, n_kb), 0)
    pj = lax.broadcasted_iota(jnp.int32, (n_slc, n_kb), 1)
    per_kb = _dot(any_q, jnp.where(pi * NSA_SEL_LEN // BLK == pj, 1.0, 0.0).astype(BF16))
    per_kb = jnp.where(per_kb > 0.5, 1.0, 0.0).astype(BF16)
    n_word = need_ref.shape[-1]
    bi = lax.broadcasted_iota(jnp.int32, (n_kb, n_word), 0)
    bw = lax.broadcasted_iota(jnp.int32, (n_kb, n_word), 1)
    weight = jnp.where(bi // 16 == bw, jnp.left_shift(1, bi % 16), 0).astype(F32).astype(BF16)
    words = _dot(per_kb, weight)
    need_ref[0, 0] = words.astype(jnp.int32)


def _nsa_compressed(z, k_cmp, v_cmp, slopes):
    s = z.shape[0]
    n_slc = s // NSA_SEL_LEN
    g_per = H_NSA // KV_NSA
    qw = g_per * HEAD_DIM
    n_cmp = k_cmp.shape[1]
    n_word = 128
    assert (s // BLK) <= 16 * n_word
    return pl.pallas_call(
        functools.partial(_nsa_cmp_kernel, g_per=g_per, n_slc=n_slc),
        grid=(KV_NSA, s // BLK),
        in_specs=[pl.BlockSpec(memory_space=pltpu.SMEM),
                  pl.BlockSpec((BLK, qw), lambda h, i: (i, h)),
                  pl.BlockSpec((1, n_cmp, HEAD_DIM), lambda h, i: (h, 0, 0)),
                  pl.BlockSpec((1, n_cmp, HEAD_DIM), lambda h, i: (h, 0, 0))],
        out_specs=[pl.BlockSpec((BLK, qw), lambda h, i: (i, h)),
                   pl.BlockSpec((1, BLK, n_slc), lambda h, i: (h, i, 0)),
                   pl.BlockSpec((1, 1, 8, n_word), lambda h, i: (h, i, 0, 0))],
        out_shape=[jax.ShapeDtypeStruct((s, H_NSA * HEAD_DIM), F32),
                   jax.ShapeDtypeStruct((KV_NSA, s, n_slc), BF16),
                   jax.ShapeDtypeStruct((KV_NSA, s // BLK, 8, n_word), jnp.int32)],
        scratch_shapes=[pltpu.VMEM((BLK, n_cmp), F32),
                        pltpu.VMEM((g_per * BLK, HEAD_DIM), BF16),
                        pltpu.VMEM((g_per * BLK, HEAD_DIM), F32)],
        compiler_params=_params("parallel", "arbitrary"),
        name="nsa_compressed",
    )(slopes, z, k_cmp, v_cmp)


def _stack_heads(q_ref, qs_ref, g_per):
    for g in range(g_per):
        qs_ref[g * BLK:(g + 1) * BLK, :] = q_ref[:, g * HEAD_DIM:(g + 1) * HEAD_DIM]


_ROW_SPLIT = 2


def _online_visit(i, blocks, qs_ref, k_ref, v_ref, slope_ref, m_ref, l_ref, acc_ref, g_per):
    nb = len(blocks)
    starts = [pl.multiple_of(j * BLK, BLK) for j, _ in blocks]
    k = jnp.concatenate([k_ref[pl.ds(st, BLK), :] for st in starts], axis=0)
    v = jnp.concatenate([v_ref[pl.ds(st, BLK), :] for st in starts], axis=0)
    v_ones = jnp.concatenate([v, jnp.ones((nb * BLK, HEAD_DIM), BF16)], axis=1)
    qq = lax.broadcasted_iota(jnp.int32, (BLK, BLK), 0) - lax.broadcasted_iota(jnp.int32, (BLK, BLK), 1)
    dists = [(i - j) * BLK + qq for j, _ in blocks]
    ok1 = jnp.concatenate([key_ok(d) for (_, key_ok), d in zip(blocks, dists)], axis=1)
    dist1 = jnp.concatenate([d.astype(F32) for d in dists], axis=1)
    g_sub = g_per // _ROW_SPLIT
    ok = jnp.concatenate([ok1] * g_sub, axis=0) > 0.5
    dist_f = jnp.concatenate([dist1] * g_sub, axis=0)

    groups = [slice(r * g_sub * BLK, (r + 1) * g_sub * BLK) for r in range(_ROW_SPLIT)]
    s = [_dot_nt(qs_ref[rows, :], k) for rows in groups]
    s = [jnp.where(ok, x * (SCALE * LOG2E) - jnp.concatenate([slope_ref[0, rows, :]] * nb, axis=1) * dist_f, NEG)
         for x, rows in zip(s, groups)]
    m_prev = [m_ref[rows, :] for rows in groups]
    m_new = [jnp.maximum(mp, jnp.max(x, axis=-1, keepdims=True)) for mp, x in zip(m_prev, s)]
    alpha = [jnp.exp2(mp - mn) for mp, mn in zip(m_prev, m_new)]
    p = [jnp.where(ok, jnp.exp2(x - jnp.concatenate([mn] * nb, axis=1)), 0.0) for x, mn in zip(s, m_new)]
    pv = [_dot(x.astype(BF16), v_ones) for x in p]
    for rows, a, y, mn in zip(groups, alpha, pv, m_new):
        l_ref[rows, :] = a * l_ref[rows, :] + y[:, HEAD_DIM:]
        acc_ref[rows, :] = a * acc_ref[rows, :] + y[:, :HEAD_DIM]
        m_ref[rows, :] = mn


def _online_init(m_ref, l_ref, acc_ref):
    m_ref[...] = jnp.full_like(m_ref, NEG)
    l_ref[...] = jnp.zeros_like(l_ref)
    acc_ref[...] = jnp.zeros_like(acc_ref)


_SEL_GROUP = 8


def _nsa_sel_kernel(need_ref, q_ref, sel_ref, slope_ref, k_ref, v_ref, o_ref,
                    qs_ref, m_ref, l_ref, acc_ref, list_ref, *, g_per, n_qblk, n_word):
    h = pl.program_id(0)
    i = pl.program_id(1)
    _stack_heads(q_ref, qs_ref, g_per)
    _online_init(m_ref, l_ref, acc_ref)
    sel = sel_ref[0]
    n_slc = sel.shape[1]
    per_blk = BLK // NSA_SEL_LEN
    eb = lax.broadcasted_iota(jnp.int32, (n_slc, BLK), 0)
    el = lax.broadcasted_iota(jnp.int32, (n_slc, BLK), 1) // NSA_SEL_LEN
    base = (h * n_qblk + i) * n_word

    def scan(j, n):
        word = need_ref[base + jnp.right_shift(j, 4)]
        needed = jnp.bitwise_and(jnp.right_shift(word, jnp.bitwise_and(j, 15)), 1)

        @pl.when(needed == 1)
        def _():
            list_ref[n] = j
        return n + needed

    n_needed = lax.fori_loop(0, i + 1, scan, 0)

    def visit(first, count):
        blocks = []
        for u in range(count):
            j = list_ref[first + u]
            expand = jnp.where(eb == per_blk * j + el, 1.0, 0.0).astype(BF16)
            picked = _dot(sel, expand)
            blocks.append((j, functools.partial(
                lambda dist, picked: jnp.where(dist >= 0, picked, 0.0), picked=picked)))
        _online_visit(i, blocks, qs_ref, k_ref, v_ref, slope_ref, m_ref, l_ref, acc_ref, g_per)

    def group(t, carry):
        visit(t * _SEL_GROUP, _SEL_GROUP)
        return carry

    n_groups = n_needed // _SEL_GROUP
    lax.fori_loop(0, n_groups, group, 0)
    done = n_groups * _SEL_GROUP
    size = _SEL_GROUP // 2
    while size >= 1:
        has = jnp.bitwise_and(n_needed, size) != 0
        pl.when(has)(functools.partial(visit, done, size))
        done = done + jnp.where(has, size, 0)
        size //= 2
    out = acc_ref[...] / jnp.maximum(l_ref[...], 1e-30)
    for g in range(g_per):
        o_ref[:, g * HEAD_DIM:(g + 1) * HEAD_DIM] = out[g * BLK:(g + 1) * BLK, :]


def _nsa_selected(z, sel, need, slope_rows, *, k_col, v_col):
    s = z.shape[0]
    g_per = H_NSA // KV_NSA
    qw = g_per * HEAD_DIM
    rows = g_per * BLK
    n_slc = sel.shape[-1]
    n_qblk = s // BLK
    n_word = -(-n_qblk // 16)
    need = need[:, :, 0, :n_word]
    return pl.pallas_call(
        functools.partial(_nsa_sel_kernel, g_per=g_per, n_qblk=n_qblk, n_word=n_word),
        grid=(KV_NSA, n_qblk),
        in_specs=[pl.BlockSpec(memory_space=pltpu.SMEM),
                  pl.BlockSpec((BLK, qw), lambda h, i: (i, h)),
                  pl.BlockSpec((1, BLK, n_slc), lambda h, i: (h, i, 0)),
                  pl.BlockSpec((1, rows, HEAD_DIM), lambda h, i: (h, 0, 0)),
                  pl.BlockSpec((s, HEAD_DIM), lambda h, i: (0, k_col + h)),
                  pl.BlockSpec((s, HEAD_DIM), lambda h, i: (0, v_col + h))],
        out_specs=pl.BlockSpec((BLK, qw), lambda h, i: (i, h)),
        out_shape=jax.ShapeDtypeStruct((s, H_NSA * HEAD_DIM), F32),
        scratch_shapes=[pltpu.VMEM((rows, HEAD_DIM), BF16),
                        pltpu.VMEM((rows, HEAD_DIM), F32),
                        pltpu.VMEM((rows, HEAD_DIM), F32),
                        pltpu.VMEM((rows, HEAD_DIM), F32),
                        pltpu.SMEM((n_qblk + _SEL_GROUP,), jnp.int32)],
        compiler_params=_params("parallel", "arbitrary"),
        name="nsa_selected",
    )(need.reshape(-1), z, sel, slope_rows, z, z)


def _nsa_win_kernel(q_ref, slope_ref, k_ref, v_ref, gate_ref, ocmp_ref, oslc_ref, o_ref,
                    qs_ref, m_ref, l_ref, acc_ref, *, g_per):
    i = pl.program_id(1)
    _stack_heads(q_ref, qs_ref, g_per)
    _online_init(m_ref, l_ref, acc_ref)
    n_prev = -(-(NSA_WINDOW - 1) // BLK)

    blocks = []
    for step in range(n_prev + 1):
        j = i - n_prev + step
        present = jnp.where(j >= 0, 1.0, 0.0)
        blocks.append((jnp.maximum(j, 0), functools.partial(
            lambda dist, present: jnp.where((dist >= 0) & (dist < NSA_WINDOW), present, 0.0),
            present=present)))
    _online_visit(i, blocks, qs_ref, k_ref, v_ref, slope_ref, m_ref, l_ref, acc_ref, g_per)
    o_win = acc_ref[...] / jnp.maximum(l_ref[...], 1e-30)
    gates = jax.nn.sigmoid(gate_ref[...])
    for g in range(g_per):
        hs = slice(g * HEAD_DIM, (g + 1) * HEAD_DIM)
        o = (gates[:, 3 * g:3 * g + 1] * ocmp_ref[:, hs]
             + gates[:, 3 * g + 1:3 * g + 2] * oslc_ref[:, hs]
             + gates[:, 3 * g + 2:3 * g + 3] * o_win[g * BLK:(g + 1) * BLK, :])
        o_ref[:, hs] = o.astype(o_ref.dtype)


def _nsa_window_combine(z, gate_logits, o_cmp, o_slc, slope_rows, *, k_col, v_col):
    s = z.shape[0]
    g_per = H_NSA // KV_NSA
    qw = g_per * HEAD_DIM
    rows = g_per * BLK
    head_blk = lambda h, i: (i, h)
    return pl.pallas_call(
        functools.partial(_nsa_win_kernel, g_per=g_per),
        grid=(KV_NSA, s // BLK),
        in_specs=[pl.BlockSpec((BLK, qw), head_blk),
                  pl.BlockSpec((1, rows, HEAD_DIM), lambda h, i: (h, 0, 0)),
                  pl.BlockSpec((s, HEAD_DIM), lambda h, i: (0, k_col + h)),
                  pl.BlockSpec((s, HEAD_DIM), lambda h, i: (0, v_col + h)),
                  pl.BlockSpec((BLK, 128), head_blk),
                  pl.BlockSpec((BLK, qw), head_blk),
                  pl.BlockSpec((BLK, qw), head_blk)],
        out_specs=pl.BlockSpec((BLK, qw), head_blk),
        out_shape=jax.ShapeDtypeStruct((s, H_NSA * HEAD_DIM), BF16),
        scratch_shapes=[pltpu.VMEM((rows, HEAD_DIM), BF16),
                        pltpu.VMEM((rows, HEAD_DIM), F32),
                        pltpu.VMEM((rows, HEAD_DIM), F32),
                        pltpu.VMEM((rows, HEAD_DIM), F32)],
        compiler_params=_params("parallel", "arbitrary"),
        name="nsa_window_combine",
    )(z, slope_rows, z, z, gate_logits, o_cmp, o_slc)


def _ab_mixer(h, w_in, sinks, w_out, x_res):
    z = _matmul(h, w_in.astype(BF16), BF16, tm=1024, tn=768)
    d_sb = H_SB * HEAD_DIM
    o_a = _stick_breaking(z, n_heads=H_SB, q_col=0, k_col=H_SB, v_col=2 * H_SB)
    q_b = 3 * d_sb
    k_b = q_b + H_SW * HEAD_DIM
    v_b = k_b + KV_SW * HEAD_DIM
    o_b = _window_sink_attention(z, sinks, q_col=q_b, k_col=k_b, v_col=v_b)
    o = jnp.concatenate([o_a, o_b], axis=-1)
    return _matmul(o, w_out.astype(BF16), F32, tm=1024, tn=512, residual=x_res)


def _nsa_mixer(h, w_in, cmp_pos, wk1, wk2, wv1, wv2, w_out, x_res):
    s = h.shape[0]
    dq = H_NSA * HEAD_DIM
    dkv = KV_NSA * HEAD_DIM
    g_per = H_NSA // KV_NSA
    main = dq + 6 * dkv
    z = _matmul(h, w_in[:, :main].astype(BF16), BF16, tm=1024, tn=512)
    w_gate = w_in[:, main:].reshape(-1, KV_NSA, 3 * g_per)
    w_gate = jnp.pad(w_gate, ((0, 0), (0, 0), (0, 128 - 3 * g_per))).reshape(-1, KV_NSA * 128)
    gate_logits = _matmul(h, w_gate.astype(BF16), F32, tm=1024, tn=KV_NSA * 128)

    n_chunk = s // NSA_CMP_STRIDE
    to_chunks = lambda c: (z[:, c:c + dkv].reshape(n_chunk, NSA_CMP_STRIDE, KV_NSA, HEAD_DIM)
                           .transpose(2, 0, 1, 3).reshape(KV_NSA, n_chunk, NSA_CMP_STRIDE * HEAD_DIM))
    chunks = jnp.concatenate([to_chunks(dq), to_chunks(dq + dkv)], axis=0)
    kv_cmp = _compress(chunks, cmp_pos.reshape(1, -1).astype(F32),
                       jnp.stack([wk1, wv1]).astype(BF16), jnp.stack([wk2, wv2]).astype(BF16))
    k_cmp, v_cmp = kv_cmp[:KV_NSA], kv_cmp[KV_NSA:]

    slopes = _alibi_slopes(H_NSA, KV_NSA) * np.float32(LOG2E)
    slope_rows = jnp.asarray(np.repeat(np.repeat(slopes, BLK, axis=1)[..., None], HEAD_DIM, axis=2))
    o_cmp, sel, need = _nsa_compressed(z, k_cmp, v_cmp, jnp.asarray(slopes.reshape(-1)))
    col = lambda off: (dq + off * dkv) // HEAD_DIM
    o_slc = _nsa_selected(z, sel, need, slope_rows, k_col=col(2), v_col=col(3))
    o = _nsa_window_combine(z, gate_logits, o_cmp, o_slc, slope_rows, k_col=col(4), v_col=col(5))
    return _matmul(o, w_out.astype(BF16), F32, tm=1024, tn=512, residual=x_res)


def _swiglu(h, w_gate_up, w_down, layer, x_res):
    act = _gate_up(h, w_gate_up, layer, tm=2048, tn=256)
    return _matmul(act, w_down, F32, tm=512, tn=512, residual=x_res, b_index=layer)


def kernel(x, attn_norm, ffn_norm, w_gate_up, w_down, ab_w_in, ab_sinks, ab_w_out, nsa_w_in,
           nsa_cmp_pos, nsa_cmp_wk1, nsa_cmp_wk2, nsa_cmp_wv1, nsa_cmp_wv2, nsa_w_out, final_norm):
    b, s, d = x.shape
    w_down = w_down.astype(BF16)
    outs = []
    for bi in range(b):
        xb = x[bi]
        for layer in range(attn_norm.shape[0]):
            h = _rmsnorm(xb, attn_norm[layer], BF16)
            if layer % 2 == 0:
                e = layer // 2
                xb = _ab_mixer(h, ab_w_in[e], ab_sinks[e], ab_w_out[e], xb)
            else:
                o = layer // 2
                xb = _nsa_mixer(h, nsa_w_in[o], nsa_cmp_pos[o], nsa_cmp_wk1[o], nsa_cmp_wk2[o],
                                nsa_cmp_wv1[o], nsa_cmp_wv2[o], nsa_w_out[o], xb)
            h = _rmsnorm(xb, ffn_norm[layer], BF16)
            xb = _swiglu(h, w_gate_up, w_down, layer, xb)
        outs.append(_rmsnorm(xb, final_norm, F32))
    return jnp.stack(outs, axis=0)
```

```python
import functools

import numpy as np
import jax
import jax.numpy as jnp
from jax import lax
from jax.experimental import pallas as pl
from jax.experimental.pallas import tpu as pltpu

F32 = jnp.float32
BF16 = jnp.bfloat16

HEAD_DIM = 128
BLK = 128
EPS = 1e-5
NEG = -1e30
SCALE = HEAD_DIM ** -0.5
LOG2E = 1.4426950408889634

H_SB = 8
H_SW = 24
KV_SW = 3
SW_WINDOW = 128

H_NSA = 32
KV_NSA = 2
NSA_CMP_LEN = 32
NSA_CMP_STRIDE = 16
NSA_SEL_LEN = 64
NSA_TOP_N = 8
NSA_WINDOW = 512
NSA_FORCE = 1e4

VMEM_LIMIT_BYTES = 52 * 2 ** 20
GATE_UP_VMEM_LIMIT_BYTES = 60 * 2 ** 20

_NT = (((1,), (1,)), ((), ()))


def _params(*sem, vmem_limit_bytes=VMEM_LIMIT_BYTES):
    return pltpu.CompilerParams(dimension_semantics=sem, vmem_limit_bytes=vmem_limit_bytes)


def _dot(a, b):
    return jnp.dot(a, b, preferred_element_type=F32)


def _dot_nt(a, b):
    return lax.dot_general(a, b, _NT, preferred_element_type=F32)


def _alibi_slopes(n_heads, n_kv):
    s = np.exp2(np.float32(-8.0) * np.arange(1, n_heads + 1, dtype=np.float32)
                / np.float32(n_heads)).astype(np.float32)
    return s.reshape(n_heads // n_kv, n_kv).T


def _rmsnorm_kernel(x_ref, g_ref, o_ref):
    x = x_ref[...]
    ms = jnp.mean(x * x, axis=-1, keepdims=True)
    o_ref[...] = (x * lax.rsqrt(ms + EPS) * g_ref[...]).astype(o_ref.dtype)


def _rmsnorm(x, g, out_dtype, tm=256):
    m, d = x.shape
    return pl.pallas_call(
        _rmsnorm_kernel,
        grid=(m // tm,),
        in_specs=[pl.BlockSpec((tm, d), lambda i: (i, 0)),
                  pl.BlockSpec((1, d), lambda i: (0, 0))],
        out_specs=pl.BlockSpec((tm, d), lambda i: (i, 0)),
        out_shape=jax.ShapeDtypeStruct((m, d), out_dtype),
        compiler_params=_params("parallel"),
        name="rmsnorm",
    )(x, g.reshape(1, d).astype(F32))


def _mm_kernel(*refs, nk, has_res):
    if has_res:
        a_ref, b_ref, r_ref, o_ref = refs[:4]
        scratch = refs[4:]
    else:
        a_ref, b_ref, o_ref = refs[:3]
        r_ref = None
        scratch = refs[3:]

    def finish(acc):
        if has_res:
            acc = acc + r_ref[...]
        o_ref[...] = acc.astype(o_ref.dtype)

    part = _dot(a_ref[...], b_ref[...])
    if nk == 1:
        finish(part)
    else:
        acc_ref, = scratch
        k = pl.program_id(2)

        @pl.when(k == 0)
        def _():
            acc_ref[...] = part

        @pl.when(k > 0)
        def _():
            acc_ref[...] += part

        @pl.when(k == nk - 1)
        def _():
            finish(acc_ref[...])


def _matmul(a, b, out_dtype, *, tm, tn, tk=None, residual=None, b_index=None):
    m, kd = a.shape
    n = b.shape[-1]
    tk = kd if tk is None else tk
    nk = kd // tk
    assert m % tm == 0 and n % tn == 0 and kd % tk == 0
    if b_index is None:
        b_spec = pl.BlockSpec((tk, tn), lambda i, j, k: (k, j))
    else:
        b_spec = pl.BlockSpec((None, tk, tn), lambda i, j, k: (b_index, k, j))
    in_specs = [pl.BlockSpec((tm, tk), lambda i, j, k: (i, k)), b_spec]
    args = [a, b]
    if residual is not None:
        in_specs.append(pl.BlockSpec((tm, tn), lambda i, j, k: (i, j)))
        args.append(residual)
    return pl.pallas_call(
        functools.partial(_mm_kernel, nk=nk, has_res=residual is not None),
        grid=(m // tm, n // tn, nk),
        in_specs=in_specs,
        out_specs=pl.BlockSpec((tm, tn), lambda i, j, k: (i, j)),
        out_shape=jax.ShapeDtypeStruct((m, n), out_dtype),
        scratch_shapes=[pltpu.VMEM((tm, tn), F32)] if nk > 1 else [],
        compiler_params=_params("parallel", "parallel", "arbitrary"),
        name="matmul",
    )(*args)


def _gate_up_kernel(a_ref, wg_ref, wu_ref, o_ref):
    a = a_ref[...]
    gate = _dot(a, wg_ref[...].astype(BF16))
    up = _dot(a, wu_ref[...].astype(BF16))
    o_ref[...] = (gate * jax.nn.sigmoid(gate) * up).astype(o_ref.dtype)


def _gate_up(a, w_gate_up, layer, *, tm, tn):
    m, kd = a.shape
    f = w_gate_up.shape[-1] // 2
    nf = f // tn
    assert m % tm == 0 and f % tn == 0
    return pl.pallas_call(
        _gate_up_kernel,
        grid=(m // tm, nf),
        in_specs=[pl.BlockSpec((tm, kd), lambda i, j: (i, 0)),
                  pl.BlockSpec((None, kd, tn), lambda i, j: (layer, 0, j)),
                  pl.BlockSpec((None, kd, tn), lambda i, j: (layer, 0, j + nf))],
        out_specs=pl.BlockSpec((tm, tn), lambda i, j: (i, j)),
        out_shape=jax.ShapeDtypeStruct((m, f), BF16),
        compiler_params=_params("parallel", "parallel", vmem_limit_bytes=GATE_UP_VMEM_LIMIT_BYTES),
        name="gate_up",
    )(a, w_gate_up, w_gate_up)


_SB_DEAD = -200.0


def _sb_kernel(q_ref, k_ref, v_ref, o_ref, acc_ref, later_ref, z_ref, w_ref, *, tq, tk):
    qi = pl.program_id(1)
    ratio = tq // tk
    assert ratio % 2 == 0
    q = q_ref[...]
    row = lax.broadcasted_iota(jnp.int32, (tk, tk), 0)
    col = lax.broadcasted_iota(jnp.int32, (tk, tk), 1)
    suffix = jnp.where(row >= col, 1.0, 0.0).astype(BF16)
    acc_ref[...] = jnp.zeros_like(acc_ref)
    later_ref[...] = jnp.zeros_like(later_ref)

    def logits(j, r0=0):
        start = pl.multiple_of(j * tk, tk)
        return _dot_nt(q[r0:], k_ref[pl.ds(start, tk), :]) * (-SCALE * LOG2E)

    def weights(j, nz, masked, r0=0):
        neg_abs = lax.bitcast_convert_type(
            lax.bitcast_convert_type(nz, jnp.uint32) | jnp.uint32(0x80000000), F32)
        a = jnp.minimum(nz, 0.0) - jnp.log2(1.0 + jnp.exp2(neg_abs))
        if masked:
            t_pos = qi * tq + r0 + lax.broadcasted_iota(jnp.int32, nz.shape, 0)
            s_pos = j * tk + lax.broadcasted_iota(jnp.int32, nz.shape, 1)
            before = s_pos < t_pos
            a = jnp.where(before, a, 0.0)
        within = _dot(a.astype(BF16), suffix)
        later = later_ref[r0:, :]
        w = jnp.exp2(within + jnp.concatenate([later] * (tk // HEAD_DIM), axis=1) - nz)
        later_ref[r0:, :] = later + within[:, 0:1]
        if masked:
            w = jnp.where(before, w, 0.0)
        return w.astype(BF16)

    def accumulate(j, w, r0=0):
        start = pl.multiple_of(j * tk, tk)
        acc_ref[r0:, :] += _dot(w, v_ref[pl.ds(start, tk), :])

    n_full = ratio * qi

    for b in reversed(range(ratio)):
        j = n_full + b
        accumulate(j, weights(j, logits(j, b * tk), True, b * tk), b * tk)

    def step(j, slot):
        accumulate(j + 1, w_ref[1 - slot])
        z_ref[1 - slot] = logits(jnp.maximum(j - 1, 0))
        w_ref[slot] = weights(j, z_ref[slot], False)

    def remaining_mass():
        return jnp.max(later_ref[...])

    def two_steps(carry):
        s, _ = carry
        j = n_full - 1 - 2 * s
        step(j, 0)
        step(j - 1, 1)
        return s + 1, remaining_mass()

    def unfinished(carry):
        s, mass = carry
        return jnp.logical_and(s < n_full // 2, mass > _SB_DEAD)

    z_ref[0] = logits(jnp.maximum(n_full - 1, 0))
    w_ref[1] = jnp.zeros((tq, tk), BF16)
    trips, _ = lax.while_loop(unfinished, two_steps, (jnp.int32(0), remaining_mass()))
    accumulate(n_full - 2 * trips, w_ref[1])
    o_ref[...] = acc_ref[...].astype(o_ref.dtype)


def _stick_breaking(z, *, n_heads, q_col, k_col, v_col, tq=1024, tk=256):
    s = z.shape[0]
    return pl.pallas_call(
        functools.partial(_sb_kernel, tq=tq, tk=tk),
        grid=(n_heads, s // tq),
        in_specs=[pl.BlockSpec((tq, HEAD_DIM), lambda h, i: (i, q_col + h)),
                  pl.BlockSpec((s, HEAD_DIM), lambda h, i: (0, k_col + h)),
                  pl.BlockSpec((s, HEAD_DIM), lambda h, i: (0, v_col + h))],
        out_specs=pl.BlockSpec((tq, HEAD_DIM), lambda h, i: (i, h)),
        out_shape=jax.ShapeDtypeStruct((s, n_heads * HEAD_DIM), BF16),
        scratch_shapes=[pltpu.VMEM((tq, HEAD_DIM), F32),
                        pltpu.VMEM((tq, HEAD_DIM), F32),
                        pltpu.VMEM((2, tq, tk), F32),
                        pltpu.VMEM((2, tq, tk), BF16)],
        compiler_params=_params("parallel", "arbitrary"),
        name="stick_breaking",
    )(z, z, z)


def _swa_kernel(sink_ref, q_ref, kp_ref, kc_ref, vp_ref, vc_ref, o_ref, *, slopes):
    i = pl.program_id(0)
    n_kv, g_per = slopes.shape
    qq = lax.broadcasted_iota(jnp.int32, (BLK, 2 * BLK), 0)
    kk = lax.broadcasted_iota(jnp.int32, (BLK, 2 * BLK), 1)
    dist = qq + BLK - kk
    first_key = jnp.where(i > 0, 0, BLK)
    mask = (dist >= 0) & (dist < SW_WINDOW) & (kk >= first_key)
    dist_f = dist.astype(F32)
    for h in range(n_kv):
        cs = slice(h * HEAD_DIM, (h + 1) * HEAD_DIM)
        k = jnp.concatenate([kp_ref[:, cs], kc_ref[:, cs]], axis=0)
        v = jnp.concatenate([vp_ref[:, cs], vc_ref[:, cs]], axis=0)
        heads = [h * g_per + g for g in range(g_per)]
        cols = [slice(head * HEAD_DIM, (head + 1) * HEAD_DIM) for head in heads]
        sinks = [sink_ref[head] for head in heads]
        s = [_dot_nt(q_ref[:, hs], k) for hs in cols]
        s = [jnp.where(mask, x * SCALE - float(slopes[h, g]) * dist_f, NEG) for g, x in enumerate(s)]
        m = [jnp.maximum(jnp.max(x, axis=-1, keepdims=True), sink) for x, sink in zip(s, sinks)]
        e = [jnp.where(mask, jnp.exp(x - mx), 0.0) for x, mx in zip(s, m)]
        p = [x / (jnp.sum(x, axis=-1, keepdims=True) + jnp.exp(sink - mx))
             for x, mx, sink in zip(e, m, sinks)]
        for hs, x in zip(cols, p):
            o_ref[:, hs] = _dot(x.astype(BF16), v).astype(o_ref.dtype)


def _window_sink_attention(z, sinks, *, q_col, k_col, v_col):
    s = z.shape[0]
    qw = H_SW * HEAD_DIM
    kw = KV_SW * HEAD_DIM
    assert q_col % qw == 0 and k_col % kw == 0 and v_col % kw == 0
    qb, kb, vb = q_col // qw, k_col // kw, v_col // kw
    prev = lambda i: jnp.maximum(i - 1, 0)
    return pl.pallas_call(
        functools.partial(_swa_kernel, slopes=_alibi_slopes(H_SW, KV_SW)),
        grid=(s // BLK,),
        in_specs=[pl.BlockSpec(memory_space=pltpu.SMEM),
                  pl.BlockSpec((BLK, qw), lambda i: (i, qb)),
                  pl.BlockSpec((BLK, kw), lambda i: (prev(i), kb)),
                  pl.BlockSpec((BLK, kw), lambda i: (i, kb)),
                  pl.BlockSpec((BLK, kw), lambda i: (prev(i), vb)),
                  pl.BlockSpec((BLK, kw), lambda i: (i, vb))],
        out_specs=pl.BlockSpec((BLK, qw), lambda i: (i, 0)),
        out_shape=jax.ShapeDtypeStruct((s, qw), BF16),
        compiler_params=_params("parallel"),
        name="window_sink_attention",
    )(sinks.astype(F32), z, z, z, z, z)


def _compress_kernel(x_ref, pos_ref, w1_ref, w2_ref, o_ref):
    x = x_ref[0].astype(F32)
    half = x.shape[1]
    n_chunk = x.shape[0]
    lo = _dot((x + pos_ref[:, :half]).astype(BF16), w1_ref[0, :half, :])
    hi = _dot((x + pos_ref[:, half:]).astype(BF16), w1_ref[0, half:, :])
    pre = lo + pltpu.roll(hi, n_chunk - 1, 0)
    act = jax.nn.gelu(pre, approximate=True)
    out = _dot(act.astype(BF16), w2_ref[0])
    last = lax.broadcasted_iota(jnp.int32, out.shape, 0) == n_chunk - 1
    o_ref[0] = jnp.where(last, 0.0, out).astype(o_ref.dtype)


def _compress(x, pos, w1, w2):
    n, n_chunk, width = x.shape
    hid = w1.shape[-1]
    return pl.pallas_call(
        _compress_kernel,
        grid=(n,),
        in_specs=[pl.BlockSpec((1, n_chunk, width), lambda i: (i, 0, 0)),
                  pl.BlockSpec((1, 2 * width), lambda i: (0, 0)),
                  pl.BlockSpec((1, 2 * width, hid), lambda i: (i // KV_NSA, 0, 0)),
                  pl.BlockSpec((1, hid, HEAD_DIM), lambda i: (i // KV_NSA, 0, 0))],
        out_specs=pl.BlockSpec((1, n_chunk, HEAD_DIM), lambda i: (i, 0, 0)),
        out_shape=jax.ShapeDtypeStruct((n, n_chunk, HEAD_DIM), BF16),
        compiler_params=_params("parallel"),
        name="nsa_compress",
    )(x, pos, w1, w2)


_CMP_CHUNK = 256
_CMP_UNROLL = 8


def _split3(x):
    hi = x.astype(BF16)
    r1 = x - hi.astype(F32)
    mid = r1.astype(BF16)
    lo = (r1 - mid.astype(F32)).astype(BF16)
    return hi, mid, lo


def _nsa_cmp_kernel(slope_ref, q_ref, kc_ref, vc_ref, o_ref, sel_ref, need_ref,
                    imp_ref, qs_ref, os_ref, *, g_per, n_slc):
    h = pl.program_id(0)
    i = pl.program_id(1)
    n_cmp = kc_ref.shape[1]
    _stack_heads(q_ref, qs_ref, g_per)
    imp_ref[...] = jnp.zeros_like(imp_ref)

    def attend(width):
        kc = kc_ref[0, :width, :]
        vc = vc_ref[0, :width, :]
        t = i * BLK + lax.broadcasted_iota(jnp.int32, (BLK, width), 0)
        c_end = (lax.broadcasted_iota(jnp.int32, (BLK, width), 1) * NSA_CMP_STRIDE
                 + (NSA_CMP_LEN - 1))
        dist = t - c_end
        mask = dist >= 0
        dist_f = dist.astype(F32)

        def group(gg, carry):
            heads = [gg * _CMP_UNROLL + u for u in range(_CMP_UNROLL)]
            rows = [pl.ds(pl.multiple_of(g * BLK, BLK), BLK) for g in heads]
            s = [_dot_nt(qs_ref[r, :], kc) for r in rows]
            s = [jnp.where(mask, x * (SCALE * LOG2E) - slope_ref[h * g_per + g] * dist_f, NEG)
                 for x, g in zip(s, heads)]
            m = [jnp.max(x, axis=-1, keepdims=True) for x in s]
            e = [jnp.where(mask, jnp.exp2(x - mx), 0.0) for x, mx in zip(s, m)]
            p = [x / jnp.maximum(jnp.sum(x, axis=-1, keepdims=True), 1e-30) for x in e]
            for r, x in zip(rows, p):
                os_ref[r, :] = _dot(x.astype(BF16), vc)
            total = p[0]
            for x in p[1:]:
                total = total + x
            imp_ref[:, :width] += total
            return carry

        lax.fori_loop(0, g_per // _CMP_UNROLL, group, 0)

    n_visible = (BLK * i + BLK - NSA_CMP_LEN) // NSA_CMP_STRIDE + 1
    widths = sorted({min(w, n_cmp) for w in range(_CMP_CHUNK, n_cmp + _CMP_CHUNK, _CMP_CHUNK)})
    case = jnp.minimum((n_visible - 1) // _CMP_CHUNK, len(widths) - 1)
    for idx, width in enumerate(widths):
        pl.when(case == idx)(functools.partial(attend, width))
    for g in range(g_per):
        o_ref[:, g * HEAD_DIM:(g + 1) * HEAD_DIM] = os_ref[g * BLK:(g + 1) * BLK, :]

    r = NSA_SEL_LEN // NSA_CMP_STRIDE
    ci = lax.broadcasted_iota(jnp.int32, (n_cmp, n_slc), 0)
    cj = lax.broadcasted_iota(jnp.int32, (n_cmp, n_slc), 1)
    pool = jnp.where((ci >= r * cj - 1) & (ci <= r * cj + r - 1), 1.0, 0.0).astype(BF16)
    hi, mid, lo = _split3(imp_ref[...])
    imp = _dot(hi, pool) + (_dot(mid, pool) + _dot(lo, pool))

    blk = lax.broadcasted_iota(jnp.int32, (BLK, n_slc), 1)
    cur = (i * BLK + lax.broadcasted_iota(jnp.int32, (BLK, n_slc), 0)) // NSA_SEL_LEN
    valid = blk <= cur
    forced = (blk == 0) | (blk == cur) | (blk == cur - 1)
    score = jnp.where(valid, imp + jnp.where(forced, NSA_FORCE, 0.0), NEG)
    blk_f = blk.astype(F32)
    chosen = jnp.zeros((BLK, n_slc), F32)
    for _ in range(min(NSA_TOP_N, n_slc)):
        best = jnp.max(score, axis=-1, keepdims=True)
        first = jnp.min(jnp.where(score == best, blk_f, float(n_slc)), axis=-1, keepdims=True)
        pick = blk_f == first
        chosen = jnp.where(pick, 1.0, chosen)
        score = jnp.where(pick, -jnp.inf, score)
    chosen = jnp.where(valid, chosen, 0.0)
    sel_ref[0] = chosen.astype(sel_ref.dtype)

    n_kb = n_slc * NSA_SEL_LEN // BLK
    any_q = jnp.max(chosen, axis=0, keepdims=True)
    any_q = jnp.broadcast_to(any_q, (8, n_slc)).astype(BF16)
    pi = lax.broadcasted_iota(jnp.int32, (n_slc, n_kb), 0)
    pj = lax.broadcasted_iota(jnp.int32, (n_slc, n_kb), 1)
    per_kb = _dot(any_q, jnp.where(pi * NSA_SEL_LEN // BLK == pj, 1.0, 0.0).astype(BF16))
    per_kb = jnp.where(per_kb > 0.5, 1.0, 0.0).astype(BF16)
    n_word = need_ref.shape[-1]
    bi = lax.broadcasted_iota(jnp.int32, (n_kb, n_word), 0)
    bw = lax.broadcasted_iota(jnp.int32, (n_kb, n_word), 1)
    weight = jnp.where(bi // 16 == bw, jnp.left_shift(1, bi % 16), 0).astype(F32).astype(BF16)
    words = _dot(per_kb, weight)
    need_ref[0, 0] = words.astype(jnp.int32)


def _nsa_compressed(z, k_cmp, v_cmp, slopes):
    s = z.shape[0]
    n_slc = s // NSA_SEL_LEN
    g_per = H_NSA // KV_NSA
    qw = g_per * HEAD_DIM
    n_cmp = k_cmp.shape[1]
    n_word = 128
    assert (s // BLK) <= 16 * n_word
    return pl.pallas_call(
        functools.partial(_nsa_cmp_kernel, g_per=g_per, n_slc=n_slc),
        grid=(KV_NSA, s // BLK),
        in_specs=[pl.BlockSpec(memory_space=pltpu.SMEM),
                  pl.BlockSpec((BLK, qw), lambda h, i: (i, h)),
                  pl.BlockSpec((1, n_cmp, HEAD_DIM), lambda h, i: (h, 0, 0)),
                  pl.BlockSpec((1, n_cmp, HEAD_DIM), lambda h, i: (h, 0, 0))],
        out_specs=[pl.BlockSpec((BLK, qw), lambda h, i: (i, h)),
                   pl.BlockSpec((1, BLK, n_slc), lambda h, i: (h, i, 0)),
                   pl.BlockSpec((1, 1, 8, n_word), lambda h, i: (h, i, 0, 0))],
        out_shape=[jax.ShapeDtypeStruct((s, H_NSA * HEAD_DIM), F32),
                   jax.ShapeDtypeStruct((KV_NSA, s, n_slc), BF16),
                   jax.ShapeDtypeStruct((KV_NSA, s // BLK, 8, n_word), jnp.int32)],
        scratch_shapes=[pltpu.VMEM((BLK, n_cmp), F32),
                        pltpu.VMEM((g_per * BLK, HEAD_DIM), BF16),
                        pltpu.VMEM((g_per * BLK, HEAD_DIM), F32)],
        compiler_params=_params("parallel", "arbitrary"),
        name="nsa_compressed",
    )(slopes, z, k_cmp, v_cmp)


def _stack_heads(q_ref, qs_ref, g_per):
    for g in range(g_per):
        qs_ref[g * BLK:(g + 1) * BLK, :] = q_ref[:, g * HEAD_DIM:(g + 1) * HEAD_DIM]


_ROW_SPLIT = 2


def _online_visit(i, blocks, qs_ref, k_ref, v_ref, slope_ref, m_ref, l_ref, acc_ref, g_per):
    nb = len(blocks)
    starts = [pl.multiple_of(j * BLK, BLK) for j, _ in blocks]
    k = jnp.concatenate([k_ref[pl.ds(st, BLK), :] for st in starts], axis=0)
    v = jnp.concatenate([v_ref[pl.ds(st, BLK), :] for st in starts], axis=0)
    v_ones = jnp.concatenate([v, jnp.ones((nb * BLK, HEAD_DIM), BF16)], axis=1)
    qq = lax.broadcasted_iota(jnp.int32, (BLK, BLK), 0) - lax.broadcasted_iota(jnp.int32, (BLK, BLK), 1)
    dists = [(i - j) * BLK + qq for j, _ in blocks]
    ok1 = jnp.concatenate([key_ok(d) for (_, key_ok), d in zip(blocks, dists)], axis=1)
    dist1 = jnp.concatenate([d.astype(F32) for d in dists], axis=1)
    g_sub = g_per // _ROW_SPLIT
    ok = jnp.concatenate([ok1] * g_sub, axis=0) > 0.5
    dist_f = jnp.concatenate([dist1] * g_sub, axis=0)

    groups = [slice(r * g_sub * BLK, (r + 1) * g_sub * BLK) for r in range(_ROW_SPLIT)]
    s = [_dot_nt(qs_ref[rows, :], k) for rows in groups]
    s = [jnp.where(ok, x * (SCALE * LOG2E) - jnp.concatenate([slope_ref[0, rows, :]] * nb, axis=1) * dist_f, NEG)
         for x, rows in zip(s, groups)]
    m_prev = [m_ref[rows, :] for rows in groups]
    m_new = [jnp.maximum(mp, jnp.max(x, axis=-1, keepdims=True)) for mp, x in zip(m_prev, s)]
    alpha = [jnp.exp2(mp - mn) for mp, mn in zip(m_prev, m_new)]
    p = [jnp.where(ok, jnp.exp2(x - jnp.concatenate([mn] * nb, axis=1)), 0.0) for x, mn in zip(s, m_new)]
    pv = [_dot(x.astype(BF16), v_ones) for x in p]
    for rows, a, y, mn in zip(groups, alpha, pv, m_new):
        l_ref[rows, :] = a * l_ref[rows, :] + y[:, HEAD_DIM:]
        acc_ref[rows, :] = a * acc_ref[rows, :] + y[:, :HEAD_DIM]
        m_ref[rows, :] = mn


def _online_init(m_ref, l_ref, acc_ref):
    m_ref[...] = jnp.full_like(m_ref, NEG)
    l_ref[...] = jnp.zeros_like(l_ref)
    acc_ref[...] = jnp.zeros_like(acc_ref)


_SEL_GROUP = 8


def _nsa_sel_kernel(need_ref, q_ref, sel_ref, slope_ref, k_ref, v_ref, o_ref,
                    qs_ref, m_ref, l_ref, acc_ref, list_ref, *, g_per, n_qblk, n_word):
    h = pl.program_id(0)
    i = pl.program_id(1)
    _stack_heads(q_ref, qs_ref, g_per)
    _online_init(m_ref, l_ref, acc_ref)
    sel = sel_ref[0]
    n_slc = sel.shape[1]
    per_blk = BLK // NSA_SEL_LEN
    eb = lax.broadcasted_iota(jnp.int32, (n_slc, BLK), 0)
    el = lax.broadcasted_iota(jnp.int32, (n_slc, BLK), 1) // NSA_SEL_LEN
    base = (h * n_qblk + i) * n_word

    def scan(j, n):
        word = need_ref[base + jnp.right_shift(j, 4)]
        needed = jnp.bitwise_and(jnp.right_shift(word, jnp.bitwise_and(j, 15)), 1)
        list_ref[n] = j
        return n + needed

    n_needed = lax.fori_loop(0, i + 1, scan, 0)

    def visit(first, count):
        blocks = []
        for u in range(count):
            j = list_ref[first + u]
            expand = jnp.where(eb == per_blk * j + el, 1.0, 0.0).astype(BF16)
            picked = _dot(sel, expand)
            blocks.append((j, functools.partial(
                lambda dist, picked: jnp.where(dist >= 0, picked, 0.0), picked=picked)))
        _online_visit(i, blocks, qs_ref, k_ref, v_ref, slope_ref, m_ref, l_ref, acc_ref, g_per)

    def group(t, carry):
        visit(t * _SEL_GROUP, _SEL_GROUP)
        return carry

    n_groups = n_needed // _SEL_GROUP
    lax.fori_loop(0, n_groups, group, 0)
    done = n_groups * _SEL_GROUP
    size = _SEL_GROUP // 2
    while size >= 1:
        has = jnp.bitwise_and(n_needed, size) != 0
        pl.when(has)(functools.partial(visit, done, size))
        done = done + jnp.where(has, size, 0)
        size //= 2
    out = acc_ref[...] / jnp.maximum(l_ref[...], 1e-30)
    for g in range(g_per):
        o_ref[:, g * HEAD_DIM:(g + 1) * HEAD_DIM] = out[g * BLK:(g + 1) * BLK, :]


def _nsa_selected(z, sel, need, slope_rows, *, k_col, v_col):
    s = z.shape[0]
    g_per = H_NSA // KV_NSA
    qw = g_per * HEAD_DIM
    rows = g_per * BLK
    n_slc = sel.shape[-1]
    n_qblk = s // BLK
    n_word = -(-n_qblk // 16)
    need = need[:, :, 0, :n_word]
    return pl.pallas_call(
        functools.partial(_nsa_sel_kernel, g_per=g_per, n_qblk=n_qblk, n_word=n_word),
        grid=(KV_NSA, n_qblk),
        in_specs=[pl.BlockSpec(memory_space=pltpu.SMEM),
                  pl.BlockSpec((BLK, qw), lambda h, i: (i, h)),
                  pl.BlockSpec((1, BLK, n_slc), lambda h, i: (h, i, 0)),
                  pl.BlockSpec((1, rows, HEAD_DIM), lambda h, i: (h, 0, 0)),
                  pl.BlockSpec((s, HEAD_DIM), lambda h, i: (0, k_col + h)),
                  pl.BlockSpec((s, HEAD_DIM), lambda h, i: (0, v_col + h))],
        out_specs=pl.BlockSpec((BLK, qw), lambda h, i: (i, h)),
        out_shape=jax.ShapeDtypeStruct((s, H_NSA * HEAD_DIM), F32),
        scratch_shapes=[pltpu.VMEM((rows, HEAD_DIM), BF16),
                        pltpu.VMEM((rows, HEAD_DIM), F32),
                        pltpu.VMEM((rows, HEAD_DIM), F32),
                        pltpu.VMEM((rows, HEAD_DIM), F32),
                        pltpu.SMEM((n_qblk + _SEL_GROUP,), jnp.int32)],
        compiler_params=_params("parallel", "arbitrary"),
        name="nsa_selected",
    )(need.reshape(-1), z, sel, slope_rows, z, z)


def _nsa_win_kernel(q_ref, slope_ref, k_ref, v_ref, gate_ref, ocmp_ref, oslc_ref, o_ref,
                    qs_ref, m_ref, l_ref, acc_ref, *, g_per):
    i = pl.program_id(1)
    _stack_heads(q_ref, qs_ref, g_per)
    _online_init(m_ref, l_ref, acc_ref)
    n_prev = -(-(NSA_WINDOW - 1) // BLK)

    blocks = []
    for step in range(n_prev + 1):
        j = i - n_prev + step
        present = jnp.where(j >= 0, 1.0, 0.0)
        blocks.append((jnp.maximum(j, 0), functools.partial(
            lambda dist, present: jnp.where((dist >= 0) & (dist < NSA_WINDOW), present, 0.0),
            present=present)))
    _online_visit(i, blocks, qs_ref, k_ref, v_ref, slope_ref, m_ref, l_ref, acc_ref, g_per)
    o_win = acc_ref[...] / jnp.maximum(l_ref[...], 1e-30)
    gates = jax.nn.sigmoid(gate_ref[...])
    for g in range(g_per):
        hs = slice(g * HEAD_DIM, (g + 1) * HEAD_DIM)
        o = (gates[:, 3 * g:3 * g + 1] * ocmp_ref[:, hs]
             + gates[:, 3 * g + 1:3 * g + 2] * oslc_ref[:, hs]
             + gates[:, 3 * g + 2:3 * g + 3] * o_win[g * BLK:(g + 1) * BLK, :])
        o_ref[:, hs] = o.astype(o_ref.dtype)


def _nsa_window_combine(z, gate_logits, o_cmp, o_slc, slope_rows, *, k_col, v_col):
    s = z.shape[0]
    g_per = H_NSA // KV_NSA
    qw = g_per * HEAD_DIM
    rows = g_per * BLK
    head_blk = lambda h, i: (i, h)
    return pl.pallas_call(
        functools.partial(_nsa_win_kernel, g_per=g_per),
        grid=(KV_NSA, s // BLK),
        in_specs=[pl.BlockSpec((BLK, qw), head_blk),
                  pl.BlockSpec((1, rows, HEAD_DIM), lambda h, i: (h, 0, 0)),
                  pl.BlockSpec((s, HEAD_DIM), lambda h, i: (0, k_col + h)),
                  pl.BlockSpec((s, HEAD_DIM), lambda h, i: (0, v_col + h)),
                  pl.BlockSpec((BLK, 128), head_blk),
                  pl.BlockSpec((BLK, qw), head_blk),
                  pl.BlockSpec((BLK, qw), head_blk)],
        out_specs=pl.BlockSpec((BLK, qw), head_blk),
        out_shape=jax.ShapeDtypeStruct((s, H_NSA * HEAD_DIM), BF16),
        scratch_shapes=[pltpu.VMEM((rows, HEAD_DIM), BF16),
                        pltpu.VMEM((rows, HEAD_DIM), F32),
                        pltpu.VMEM((rows, HEAD_DIM), F32),
                        pltpu.VMEM((rows, HEAD_DIM), F32)],
        compiler_params=_params("parallel", "arbitrary"),
        name="nsa_window_combine",
    )(z, slope_rows, z, z, gate_logits, o_cmp, o_slc)


def _ab_mixer(h, w_in, sinks, w_out, x_res):
    z = _matmul(h, w_in.astype(BF16), BF16, tm=1024, tn=768)
    d_sb = H_SB * HEAD_DIM
    o_a = _stick_breaking(z, n_heads=H_SB, q_col=0, k_col=H_SB, v_col=2 * H_SB)
    q_b = 3 * d_sb
    k_b = q_b + H_SW * HEAD_DIM
    v_b = k_b + KV_SW * HEAD_DIM
    o_b = _window_sink_attention(z, sinks, q_col=q_b, k_col=k_b, v_col=v_b)
    o = jnp.concatenate([o_a, o_b], axis=-1)
    return _matmul(o, w_out.astype(BF16), F32, tm=1024, tn=512, residual=x_res)


def _nsa_mixer(h, w_in, cmp_pos, wk1, wk2, wv1, wv2, w_out, x_res):
    s = h.shape[0]
    dq = H_NSA * HEAD_DIM
    dkv = KV_NSA * HEAD_DIM
    g_per = H_NSA // KV_NSA
    main = dq + 6 * dkv
    z = _matmul(h, w_in[:, :main].astype(BF16), BF16, tm=1024, tn=512)
    w_gate = w_in[:, main:].reshape(-1, KV_NSA, 3 * g_per)
    w_gate = jnp.pad(w_gate, ((0, 0), (0, 0), (0, 128 - 3 * g_per))).reshape(-1, KV_NSA * 128)
    gate_logits = _matmul(h, w_gate.astype(BF16), F32, tm=1024, tn=KV_NSA * 128)

    n_chunk = s // NSA_CMP_STRIDE
    to_chunks = lambda c: (z[:, c:c + dkv].reshape(n_chunk, NSA_CMP_STRIDE, KV_NSA, HEAD_DIM)
                           .transpose(2, 0, 1, 3).reshape(KV_NSA, n_chunk, NSA_CMP_STRIDE * HEAD_DIM))
    chunks = jnp.concatenate([to_chunks(dq), to_chunks(dq + dkv)], axis=0)
    kv_cmp = _compress(chunks, cmp_pos.reshape(1, -1).astype(F32),
                       jnp.stack([wk1, wv1]).astype(BF16), jnp.stack([wk2, wv2]).astype(BF16))
    k_cmp, v_cmp = kv_cmp[:KV_NSA], kv_cmp[KV_NSA:]

    slopes = _alibi_slopes(H_NSA, KV_NSA) * np.float32(LOG2E)
    slope_rows = jnp.asarray(np.repeat(np.repeat(slopes, BLK, axis=1)[..., None], HEAD_DIM, axis=2))
    o_cmp, sel, need = _nsa_compressed(z, k_cmp, v_cmp, jnp.asarray(slopes.reshape(-1)))
    col = lambda off: (dq + off * dkv) // HEAD_DIM
    o_slc = _nsa_selected(z, sel, need, slope_rows, k_col=col(2), v_col=col(3))
    o = _nsa_window_combine(z, gate_logits, o_cmp, o_slc, slope_rows, k_col=col(4), v_col=col(5))
    return _matmul(o, w_out.astype(BF16), F32, tm=1024, tn=512, residual=x_res)


def _swiglu(h, w_gate_up, w_down, layer, x_res):
    act = _gate_up(h, w_gate_up, layer, tm=2048, tn=256)
    return _matmul(act, w_down, F32, tm=512, tn=512, residual=x_res, b_index=layer)


def kernel(x, attn_norm, ffn_norm, w_gate_up, w_down, ab_w_in, ab_sinks, ab_w_out, nsa_w_in,
           nsa_cmp_pos, nsa_cmp_wk1, nsa_cmp_wk2, nsa_cmp_wv1, nsa_cmp_wv2, nsa_w_out, final_norm):
    b, s, d = x.shape
    w_down = w_down.astype(BF16)
    outs = []
    for bi in range(b):
        xb = x[bi]
        for layer in range(attn_norm.shape[0]):
            h = _rmsnorm(xb, attn_norm[layer], BF16)
            if layer % 2 == 0:
                e = layer // 2
                xb = _ab_mixer(h, ab_w_in[e], ab_sinks[e], ab_w_out[e], xb)
            else:
                o = layer // 2
                xb = _nsa_mixer(h, nsa_w_in[o], nsa_cmp_pos[o], nsa_cmp_wk1[o], nsa_cmp_wk2[o],
                                nsa_cmp_wv1[o], nsa_cmp_wv2[o], nsa_w_out[o], xb)
            h = _rmsnorm(xb, ffn_norm[layer], BF16)
            xb = _swiglu(h, w_gate_up, w_down, layer, xb)
        outs.append(_rmsnorm(xb, final_norm, F32))
    return jnp.stack(outs, axis=0)
```

```python
import functools

import numpy as np
import jax
import jax.numpy as jnp
from jax import lax
from jax.experimental import pallas as pl
from jax.experimental.pallas import tpu as pltpu

F32 = jnp.float32
BF16 = jnp.bfloat16

HEAD_DIM = 128
BLK = 128
EPS = 1e-5
NEG = -1e30
SCALE = HEAD_DIM ** -0.5
LOG2E = 1.4426950408889634

H_SB = 8
H_SW = 24
KV_SW = 3
SW_WINDOW = 128

H_NSA = 32
KV_NSA = 2
NSA_CMP_LEN = 32
NSA_CMP_STRIDE = 16
NSA_SEL_LEN = 64
NSA_TOP_N = 8
NSA_WINDOW = 512
NSA_FORCE = 1e4

VMEM_LIMIT_BYTES = 52 * 2 ** 20
GATE_UP_VMEM_LIMIT_BYTES = 60 * 2 ** 20

_NT = (((1,), (1,)), ((), ()))


def _params(*sem, vmem_limit_bytes=VMEM_LIMIT_BYTES):
    return pltpu.CompilerParams(dimension_semantics=sem, vmem_limit_bytes=vmem_limit_bytes)


def _dot(a, b):
    return jnp.dot(a, b, preferred_element_type=F32)


def _dot_nt(a, b):
    return lax.dot_general(a, b, _NT, preferred_element_type=F32)


def _alibi_slopes(n_heads, n_kv):
    s = np.exp2(np.float32(-8.0) * np.arange(1, n_heads + 1, dtype=np.float32)
                / np.float32(n_heads)).astype(np.float32)
    return s.reshape(n_heads // n_kv, n_kv).T


def _rmsnorm_kernel(x_ref, g_ref, o_ref):
    x = x_ref[...]
    ms = jnp.mean(x * x, axis=-1, keepdims=True)
    o_ref[...] = (x * lax.rsqrt(ms + EPS) * g_ref[...]).astype(o_ref.dtype)


def _rmsnorm(x, g, out_dtype, tm=256):
    m, d = x.shape
    return pl.pallas_call(
        _rmsnorm_kernel,
        grid=(m // tm,),
        in_specs=[pl.BlockSpec((tm, d), lambda i: (i, 0)),
                  pl.BlockSpec((1, d), lambda i: (0, 0))],
        out_specs=pl.BlockSpec((tm, d), lambda i: (i, 0)),
        out_shape=jax.ShapeDtypeStruct((m, d), out_dtype),
        compiler_params=_params("parallel"),
        name="rmsnorm",
    )(x, g.reshape(1, d).astype(F32))


def _mm_kernel(*refs, nk, has_res):
    if has_res:
        a_ref, b_ref, r_ref, o_ref = refs[:4]
        scratch = refs[4:]
    else:
        a_ref, b_ref, o_ref = refs[:3]
        r_ref = None
        scratch = refs[3:]

    def finish(acc):
        if has_res:
            acc = acc + r_ref[...]
        o_ref[...] = acc.astype(o_ref.dtype)

    part = _dot(a_ref[...], b_ref[...])
    if nk == 1:
        finish(part)
    else:
        acc_ref, = scratch
        k = pl.program_id(2)

        @pl.when(k == 0)
        def _():
            acc_ref[...] = part

        @pl.when(k > 0)
        def _():
            acc_ref[...] += part

        @pl.when(k == nk - 1)
        def _():
            finish(acc_ref[...])


def _matmul(a, b, out_dtype, *, tm, tn, tk=None, residual=None, b_index=None):
    m, kd = a.shape
    n = b.shape[-1]
    tk = kd if tk is None else tk
    nk = kd // tk
    assert m % tm == 0 and n % tn == 0 and kd % tk == 0
    if b_index is None:
        b_spec = pl.BlockSpec((tk, tn), lambda i, j, k: (k, j))
    else:
        b_spec = pl.BlockSpec((None, tk, tn), lambda i, j, k: (b_index, k, j))
    in_specs = [pl.BlockSpec((tm, tk), lambda i, j, k: (i, k)), b_spec]
    args = [a, b]
    if residual is not None:
        in_specs.append(pl.BlockSpec((tm, tn), lambda i, j, k: (i, j)))
        args.append(residual)
    return pl.pallas_call(
        functools.partial(_mm_kernel, nk=nk, has_res=residual is not None),
        grid=(m // tm, n // tn, nk),
        in_specs=in_specs,
        out_specs=pl.BlockSpec((tm, tn), lambda i, j, k: (i, j)),
        out_shape=jax.ShapeDtypeStruct((m, n), out_dtype),
        scratch_shapes=[pltpu.VMEM((tm, tn), F32)] if nk > 1 else [],
        compiler_params=_params("parallel", "parallel", "arbitrary"),
        name="matmul",
    )(*args)


def _gate_up_kernel(a_ref, wg_ref, wu_ref, o_ref):
    a = a_ref[...]
    gate = _dot(a, wg_ref[...].astype(BF16))
    up = _dot(a, wu_ref[...].astype(BF16))
    o_ref[...] = (gate * jax.nn.sigmoid(gate) * up).astype(o_ref.dtype)


def _gate_up(a, w_gate_up, layer, *, tm, tn):
    m, kd = a.shape
    f = w_gate_up.shape[-1] // 2
    nf = f // tn
    assert m % tm == 0 and f % tn == 0
    return pl.pallas_call(
        _gate_up_kernel,
        grid=(m // tm, nf),
        in_specs=[pl.BlockSpec((tm, kd), lambda i, j: (i, 0)),
                  pl.BlockSpec((None, kd, tn), lambda i, j: (layer, 0, j)),
                  pl.BlockSpec((None, kd, tn), lambda i, j: (layer, 0, j + nf))],
        out_specs=pl.BlockSpec((tm, tn), lambda i, j: (i, j)),
        out_shape=jax.ShapeDtypeStruct((m, f), BF16),
        compiler_params=_params("parallel", "parallel", vmem_limit_bytes=GATE_UP_VMEM_LIMIT_BYTES),
        name="gate_up",
    )(a, w_gate_up, w_gate_up)


_SB_DEAD = -200.0


def _sb_kernel(q_ref, k_ref, v_ref, o_ref, acc_ref, later_ref, z_ref, w_ref, *, tq, tk):
    qi = pl.program_id(1)
    ratio = tq // tk
    assert ratio % 2 == 0
    q = q_ref[...]
    row = lax.broadcasted_iota(jnp.int32, (tk, tk), 0)
    col = lax.broadcasted_iota(jnp.int32, (tk, tk), 1)
    suffix = jnp.where(row >= col, 1.0, 0.0).astype(BF16)
    acc_ref[...] = jnp.zeros_like(acc_ref)
    later_ref[...] = jnp.zeros_like(later_ref)

    def logits(j, r0=0):
        start = pl.multiple_of(j * tk, tk)
        return _dot_nt(q[r0:], k_ref[pl.ds(start, tk), :]) * (-SCALE * LOG2E)

    def weights(j, nz, masked, r0=0):
        neg_abs = lax.bitcast_convert_type(
            lax.bitcast_convert_type(nz, jnp.uint32) | jnp.uint32(0x80000000), F32)
        a = jnp.minimum(nz, 0.0) - jnp.log2(1.0 + jnp.exp2(neg_abs))
        if masked:
            t_pos = qi * tq + r0 + lax.broadcasted_iota(jnp.int32, nz.shape, 0)
            s_pos = j * tk + lax.broadcasted_iota(jnp.int32, nz.shape, 1)
            before = s_pos < t_pos
            a = jnp.where(before, a, 0.0)
        within = _dot(a.astype(BF16), suffix)
        later = later_ref[r0:, :]
        w = jnp.exp2(within + jnp.concatenate([later] * (tk // HEAD_DIM), axis=1) - nz)
        later_ref[r0:, :] = later + within[:, 0:1]
        if masked:
            w = jnp.where(before, w, 0.0)
        return w.astype(BF16)

    def accumulate(j, w, r0=0):
        start = pl.multiple_of(j * tk, tk)
        acc_ref[r0:, :] += _dot(w, v_ref[pl.ds(start, tk), :])

    n_full = ratio * qi

    for b in reversed(range(ratio)):
        j = n_full + b
        accumulate(j, weights(j, logits(j, b * tk), True, b * tk), b * tk)

    def step(j, slot):
        accumulate(j + 1, w_ref[1 - slot])
        z_ref[1 - slot] = logits(jnp.maximum(j - 1, 0))
        w_ref[slot] = weights(j, z_ref[slot], False)

    def remaining_mass():
        return jnp.max(later_ref[...])

    def two_steps(carry):
        s, _ = carry
        j = n_full - 1 - 2 * s
        step(j, 0)
        step(j - 1, 1)
        return s + 1, remaining_mass()

    def unfinished(carry):
        s, mass = carry
        return jnp.logical_and(s < n_full // 2, mass > _SB_DEAD)

    z_ref[0] = logits(jnp.maximum(n_full - 1, 0))
    w_ref[1] = jnp.zeros((tq, tk), BF16)
    trips, _ = lax.while_loop(unfinished, two_steps, (jnp.int32(0), remaining_mass()))
    accumulate(n_full - 2 * trips, w_ref[1])
    o_ref[...] = acc_ref[...].astype(o_ref.dtype)


def _stick_breaking(z, *, n_heads, q_col, k_col, v_col, tq=1024, tk=256):
    s = z.shape[0]
    return pl.pallas_call(
        functools.partial(_sb_kernel, tq=tq, tk=tk),
        grid=(n_heads, s // tq),
        in_specs=[pl.BlockSpec((tq, HEAD_DIM), lambda h, i: (i, q_col + h)),
                  pl.BlockSpec((s, HEAD_DIM), lambda h, i: (0, k_col + h)),
                  pl.BlockSpec((s, HEAD_DIM), lambda h, i: (0, v_col + h))],
        out_specs=pl.BlockSpec((tq, HEAD_DIM), lambda h, i: (i, h)),
        out_shape=jax.ShapeDtypeStruct((s, n_heads * HEAD_DIM), BF16),
        scratch_shapes=[pltpu.VMEM((tq, HEAD_DIM), F32),
                        pltpu.VMEM((tq, HEAD_DIM), F32),
                        pltpu.VMEM((2, tq, tk), F32),
                        pltpu.VMEM((2, tq, tk), BF16)],
        compiler_params=_params("parallel", "arbitrary"),
        name="stick_breaking",
    )(z, z, z)


def _swa_kernel(sink_ref, q_ref, kp_ref, kc_ref, vp_ref, vc_ref, o_ref, *, slopes):
    i = pl.program_id(0)
    n_kv, g_per = slopes.shape
    qq = lax.broadcasted_iota(jnp.int32, (BLK, 2 * BLK), 0)
    kk = lax.broadcasted_iota(jnp.int32, (BLK, 2 * BLK), 1)
    dist = qq + BLK - kk
    first_key = jnp.where(i > 0, 0, BLK)
    mask = (dist >= 0) & (dist < SW_WINDOW) & (kk >= first_key)
    dist_f = dist.astype(F32)
    for h in range(n_kv):
        cs = slice(h * HEAD_DIM, (h + 1) * HEAD_DIM)
        k = jnp.concatenate([kp_ref[:, cs], kc_ref[:, cs]], axis=0)
        v = jnp.concatenate([vp_ref[:, cs], vc_ref[:, cs]], axis=0)
        heads = [h * g_per + g for g in range(g_per)]
        cols = [slice(head * HEAD_DIM, (head + 1) * HEAD_DIM) for head in heads]
        sinks = [sink_ref[head] for head in heads]
        s = [_dot_nt(q_ref[:, hs], k) for hs in cols]
        s = [jnp.where(mask, x * SCALE - float(slopes[h, g]) * dist_f, NEG) for g, x in enumerate(s)]
        m = [jnp.maximum(jnp.max(x, axis=-1, keepdims=True), sink) for x, sink in zip(s, sinks)]
        e = [jnp.where(mask, jnp.exp(x - mx), 0.0) for x, mx in zip(s, m)]
        p = [x / (jnp.sum(x, axis=-1, keepdims=True) + jnp.exp(sink - mx))
             for x, mx, sink in zip(e, m, sinks)]
        for hs, x in zip(cols, p):
            o_ref[:, hs] = _dot(x.astype(BF16), v).astype(o_ref.dtype)


def _window_sink_attention(z, sinks, *, q_col, k_col, v_col):
    s = z.shape[0]
    qw = H_SW * HEAD_DIM
    kw = KV_SW * HEAD_DIM
    assert q_col % qw == 0 and k_col % kw == 0 and v_col % kw == 0
    qb, kb, vb = q_col // qw, k_col // kw, v_col // kw
    prev = lambda i: jnp.maximum(i - 1, 0)
    return pl.pallas_call(
        functools.partial(_swa_kernel, slopes=_alibi_slopes(H_SW, KV_SW)),
        grid=(s // BLK,),
        in_specs=[pl.BlockSpec(memory_space=pltpu.SMEM),
                  pl.BlockSpec((BLK, qw), lambda i: (i, qb)),
                  pl.BlockSpec((BLK, kw), lambda i: (prev(i), kb)),
                  pl.BlockSpec((BLK, kw), lambda i: (i, kb)),
                  pl.BlockSpec((BLK, kw), lambda i: (prev(i), vb)),
                  pl.BlockSpec((BLK, kw), lambda i: (i, vb))],
        out_specs=pl.BlockSpec((BLK, qw), lambda i: (i, 0)),
        out_shape=jax.ShapeDtypeStruct((s, qw), BF16),
        compiler_params=_params("parallel"),
        name="window_sink_attention",
    )(sinks.astype(F32), z, z, z, z, z)


def _compress_kernel(x_ref, pos_ref, w1_ref, w2_ref, o_ref):
    x = x_ref[0].astype(F32)
    half = x.shape[1]
    n_chunk = x.shape[0]
    lo = _dot((x + pos_ref[:, :half]).astype(BF16), w1_ref[0, :half, :])
    hi = _dot((x + pos_ref[:, half:]).astype(BF16), w1_ref[0, half:, :])
    pre = lo + pltpu.roll(hi, n_chunk - 1, 0)
    act = jax.nn.gelu(pre, approximate=True)
    out = _dot(act.astype(BF16), w2_ref[0])
    last = lax.broadcasted_iota(jnp.int32, out.shape, 0) == n_chunk - 1
    o_ref[0] = jnp.where(last, 0.0, out).astype(o_ref.dtype)


def _compress(x, pos, w1, w2):
    n, n_chunk, width = x.shape
    hid = w1.shape[-1]
    return pl.pallas_call(
        _compress_kernel,
        grid=(n,),
        in_specs=[pl.BlockSpec((1, n_chunk, width), lambda i: (i, 0, 0)),
                  pl.BlockSpec((1, 2 * width), lambda i: (0, 0)),
                  pl.BlockSpec((1, 2 * width, hid), lambda i: (i // KV_NSA, 0, 0)),
                  pl.BlockSpec((1, hid, HEAD_DIM), lambda i: (i // KV_NSA, 0, 0))],
        out_specs=pl.BlockSpec((1, n_chunk, HEAD_DIM), lambda i: (i, 0, 0)),
        out_shape=jax.ShapeDtypeStruct((n, n_chunk, HEAD_DIM), BF16),
        compiler_params=_params("parallel"),
        name="nsa_compress",
    )(x, pos, w1, w2)


_CMP_CHUNK = 256
_CMP_UNROLL = 8


def _split3(x):
    hi = x.astype(BF16)
    r1 = x - hi.astype(F32)
    mid = r1.astype(BF16)
    lo = (r1 - mid.astype(F32)).astype(BF16)
    return hi, mid, lo


def _nsa_cmp_kernel(slope_ref, q_ref, kc_ref, vc_ref, o_ref, sel_ref, need_ref,
                    imp_ref, qs_ref, os_ref, *, g_per, n_slc):
    i = pl.program_id(0)
    n_kv, n_cmp = kc_ref.shape[0], kc_ref.shape[1]
    n_heads = n_kv * g_per
    assert g_per % _CMP_UNROLL == 0
    _stack_heads(q_ref, qs_ref, n_heads)
    imp_ref[...] = jnp.zeros_like(imp_ref)

    def attend(width):
        t = i * BLK + lax.broadcasted_iota(jnp.int32, (BLK, width), 0)
        c_end = (lax.broadcasted_iota(jnp.int32, (BLK, width), 1) * NSA_CMP_STRIDE
                 + (NSA_CMP_LEN - 1))
        dist = t - c_end
        mask = dist >= 0
        dist_f = dist.astype(F32)

        def group(gg, carry):
            kv = gg // (g_per // _CMP_UNROLL)
            kc = kc_ref[kv, :width, :]
            vc = vc_ref[kv, :width, :]
            heads = [gg * _CMP_UNROLL + u for u in range(_CMP_UNROLL)]
            rows = [pl.ds(pl.multiple_of(g * BLK, BLK), BLK) for g in heads]
            s = [_dot_nt(qs_ref[r, :], kc) for r in rows]
            s = [jnp.where(mask, x * (SCALE * LOG2E) - slope_ref[g] * dist_f, NEG)
                 for x, g in zip(s, heads)]
            m = [jnp.max(x, axis=-1, keepdims=True) for x in s]
            e = [jnp.where(mask, jnp.exp2(x - mx), 0.0) for x, mx in zip(s, m)]
            p = [x / jnp.maximum(jnp.sum(x, axis=-1, keepdims=True), 1e-30) for x in e]
            for r, x in zip(rows, p):
                os_ref[r, :] = _dot(x.astype(BF16), vc)
            total = p[0]
            for x in p[1:]:
                total = total + x
            imp_ref[kv, :, :width] += total
            return carry

        lax.fori_loop(0, n_heads // _CMP_UNROLL, group, 0)

    n_visible = (BLK * i + BLK - NSA_CMP_LEN) // NSA_CMP_STRIDE + 1
    widths = sorted({min(w, n_cmp) for w in range(_CMP_CHUNK, n_cmp + _CMP_CHUNK, _CMP_CHUNK)})
    case = jnp.minimum((n_visible - 1) // _CMP_CHUNK, len(widths) - 1)
    for idx, width in enumerate(widths):
        pl.when(case == idx)(functools.partial(attend, width))
    for g in range(n_heads):
        o_ref[:, g * HEAD_DIM:(g + 1) * HEAD_DIM] = os_ref[g * BLK:(g + 1) * BLK, :]

    groups = range(n_kv)
    r = NSA_SEL_LEN // NSA_CMP_STRIDE
    ci = lax.broadcasted_iota(jnp.int32, (n_cmp, n_slc), 0)
    cj = lax.broadcasted_iota(jnp.int32, (n_cmp, n_slc), 1)
    pool = jnp.where((ci >= r * cj - 1) & (ci <= r * cj + r - 1), 1.0, 0.0).astype(BF16)
    parts = [_split3(imp_ref[kv]) for kv in groups]
    imp = [_dot(hi, pool) + (_dot(mid, pool) + _dot(lo, pool)) for hi, mid, lo in parts]

    blk = lax.broadcasted_iota(jnp.int32, (BLK, n_slc), 1)
    cur = (i * BLK + lax.broadcasted_iota(jnp.int32, (BLK, n_slc), 0)) // NSA_SEL_LEN
    valid = blk <= cur
    forced = (blk == 0) | (blk == cur) | (blk == cur - 1)
    bonus = jnp.where(forced, NSA_FORCE, 0.0)
    score = [jnp.where(valid, x + bonus, NEG) for x in imp]
    blk_f = blk.astype(F32)
    chosen = [jnp.zeros((BLK, n_slc), F32) for _ in groups]
    for _ in range(min(NSA_TOP_N, n_slc)):
        best = [jnp.max(x, axis=-1, keepdims=True) for x in score]
        first = [jnp.min(jnp.where(x == b, blk_f, float(n_slc)), axis=-1, keepdims=True)
                 for x, b in zip(score, best)]
        pick = [blk_f == f for f in first]
        chosen = [jnp.where(p, 1.0, c) for p, c in zip(pick, chosen)]
        score = [jnp.where(p, -jnp.inf, x) for p, x in zip(pick, score)]
    chosen = [jnp.where(valid, c, 0.0) for c in chosen]
    for kv in groups:
        sel_ref[kv] = chosen[kv].astype(sel_ref.dtype)

    n_kb = n_slc * NSA_SEL_LEN // BLK
    pi = lax.broadcasted_iota(jnp.int32, (n_slc, n_kb), 0)
    pj = lax.broadcasted_iota(jnp.int32, (n_slc, n_kb), 1)
    pair = jnp.where(pi * NSA_SEL_LEN // BLK == pj, 1.0, 0.0).astype(BF16)
    n_word = need_ref.shape[-1]
    bi = lax.broadcasted_iota(jnp.int32, (n_kb, n_word), 0)
    bw = lax.broadcasted_iota(jnp.int32, (n_kb, n_word), 1)
    weight = jnp.where(bi // 16 == bw, jnp.left_shift(1, bi % 16), 0).astype(F32).astype(BF16)
    for kv in groups:
        any_q = jnp.max(chosen[kv], axis=0, keepdims=True)
        any_q = jnp.broadcast_to(any_q, (8, n_slc)).astype(BF16)
        per_kb = jnp.where(_dot(any_q, pair) > 0.5, 1.0, 0.0).astype(BF16)
        need_ref[kv, 0] = _dot(per_kb, weight).astype(jnp.int32)


def _nsa_compressed(z, k_cmp, v_cmp, slopes):
    s = z.shape[0]
    n_slc = s // NSA_SEL_LEN
    g_per = H_NSA // KV_NSA
    qw = H_NSA * HEAD_DIM
    n_cmp = k_cmp.shape[1]
    n_word = 128
    assert (s // BLK) <= 16 * n_word
    return pl.pallas_call(
        functools.partial(_nsa_cmp_kernel, g_per=g_per, n_slc=n_slc),
        grid=(s // BLK,),
        in_specs=[pl.BlockSpec(memory_space=pltpu.SMEM),
                  pl.BlockSpec((BLK, qw), lambda i: (i, 0)),
                  pl.BlockSpec((KV_NSA, n_cmp, HEAD_DIM), lambda i: (0, 0, 0)),
                  pl.BlockSpec((KV_NSA, n_cmp, HEAD_DIM), lambda i: (0, 0, 0))],
        out_specs=[pl.BlockSpec((BLK, qw), lambda i: (i, 0)),
                   pl.BlockSpec((KV_NSA, BLK, n_slc), lambda i: (0, i, 0)),
                   pl.BlockSpec((KV_NSA, 1, 8, n_word), lambda i: (0, i, 0, 0))],
        out_shape=[jax.ShapeDtypeStruct((s, qw), F32),
                   jax.ShapeDtypeStruct((KV_NSA, s, n_slc), BF16),
                   jax.ShapeDtypeStruct((KV_NSA, s // BLK, 8, n_word), jnp.int32)],
        scratch_shapes=[pltpu.VMEM((KV_NSA, BLK, n_cmp), F32),
                        pltpu.VMEM((H_NSA * BLK, HEAD_DIM), BF16),
                        pltpu.VMEM((H_NSA * BLK, HEAD_DIM), F32)],
        compiler_params=_params("parallel"),
        name="nsa_compressed",
    )(slopes, z, k_cmp, v_cmp)


def _stack_heads(q_ref, qs_ref, g_per):
    for g in range(g_per):
        qs_ref[g * BLK:(g + 1) * BLK, :] = q_ref[:, g * HEAD_DIM:(g + 1) * HEAD_DIM]


_ROW_SPLIT = 2


def _online_visit(i, blocks, qs_ref, k_ref, v_ref, slope_ref, m_ref, l_ref, acc_ref, g_per):
    nb = len(blocks)
    starts = [pl.multiple_of(j * BLK, BLK) for j, _ in blocks]
    k = jnp.concatenate([k_ref[pl.ds(st, BLK), :] for st in starts], axis=0)
    v = jnp.concatenate([v_ref[pl.ds(st, BLK), :] for st in starts], axis=0)
    v_ones = jnp.concatenate([v, jnp.ones((nb * BLK, HEAD_DIM), BF16)], axis=1)
    qq = lax.broadcasted_iota(jnp.int32, (BLK, BLK), 0) - lax.broadcasted_iota(jnp.int32, (BLK, BLK), 1)
    dists = [(i - j) * BLK + qq for j, _ in blocks]
    ok1 = jnp.concatenate([key_ok(d) for (_, key_ok), d in zip(blocks, dists)], axis=1)
    dist1 = jnp.concatenate([d.astype(F32) for d in dists], axis=1)
    g_sub = g_per // _ROW_SPLIT
    ok = jnp.concatenate([ok1] * g_sub, axis=0) > 0.5
    dist_f = jnp.concatenate([dist1] * g_sub, axis=0)

    groups = [slice(r * g_sub * BLK, (r + 1) * g_sub * BLK) for r in range(_ROW_SPLIT)]
    s = [_dot_nt(qs_ref[rows, :], k) for rows in groups]
    s = [jnp.where(ok, x * (SCALE * LOG2E) - jnp.concatenate([slope_ref[0, rows, :]] * nb, axis=1) * dist_f, NEG)
         for x, rows in zip(s, groups)]
    m_prev = [m_ref[rows, :] for rows in groups]
    m_new = [jnp.maximum(mp, jnp.max(x, axis=-1, keepdims=True)) for mp, x in zip(m_prev, s)]
    alpha = [jnp.exp2(mp - mn) for mp, mn in zip(m_prev, m_new)]
    p = [jnp.where(ok, jnp.exp2(x - jnp.concatenate([mn] * nb, axis=1)), 0.0) for x, mn in zip(s, m_new)]
    pv = [_dot(x.astype(BF16), v_ones) for x in p]
    for rows, a, y, mn in zip(groups, alpha, pv, m_new):
        l_ref[rows, :] = a * l_ref[rows, :] + y[:, HEAD_DIM:]
        acc_ref[rows, :] = a * acc_ref[rows, :] + y[:, :HEAD_DIM]
        m_ref[rows, :] = mn


def _online_init(m_ref, l_ref, acc_ref):
    m_ref[...] = jnp.full_like(m_ref, NEG)
    l_ref[...] = jnp.zeros_like(l_ref)
    acc_ref[...] = jnp.zeros_like(acc_ref)


_SEL_GROUP = 8


def _nsa_sel_kernel(need_ref, q_ref, sel_ref, slope_ref, k_ref, v_ref, o_ref,
                    qs_ref, m_ref, l_ref, acc_ref, list_ref, *, g_per, n_qblk, n_word):
    h = pl.program_id(0)
    i = pl.program_id(1)
    _stack_heads(q_ref, qs_ref, g_per)
    _online_init(m_ref, l_ref, acc_ref)
    sel = sel_ref[0]
    n_slc = sel.shape[1]
    per_blk = BLK // NSA_SEL_LEN
    eb = lax.broadcasted_iota(jnp.int32, (n_slc, BLK), 0)
    el = lax.broadcasted_iota(jnp.int32, (n_slc, BLK), 1) // NSA_SEL_LEN
    base = (h * n_qblk + i) * n_word

    def scan(j, n):
        word = need_ref[base + jnp.right_shift(j, 4)]
        needed = jnp.bitwise_and(jnp.right_shift(word, jnp.bitwise_and(j, 15)), 1)
        list_ref[n] = j
        return n + needed

    n_needed = lax.fori_loop(0, i + 1, scan, 0)

    def visit(first, count):
        blocks = []
        for u in range(count):
            j = list_ref[first + u]
            expand = jnp.where(eb == per_blk * j + el, 1.0, 0.0).astype(BF16)
            picked = _dot(sel, expand)
            blocks.append((j, functools.partial(
                lambda dist, picked: jnp.where(dist >= 0, picked, 0.0), picked=picked)))
        _online_visit(i, blocks, qs_ref, k_ref, v_ref, slope_ref, m_ref, l_ref, acc_ref, g_per)

    def group(t, carry):
        visit(t * _SEL_GROUP, _SEL_GROUP)
        return carry

    n_groups = n_needed // _SEL_GROUP
    lax.fori_loop(0, n_groups, group, 0)
    done = n_groups * _SEL_GROUP
    size = _SEL_GROUP // 2
    while size >= 1:
        has = jnp.bitwise_and(n_needed, size) != 0
        pl.when(has)(functools.partial(visit, done, size))
        done = done + jnp.where(has, size, 0)
        size //= 2
    out = acc_ref[...] / jnp.maximum(l_ref[...], 1e-30)
    for g in range(g_per):
        o_ref[:, g * HEAD_DIM:(g + 1) * HEAD_DIM] = out[g * BLK:(g + 1) * BLK, :]


def _nsa_selected(z, sel, need, slope_rows, *, k_col, v_col):
    s = z.shape[0]
    g_per = H_NSA // KV_NSA
    qw = g_per * HEAD_DIM
    rows = g_per * BLK
    n_slc = sel.shape[-1]
    n_qblk = s // BLK
    n_word = -(-n_qblk // 16)
    need = need[:, :, 0, :n_word]
    return pl.pallas_call(
        functools.partial(_nsa_sel_kernel, g_per=g_per, n_qblk=n_qblk, n_word=n_word),
        grid=(KV_NSA, n_qblk),
        in_specs=[pl.BlockSpec(memory_space=pltpu.SMEM),
                  pl.BlockSpec((BLK, qw), lambda h, i: (i, h)),
                  pl.BlockSpec((1, BLK, n_slc), lambda h, i: (h, i, 0)),
                  pl.BlockSpec((1, rows, HEAD_DIM), lambda h, i: (h, 0, 0)),
                  pl.BlockSpec((s, HEAD_DIM), lambda h, i: (0, k_col + h)),
                  pl.BlockSpec((s, HEAD_DIM), lambda h, i: (0, v_col + h))],
        out_specs=pl.BlockSpec((BLK, qw), lambda h, i: (i, h)),
        out_shape=jax.ShapeDtypeStruct((s, H_NSA * HEAD_DIM), F32),
        scratch_shapes=[pltpu.VMEM((rows, HEAD_DIM), BF16),
                        pltpu.VMEM((rows, HEAD_DIM), F32),
                        pltpu.VMEM((rows, HEAD_DIM), F32),
                        pltpu.VMEM((rows, HEAD_DIM), F32),
                        pltpu.SMEM((n_qblk + _SEL_GROUP,), jnp.int32)],
        compiler_params=_params("parallel", "arbitrary"),
        name="nsa_selected",
    )(need.reshape(-1), z, sel, slope_rows, z, z)


def _nsa_win_kernel(q_ref, slope_ref, k_ref, v_ref, gate_ref, ocmp_ref, oslc_ref, o_ref,
                    qs_ref, m_ref, l_ref, acc_ref, *, g_per):
    i = pl.program_id(1)
    _stack_heads(q_ref, qs_ref, g_per)
    _online_init(m_ref, l_ref, acc_ref)
    n_prev = -(-(NSA_WINDOW - 1) // BLK)

    blocks = []
    for step in range(n_prev + 1):
        j = i - n_prev + step
        present = jnp.where(j >= 0, 1.0, 0.0)
        blocks.append((jnp.maximum(j, 0), functools.partial(
            lambda dist, present: jnp.where((dist >= 0) & (dist < NSA_WINDOW), present, 0.0),
            present=present)))
    _online_visit(i, blocks, qs_ref, k_ref, v_ref, slope_ref, m_ref, l_ref, acc_ref, g_per)
    o_win = acc_ref[...] / jnp.maximum(l_ref[...], 1e-30)
    gates = jax.nn.sigmoid(gate_ref[...])
    for g in range(g_per):
        hs = slice(g * HEAD_DIM, (g + 1) * HEAD_DIM)
        o = (gates[:, 3 * g:3 * g + 1] * ocmp_ref[:, hs]
             + gates[:, 3 * g + 1:3 * g + 2] * oslc_ref[:, hs]
             + gates[:, 3 * g + 2:3 * g + 3] * o_win[g * BLK:(g + 1) * BLK, :])
        o_ref[:, hs] = o.astype(o_ref.dtype)


def _nsa_window_combine(z, gate_logits, o_cmp, o_slc, slope_rows, *, k_col, v_col):
    s = z.shape[0]
    g_per = H_NSA // KV_NSA
    qw = g_per * HEAD_DIM
    rows = g_per * BLK
    head_blk = lambda h, i: (i, h)
    return pl.pallas_call(
        functools.partial(_nsa_win_kernel, g_per=g_per),
        grid=(KV_NSA, s // BLK),
        in_specs=[pl.BlockSpec((BLK, qw), head_blk),
                  pl.BlockSpec((1, rows, HEAD_DIM), lambda h, i: (h, 0, 0)),
                  pl.BlockSpec((s, HEAD_DIM), lambda h, i: (0, k_col + h)),
                  pl.BlockSpec((s, HEAD_DIM), lambda h, i: (0, v_col + h)),
                  pl.BlockSpec((BLK, 128), head_blk),
                  pl.BlockSpec((BLK, qw), head_blk),
                  pl.BlockSpec((BLK, qw), head_blk)],
        out_specs=pl.BlockSpec((BLK, qw), head_blk),
        out_shape=jax.ShapeDtypeStruct((s, H_NSA * HEAD_DIM), BF16),
        scratch_shapes=[pltpu.VMEM((rows, HEAD_DIM), BF16),
                        pltpu.VMEM((rows, HEAD_DIM), F32),
                        pltpu.VMEM((rows, HEAD_DIM), F32),
                        pltpu.VMEM((rows, HEAD_DIM), F32)],
        compiler_params=_params("parallel", "arbitrary"),
        name="nsa_window_combine",
    )(z, slope_rows, z, z, gate_logits, o_cmp, o_slc)


def _ab_mixer(h, w_in, sinks, w_out, x_res):
    z = _matmul(h, w_in.astype(BF16), BF16, tm=1024, tn=768)
    d_sb = H_SB * HEAD_DIM
    o_a = _stick_breaking(z, n_heads=H_SB, q_col=0, k_col=H_SB, v_col=2 * H_SB)
    q_b = 3 * d_sb
    k_b = q_b + H_SW * HEAD_DIM
    v_b = k_b + KV_SW * HEAD_DIM
    o_b = _window_sink_attention(z, sinks, q_col=q_b, k_col=k_b, v_col=v_b)
    o = jnp.concatenate([o_a, o_b], axis=-1)
    return _matmul(o, w_out.astype(BF16), F32, tm=1024, tn=512, residual=x_res)


def _nsa_mixer(h, w_in, cmp_pos, wk1, wk2, wv1, wv2, w_out, x_res):
    s = h.shape[0]
    dq = H_NSA * HEAD_DIM
    dkv = KV_NSA * HEAD_DIM
    g_per = H_NSA // KV_NSA
    main = dq + 6 * dkv
    z = _matmul(h, w_in[:, :main].astype(BF16), BF16, tm=1024, tn=512)
    w_gate = w_in[:, main:].reshape(-1, KV_NSA, 3 * g_per)
    w_gate = jnp.pad(w_gate, ((0, 0), (0, 0), (0, 128 - 3 * g_per))).reshape(-1, KV_NSA * 128)
    gate_logits = _matmul(h, w_gate.astype(BF16), F32, tm=1024, tn=KV_NSA * 128)

    n_chunk = s // NSA_CMP_STRIDE
    to_chunks = lambda c: (z[:, c:c + dkv].reshape(n_chunk, NSA_CMP_STRIDE, KV_NSA, HEAD_DIM)
                           .transpose(2, 0, 1, 3).reshape(KV_NSA, n_chunk, NSA_CMP_STRIDE * HEAD_DIM))
    chunks = jnp.concatenate([to_chunks(dq), to_chunks(dq + dkv)], axis=0)
    kv_cmp = _compress(chunks, cmp_pos.reshape(1, -1).astype(F32),
                       jnp.stack([wk1, wv1]).astype(BF16), jnp.stack([wk2, wv2]).astype(BF16))
    k_cmp, v_cmp = kv_cmp[:KV_NSA], kv_cmp[KV_NSA:]

    slopes = _alibi_slopes(H_NSA, KV_NSA) * np.float32(LOG2E)
    slope_rows = jnp.asarray(np.repeat(np.repeat(slopes, BLK, axis=1)[..., None], HEAD_DIM, axis=2))
    o_cmp, sel, need = _nsa_compressed(z, k_cmp, v_cmp, jnp.asarray(slopes.reshape(-1)))
    col = lambda off: (dq + off * dkv) // HEAD_DIM
    o_slc = _nsa_selected(z, sel, need, slope_rows, k_col=col(2), v_col=col(3))
    o = _nsa_window_combine(z, gate_logits, o_cmp, o_slc, slope_rows, k_col=col(4), v_col=col(5))
    return _matmul(o, w_out.astype(BF16), F32, tm=1024, tn=512, residual=x_res)


def _swiglu(h, w_gate_up, w_down, layer, x_res):
    act = _gate_up(h, w_gate_up, layer, tm=2048, tn=256)
    return _matmul(act, w_down, F32, tm=512, tn=512, residual=x_res, b_index=layer)


def kernel(x, attn_norm, ffn_norm, w_gate_up, w_down, ab_w_in, ab_sinks, ab_w_out, nsa_w_in,
           nsa_cmp_pos, nsa_cmp_wk1, nsa_cmp_wk2, nsa_cmp_wv1, nsa_cmp_wv2, nsa_w_out, final_norm):
    b, s, d = x.shape
    w_down = w_down.astype(BF16)
    outs = []
    for bi in range(b):
        xb = x[bi]
        for layer in range(attn_norm.shape[0]):
            h = _rmsnorm(xb, attn_norm[layer], BF16)
            if layer % 2 == 0:
                e = layer // 2
                xb = _ab_mixer(h, ab_w_in[e], ab_sinks[e], ab_w_out[e], xb)
            else:
                o = layer // 2
                xb = _nsa_mixer(h, nsa_w_in[o], nsa_cmp_pos[o], nsa_cmp_wk1[o], nsa_cmp_wk2[o],
                                nsa_cmp_wv1[o], nsa_cmp_wv2[o], nsa_w_out[o], xb)
            h = _rmsnorm(xb, ffn_norm[layer], BF16)
            xb = _swiglu(h, w_gate_up, w_down, layer, xb)
        outs.append(_rmsnorm(xb, final_norm, F32))
    return jnp.stack(outs, axis=0)
```

```python
import functools

import numpy as np
import jax
import jax.numpy as jnp
from jax import lax
from jax.experimental import pallas as pl
from jax.experimental.pallas import tpu as pltpu

F32 = jnp.float32
BF16 = jnp.bfloat16

HEAD_DIM = 128
BLK = 128
EPS = 1e-5
NEG = -1e30
SCALE = HEAD_DIM ** -0.5
LOG2E = 1.4426950408889634

H_SB = 8
H_SW = 24
KV_SW = 3
SW_WINDOW = 128

H_NSA = 32
KV_NSA = 2
NSA_CMP_LEN = 32
NSA_CMP_STRIDE = 16
NSA_SEL_LEN = 64
NSA_TOP_N = 8
NSA_WINDOW = 512
NSA_FORCE = 1e4

VMEM_LIMIT_BYTES = 52 * 2 ** 20
F32_WEIGHT_VMEM_LIMIT_BYTES = 60 * 2 ** 20

_NT = (((1,), (1,)), ((), ()))


def _params(*sem, vmem_limit_bytes=VMEM_LIMIT_BYTES):
    return pltpu.CompilerParams(dimension_semantics=sem, vmem_limit_bytes=vmem_limit_bytes)


def _dot(a, b):
    return jnp.dot(a, b, preferred_element_type=F32)


def _dot_nt(a, b):
    return lax.dot_general(a, b, _NT, preferred_element_type=F32)


def _alibi_slopes(n_heads, n_kv):
    s = np.exp2(np.float32(-8.0) * np.arange(1, n_heads + 1, dtype=np.float32)
                / np.float32(n_heads)).astype(np.float32)
    return s.reshape(n_heads // n_kv, n_kv).T


def _rmsnorm_kernel(x_ref, g_ref, o_ref):
    x = x_ref[...]
    ms = jnp.mean(x * x, axis=-1, keepdims=True)
    o_ref[...] = (x * lax.rsqrt(ms + EPS) * g_ref[...]).astype(o_ref.dtype)


def _rmsnorm(x, g, out_dtype, tm=256):
    m, d = x.shape
    return pl.pallas_call(
        _rmsnorm_kernel,
        grid=(m // tm,),
        in_specs=[pl.BlockSpec((tm, d), lambda i: (i, 0)),
                  pl.BlockSpec((1, d), lambda i: (0, 0))],
        out_specs=pl.BlockSpec((tm, d), lambda i: (i, 0)),
        out_shape=jax.ShapeDtypeStruct((m, d), out_dtype),
        compiler_params=_params("parallel"),
        name="rmsnorm",
    )(x, g.reshape(1, d).astype(F32))


def _mm_kernel(*refs, nk, has_res):
    if has_res:
        a_ref, b_ref, r_ref, o_ref = refs[:4]
        scratch = refs[4:]
    else:
        a_ref, b_ref, o_ref = refs[:3]
        r_ref = None
        scratch = refs[3:]

    def finish(acc):
        if has_res:
            acc = acc + r_ref[...]
        o_ref[...] = acc.astype(o_ref.dtype)

    part = _dot(a_ref[...], b_ref[...].astype(BF16))
    if nk == 1:
        finish(part)
    else:
        acc_ref, = scratch
        k = pl.program_id(2)

        @pl.when(k == 0)
        def _():
            acc_ref[...] = part

        @pl.when(k > 0)
        def _():
            acc_ref[...] += part

        @pl.when(k == nk - 1)
        def _():
            finish(acc_ref[...])


def _matmul(a, b, out_dtype, *, tm, tn, tk=None, residual=None, b_index=None, n=None):
    m, kd = a.shape
    n = b.shape[-1] if n is None else n
    tk = kd if tk is None else tk
    nk = kd // tk
    assert m % tm == 0 and n % tn == 0 and kd % tk == 0
    if b_index is None:
        b_spec = pl.BlockSpec((tk, tn), lambda i, j, k: (k, j))
    else:
        b_spec = pl.BlockSpec((None, tk, tn), lambda i, j, k: (b_index, k, j))
    in_specs = [pl.BlockSpec((tm, tk), lambda i, j, k: (i, k)), b_spec]
    args = [a, b]
    if residual is not None:
        in_specs.append(pl.BlockSpec((tm, tn), lambda i, j, k: (i, j)))
        args.append(residual)
    return pl.pallas_call(
        functools.partial(_mm_kernel, nk=nk, has_res=residual is not None),
        grid=(m // tm, n // tn, nk),
        in_specs=in_specs,
        out_specs=pl.BlockSpec((tm, tn), lambda i, j, k: (i, j)),
        out_shape=jax.ShapeDtypeStruct((m, n), out_dtype),
        scratch_shapes=[pltpu.VMEM((tm, tn), F32)] if nk > 1 else [],
        compiler_params=_params(
            "parallel", "parallel", "arbitrary",
            vmem_limit_bytes=F32_WEIGHT_VMEM_LIMIT_BYTES if b.dtype == F32 else VMEM_LIMIT_BYTES),
        name="matmul",
    )(*args)


def _gate_up_kernel(a_ref, wg_ref, wu_ref, o_ref):
    a = a_ref[...]
    gate = _dot(a, wg_ref[...].astype(BF16))
    up = _dot(a, wu_ref[...].astype(BF16))
    o_ref[...] = (gate * jax.nn.sigmoid(gate) * up).astype(o_ref.dtype)


def _gate_up(a, w_gate_up, layer, *, tm, tn):
    m, kd = a.shape
    f = w_gate_up.shape[-1] // 2
    nf = f // tn
    assert m % tm == 0 and f % tn == 0
    return pl.pallas_call(
        _gate_up_kernel,
        grid=(m // tm, nf),
        in_specs=[pl.BlockSpec((tm, kd), lambda i, j: (i, 0)),
                  pl.BlockSpec((None, kd, tn), lambda i, j: (layer, 0, j)),
                  pl.BlockSpec((None, kd, tn), lambda i, j: (layer, 0, j + nf))],
        out_specs=pl.BlockSpec((tm, tn), lambda i, j: (i, j)),
        out_shape=jax.ShapeDtypeStruct((m, f), BF16),
        compiler_params=_params("parallel", "parallel", vmem_limit_bytes=F32_WEIGHT_VMEM_LIMIT_BYTES),
        name="gate_up",
    )(a, w_gate_up, w_gate_up)


_SB_DEAD = -200.0


def _sb_kernel(q_ref, k_ref, v_ref, o_ref, acc_ref, later_ref, z_ref, w_ref, *, tq, tk):
    qi = pl.program_id(1)
    ratio = tq // tk
    assert ratio % 2 == 0
    q = q_ref[...]
    row = lax.broadcasted_iota(jnp.int32, (tk, tk), 0)
    col = lax.broadcasted_iota(jnp.int32, (tk, tk), 1)
    suffix = jnp.where(row >= col, 1.0, 0.0).astype(BF16)
    acc_ref[...] = jnp.zeros_like(acc_ref)
    later_ref[...] = jnp.zeros_like(later_ref)

    def logits(j, r0=0):
        start = pl.multiple_of(j * tk, tk)
        return _dot_nt(q[r0:], k_ref[pl.ds(start, tk), :]) * (-SCALE * LOG2E)

    def weights(j, nz, masked, r0=0):
        neg_abs = lax.bitcast_convert_type(
            lax.bitcast_convert_type(nz, jnp.uint32) | jnp.uint32(0x80000000), F32)
        a = jnp.minimum(nz, 0.0) - jnp.log2(1.0 + jnp.exp2(neg_abs))
        if masked:
            t_pos = qi * tq + r0 + lax.broadcasted_iota(jnp.int32, nz.shape, 0)
            s_pos = j * tk + lax.broadcasted_iota(jnp.int32, nz.shape, 1)
            before = s_pos < t_pos
            a = jnp.where(before, a, 0.0)
        within = _dot(a.astype(BF16), suffix)
        later = later_ref[r0:, :]
        w = jnp.exp2(within + jnp.concatenate([later] * (tk // HEAD_DIM), axis=1) - nz)
        later_ref[r0:, :] = later + within[:, 0:1]
        if masked:
            w = jnp.where(before, w, 0.0)
        return w.astype(BF16)

    def accumulate(j, w, r0=0):
        start = pl.multiple_of(j * tk, tk)
        acc_ref[r0:, :] += _dot(w, v_ref[pl.ds(start, tk), :])

    n_full = ratio * qi

    for b in reversed(range(ratio)):
        j = n_full + b
        accumulate(j, weights(j, logits(j, b * tk), True, b * tk), b * tk)

    def step(j, slot):
        accumulate(j + 1, w_ref[1 - slot])
        z_ref[1 - slot] = logits(jnp.maximum(j - 1, 0))
        w_ref[slot] = weights(j, z_ref[slot], False)

    def remaining_mass():
        return jnp.max(later_ref[...])

    def two_steps(carry):
        s, _ = carry
        j = n_full - 1 - 2 * s
        step(j, 0)
        step(j - 1, 1)
        return s + 1, remaining_mass()

    def unfinished(carry):
        s, mass = carry
        return jnp.logical_and(s < n_full // 2, mass > _SB_DEAD)

    z_ref[0] = logits(jnp.maximum(n_full - 1, 0))
    w_ref[1] = jnp.zeros((tq, tk), BF16)
    trips, _ = lax.while_loop(unfinished, two_steps, (jnp.int32(0), remaining_mass()))
    accumulate(n_full - 2 * trips, w_ref[1])
    o_ref[...] = acc_ref[...].astype(o_ref.dtype)


def _stick_breaking(z, *, n_heads, q_col, k_col, v_col, tq=1024, tk=256):
    s = z.shape[0]
    return pl.pallas_call(
        functools.partial(_sb_kernel, tq=tq, tk=tk),
        grid=(n_heads, s // tq),
        in_specs=[pl.BlockSpec((tq, HEAD_DIM), lambda h, i: (i, q_col + h)),
                  pl.BlockSpec((s, HEAD_DIM), lambda h, i: (0, k_col + h)),
                  pl.BlockSpec((s, HEAD_DIM), lambda h, i: (0, v_col + h))],
        out_specs=pl.BlockSpec((tq, HEAD_DIM), lambda h, i: (i, h)),
        out_shape=jax.ShapeDtypeStruct((s, n_heads * HEAD_DIM), BF16),
        scratch_shapes=[pltpu.VMEM((tq, HEAD_DIM), F32),
                        pltpu.VMEM((tq, HEAD_DIM), F32),
                        pltpu.VMEM((2, tq, tk), F32),
                        pltpu.VMEM((2, tq, tk), BF16)],
        compiler_params=_params("parallel", "arbitrary"),
        name="stick_breaking",
    )(z, z, z)


def _swa_kernel(sink_ref, q_ref, kp_ref, kc_ref, vp_ref, vc_ref, o_ref, *, slopes):
    i = pl.program_id(0)
    n_kv, g_per = slopes.shape
    qq = lax.broadcasted_iota(jnp.int32, (BLK, 2 * BLK), 0)
    kk = lax.broadcasted_iota(jnp.int32, (BLK, 2 * BLK), 1)
    dist = qq + BLK - kk
    first_key = jnp.where(i > 0, 0, BLK)
    mask = (dist >= 0) & (dist < SW_WINDOW) & (kk >= first_key)
    dist_f = dist.astype(F32)
    for h in range(n_kv):
        cs = slice(h * HEAD_DIM, (h + 1) * HEAD_DIM)
        k = jnp.concatenate([kp_ref[:, cs], kc_ref[:, cs]], axis=0)
        v = jnp.concatenate([vp_ref[:, cs], vc_ref[:, cs]], axis=0)
        heads = [h * g_per + g for g in range(g_per)]
        cols = [slice(head * HEAD_DIM, (head + 1) * HEAD_DIM) for head in heads]
        sinks = [sink_ref[head] for head in heads]
        s = [_dot_nt(q_ref[:, hs], k) for hs in cols]
        s = [jnp.where(mask, x * SCALE - float(slopes[h, g]) * dist_f, NEG) for g, x in enumerate(s)]
        m = [jnp.maximum(jnp.max(x, axis=-1, keepdims=True), sink) for x, sink in zip(s, sinks)]
        e = [jnp.where(mask, jnp.exp(x - mx), 0.0) for x, mx in zip(s, m)]
        p = [x / (jnp.sum(x, axis=-1, keepdims=True) + jnp.exp(sink - mx))
             for x, mx, sink in zip(e, m, sinks)]
        for hs, x in zip(cols, p):
            o_ref[:, hs] = _dot(x.astype(BF16), v).astype(o_ref.dtype)


def _window_sink_attention(z, sinks, *, q_col, k_col, v_col):
    s = z.shape[0]
    qw = H_SW * HEAD_DIM
    kw = KV_SW * HEAD_DIM
    assert q_col % qw == 0 and k_col % kw == 0 and v_col % kw == 0
    qb, kb, vb = q_col // qw, k_col // kw, v_col // kw
    prev = lambda i: jnp.maximum(i - 1, 0)
    return pl.pallas_call(
        functools.partial(_swa_kernel, slopes=_alibi_slopes(H_SW, KV_SW)),
        grid=(s // BLK,),
        in_specs=[pl.BlockSpec(memory_space=pltpu.SMEM),
                  pl.BlockSpec((BLK, qw), lambda i: (i, qb)),
                  pl.BlockSpec((BLK, kw), lambda i: (prev(i), kb)),
                  pl.BlockSpec((BLK, kw), lambda i: (i, kb)),
                  pl.BlockSpec((BLK, kw), lambda i: (prev(i), vb)),
                  pl.BlockSpec((BLK, kw), lambda i: (i, vb))],
        out_specs=pl.BlockSpec((BLK, qw), lambda i: (i, 0)),
        out_shape=jax.ShapeDtypeStruct((s, qw), BF16),
        compiler_params=_params("parallel"),
        name="window_sink_attention",
    )(sinks.astype(F32), z, z, z, z, z)


def _compress_kernel(x_ref, pos_ref, w1_ref, w2_ref, o_ref):
    x = x_ref[0].astype(F32)
    half = x.shape[1]
    n_chunk = x.shape[0]
    lo = _dot((x + pos_ref[:, :half]).astype(BF16), w1_ref[0, :half, :])
    hi = _dot((x + pos_ref[:, half:]).astype(BF16), w1_ref[0, half:, :])
    pre = lo + pltpu.roll(hi, n_chunk - 1, 0)
    act = jax.nn.gelu(pre, approximate=True)
    out = _dot(act.astype(BF16), w2_ref[0])
    last = lax.broadcasted_iota(jnp.int32, out.shape, 0) == n_chunk - 1
    o_ref[0] = jnp.where(last, 0.0, out).astype(o_ref.dtype)


def _compress(x, pos, w1, w2):
    n, n_chunk, width = x.shape
    hid = w1.shape[-1]
    return pl.pallas_call(
        _compress_kernel,
        grid=(n,),
        in_specs=[pl.BlockSpec((1, n_chunk, width), lambda i: (i, 0, 0)),
                  pl.BlockSpec((1, 2 * width), lambda i: (0, 0)),
                  pl.BlockSpec((1, 2 * width, hid), lambda i: (i // KV_NSA, 0, 0)),
                  pl.BlockSpec((1, hid, HEAD_DIM), lambda i: (i // KV_NSA, 0, 0))],
        out_specs=pl.BlockSpec((1, n_chunk, HEAD_DIM), lambda i: (i, 0, 0)),
        out_shape=jax.ShapeDtypeStruct((n, n_chunk, HEAD_DIM), BF16),
        compiler_params=_params("parallel"),
        name="nsa_compress",
    )(x, pos, w1, w2)


_CMP_CHUNK = 256
_CMP_UNROLL = 8


def _split3(x):
    hi = x.astype(BF16)
    r1 = x - hi.astype(F32)
    mid = r1.astype(BF16)
    lo = (r1 - mid.astype(F32)).astype(BF16)
    return hi, mid, lo


def _nsa_cmp_kernel(slope_ref, q_ref, kc_ref, vc_ref, o_ref, sel_ref, need_ref,
                    imp_ref, qs_ref, os_ref, *, g_per, n_slc):
    i = pl.program_id(0)
    n_kv, n_cmp = kc_ref.shape[0], kc_ref.shape[1]
    n_heads = n_kv * g_per
    assert g_per % _CMP_UNROLL == 0
    _stack_heads(q_ref, qs_ref, n_heads)
    imp_ref[...] = jnp.zeros_like(imp_ref)

    def attend(width):
        t = i * BLK + lax.broadcasted_iota(jnp.int32, (BLK, width), 0)
        c_end = (lax.broadcasted_iota(jnp.int32, (BLK, width), 1) * NSA_CMP_STRIDE
                 + (NSA_CMP_LEN - 1))
        dist = t - c_end
        mask = dist >= 0
        dist_f = dist.astype(F32)

        def group(gg, carry):
            kv = gg // (g_per // _CMP_UNROLL)
            kc = kc_ref[kv, :width, :]
            vc = vc_ref[kv, :width, :]
            heads = [gg * _CMP_UNROLL + u for u in range(_CMP_UNROLL)]
            rows = [pl.ds(pl.multiple_of(g * BLK, BLK), BLK) for g in heads]
            s = [_dot_nt(qs_ref[r, :], kc) for r in rows]
            s = [jnp.where(mask, x * (SCALE * LOG2E) - slope_ref[g] * dist_f, NEG)
                 for x, g in zip(s, heads)]
            m = [jnp.max(x, axis=-1, keepdims=True) for x in s]
            e = [jnp.where(mask, jnp.exp2(x - mx), 0.0) for x, mx in zip(s, m)]
            p = [x / jnp.maximum(jnp.sum(x, axis=-1, keepdims=True), 1e-30) for x in e]
            for r, x in zip(rows, p):
                os_ref[r, :] = _dot(x.astype(BF16), vc)
            total = p[0]
            for x in p[1:]:
                total = total + x
            imp_ref[kv, :, :width] += total
            return carry

        lax.fori_loop(0, n_heads // _CMP_UNROLL, group, 0)

    n_visible = (BLK * i + BLK - NSA_CMP_LEN) // NSA_CMP_STRIDE + 1
    widths = sorted({min(w, n_cmp) for w in range(_CMP_CHUNK, n_cmp + _CMP_CHUNK, _CMP_CHUNK)})
    case = jnp.minimum((n_visible - 1) // _CMP_CHUNK, len(widths) - 1)
    for idx, width in enumerate(widths):
        pl.when(case == idx)(functools.partial(attend, width))
    for g in range(n_heads):
        o_ref[:, g * HEAD_DIM:(g + 1) * HEAD_DIM] = os_ref[g * BLK:(g + 1) * BLK, :]

    groups = range(n_kv)
    r = NSA_SEL_LEN // NSA_CMP_STRIDE
    ci = lax.broadcasted_iota(jnp.int32, (n_cmp, n_slc), 0)
    cj = lax.broadcasted_iota(jnp.int32, (n_cmp, n_slc), 1)
    pool = jnp.where((ci >= r * cj - 1) & (ci <= r * cj + r - 1), 1.0, 0.0).astype(BF16)
    parts = [_split3(imp_ref[kv]) for kv in groups]
    imp = [_dot(hi, pool) + (_dot(mid, pool) + _dot(lo, pool)) for hi, mid, lo in parts]

    blk = lax.broadcasted_iota(jnp.int32, (BLK, n_slc), 1)
    cur = (i * BLK + lax.broadcasted_iota(jnp.int32, (BLK, n_slc), 0)) // NSA_SEL_LEN
    valid = blk <= cur
    forced = (blk == 0) | (blk == cur) | (blk == cur - 1)
    bonus = jnp.where(forced, NSA_FORCE, 0.0)
    score = [jnp.where(valid, x + bonus, NEG) for x in imp]
    blk_f = blk.astype(F32)
    chosen = [jnp.zeros((BLK, n_slc), F32) for _ in groups]
    for _ in range(min(NSA_TOP_N, n_slc)):
        best = [jnp.max(x, axis=-1, keepdims=True) for x in score]
        first = [jnp.min(jnp.where(x == b, blk_f, float(n_slc)), axis=-1, keepdims=True)
                 for x, b in zip(score, best)]
        pick = [blk_f == f for f in first]
        chosen = [jnp.where(p, 1.0, c) for p, c in zip(pick, chosen)]
        score = [jnp.where(p, -jnp.inf, x) for p, x in zip(pick, score)]
    chosen = [jnp.where(valid, c, 0.0) for c in chosen]
    for kv in groups:
        sel_ref[kv] = chosen[kv].astype(sel_ref.dtype)

    n_kb = n_slc * NSA_SEL_LEN // BLK
    pi = lax.broadcasted_iota(jnp.int32, (n_slc, n_kb), 0)
    pj = lax.broadcasted_iota(jnp.int32, (n_slc, n_kb), 1)
    pair = jnp.where(pi * NSA_SEL_LEN // BLK == pj, 1.0, 0.0).astype(BF16)
    n_word = need_ref.shape[-1]
    bi = lax.broadcasted_iota(jnp.int32, (n_kb, n_word), 0)
    bw = lax.broadcasted_iota(jnp.int32, (n_kb, n_word), 1)
    weight = jnp.where(bi // 16 == bw, jnp.left_shift(1, bi % 16), 0).astype(F32).astype(BF16)
    for kv in groups:
        any_q = jnp.max(chosen[kv], axis=0, keepdims=True)
        any_q = jnp.broadcast_to(any_q, (8, n_slc)).astype(BF16)
        per_kb = jnp.where(_dot(any_q, pair) > 0.5, 1.0, 0.0).astype(BF16)
        need_ref[kv, 0] = _dot(per_kb, weight).astype(jnp.int32)


def _nsa_compressed(z, k_cmp, v_cmp, slopes):
    s = z.shape[0]
    n_slc = s // NSA_SEL_LEN
    g_per = H_NSA // KV_NSA
    qw = H_NSA * HEAD_DIM
    n_cmp = k_cmp.shape[1]
    n_word = 128
    assert (s // BLK) <= 16 * n_word
    return pl.pallas_call(
        functools.partial(_nsa_cmp_kernel, g_per=g_per, n_slc=n_slc),
        grid=(s // BLK,),
        in_specs=[pl.BlockSpec(memory_space=pltpu.SMEM),
                  pl.BlockSpec((BLK, qw), lambda i: (i, 0)),
                  pl.BlockSpec((KV_NSA, n_cmp, HEAD_DIM), lambda i: (0, 0, 0)),
                  pl.BlockSpec((KV_NSA, n_cmp, HEAD_DIM), lambda i: (0, 0, 0))],
        out_specs=[pl.BlockSpec((BLK, qw), lambda i: (i, 0)),
                   pl.BlockSpec((KV_NSA, BLK, n_slc), lambda i: (0, i, 0)),
                   pl.BlockSpec((KV_NSA, 1, 8, n_word), lambda i: (0, i, 0, 0))],
        out_shape=[jax.ShapeDtypeStruct((s, qw), F32),
                   jax.ShapeDtypeStruct((KV_NSA, s, n_slc), BF16),
                   jax.ShapeDtypeStruct((KV_NSA, s // BLK, 8, n_word), jnp.int32)],
        scratch_shapes=[pltpu.VMEM((KV_NSA, BLK, n_cmp), F32),
                        pltpu.VMEM((H_NSA * BLK, HEAD_DIM), BF16),
                        pltpu.VMEM((H_NSA * BLK, HEAD_DIM), F32)],
        compiler_params=_params("parallel"),
        name="nsa_compressed",
    )(slopes, z, k_cmp, v_cmp)


def _stack_heads(q_ref, qs_ref, g_per):
    for g in range(g_per):
        qs_ref[g * BLK:(g + 1) * BLK, :] = q_ref[:, g * HEAD_DIM:(g + 1) * HEAD_DIM]


_ROW_SPLIT = 2


def _online_visit(i, blocks, qs_ref, k_ref, v_ref, slope_ref, m_ref, l_ref, acc_ref, g_per):
    nb = len(blocks)
    starts = [pl.multiple_of(j * BLK, BLK) for j, _ in blocks]
    k = jnp.concatenate([k_ref[pl.ds(st, BLK), :] for st in starts], axis=0)
    v = jnp.concatenate([v_ref[pl.ds(st, BLK), :] for st in starts], axis=0)
    v_ones = jnp.concatenate([v, jnp.ones((nb * BLK, HEAD_DIM), BF16)], axis=1)
    qq = lax.broadcasted_iota(jnp.int32, (BLK, BLK), 0) - lax.broadcasted_iota(jnp.int32, (BLK, BLK), 1)
    dists = [(i - j) * BLK + qq for j, _ in blocks]
    ok1 = jnp.concatenate([key_ok(d) for (_, key_ok), d in zip(blocks, dists)], axis=1)
    dist1 = jnp.concatenate([d.astype(F32) for d in dists], axis=1)
    g_sub = g_per // _ROW_SPLIT
    ok = jnp.concatenate([ok1] * g_sub, axis=0) > 0.5
    dist_f = jnp.concatenate([dist1] * g_sub, axis=0)

    groups = [slice(r * g_sub * BLK, (r + 1) * g_sub * BLK) for r in range(_ROW_SPLIT)]
    s = [_dot_nt(qs_ref[rows, :], k) for rows in groups]
    s = [jnp.where(ok, x * (SCALE * LOG2E) - jnp.concatenate([slope_ref[0, rows, :]] * nb, axis=1) * dist_f, NEG)
         for x, rows in zip(s, groups)]
    m_prev = [m_ref[rows, :] for rows in groups]
    m_new = [jnp.maximum(mp, jnp.max(x, axis=-1, keepdims=True)) for mp, x in zip(m_prev, s)]
    alpha = [jnp.exp2(mp - mn) for mp, mn in zip(m_prev, m_new)]
    p = [jnp.where(ok, jnp.exp2(x - jnp.concatenate([mn] * nb, axis=1)), 0.0) for x, mn in zip(s, m_new)]
    pv = [_dot(x.astype(BF16), v_ones) for x in p]
    for rows, a, y, mn in zip(groups, alpha, pv, m_new):
        l_ref[rows, :] = a * l_ref[rows, :] + y[:, HEAD_DIM:]
        acc_ref[rows, :] = a * acc_ref[rows, :] + y[:, :HEAD_DIM]
        m_ref[rows, :] = mn


def _online_init(m_ref, l_ref, acc_ref):
    m_ref[...] = jnp.full_like(m_ref, NEG)
    l_ref[...] = jnp.zeros_like(l_ref)
    acc_ref[...] = jnp.zeros_like(acc_ref)


_SEL_GROUP = 8


def _nsa_sel_kernel(need_ref, q_ref, sel_ref, slope_ref, k_ref, v_ref, o_ref,
                    qs_ref, m_ref, l_ref, acc_ref, list_ref, *, g_per, n_qblk, n_word):
    h = pl.program_id(0)
    i = pl.program_id(1)
    _stack_heads(q_ref, qs_ref, g_per)
    _online_init(m_ref, l_ref, acc_ref)
    sel = sel_ref[0]
    n_slc = sel.shape[1]
    per_blk = BLK // NSA_SEL_LEN
    eb = lax.broadcasted_iota(jnp.int32, (n_slc, BLK), 0)
    el = lax.broadcasted_iota(jnp.int32, (n_slc, BLK), 1) // NSA_SEL_LEN
    base = (h * n_qblk + i) * n_word

    def scan(j, n):
        word = need_ref[base + jnp.right_shift(j, 4)]
        needed = jnp.bitwise_and(jnp.right_shift(word, jnp.bitwise_and(j, 15)), 1)
        list_ref[n] = j
        return n + needed

    n_needed = lax.fori_loop(0, i + 1, scan, 0)

    def visit(first, count):
        blocks = []
        for u in range(count):
            j = list_ref[first + u]
            expand = jnp.where(eb == per_blk * j + el, 1.0, 0.0).astype(BF16)
            picked = _dot(sel, expand)
            blocks.append((j, functools.partial(
                lambda dist, picked: jnp.where(dist >= 0, picked, 0.0), picked=picked)))
        _online_visit(i, blocks, qs_ref, k_ref, v_ref, slope_ref, m_ref, l_ref, acc_ref, g_per)

    def group(t, carry):
        visit(t * _SEL_GROUP, _SEL_GROUP)
        return carry

    n_groups = n_needed // _SEL_GROUP
    lax.fori_loop(0, n_groups, group, 0)
    done = n_groups * _SEL_GROUP
    size = _SEL_GROUP // 2
    while size >= 1:
        has = jnp.bitwise_and(n_needed, size) != 0
        pl.when(has)(functools.partial(visit, done, size))
        done = done + jnp.where(has, size, 0)
        size //= 2
    out = acc_ref[...] / jnp.maximum(l_ref[...], 1e-30)
    for g in range(g_per):
        o_ref[:, g * HEAD_DIM:(g + 1) * HEAD_DIM] = out[g * BLK:(g + 1) * BLK, :]


def _nsa_selected(z, sel, need, slope_rows, *, k_col, v_col):
    s = z.shape[0]
    g_per = H_NSA // KV_NSA
    qw = g_per * HEAD_DIM
    rows = g_per * BLK
    n_slc = sel.shape[-1]
    n_qblk = s // BLK
    n_word = -(-n_qblk // 16)
    need = need[:, :, 0, :n_word]
    return pl.pallas_call(
        functools.partial(_nsa_sel_kernel, g_per=g_per, n_qblk=n_qblk, n_word=n_word),
        grid=(KV_NSA, n_qblk),
        in_specs=[pl.BlockSpec(memory_space=pltpu.SMEM),
                  pl.BlockSpec((BLK, qw), lambda h, i: (i, h)),
                  pl.BlockSpec((1, BLK, n_slc), lambda h, i: (h, i, 0)),
                  pl.BlockSpec((1, rows, HEAD_DIM), lambda h, i: (h, 0, 0)),
                  pl.BlockSpec((s, HEAD_DIM), lambda h, i: (0, k_col + h)),
                  pl.BlockSpec((s, HEAD_DIM), lambda h, i: (0, v_col + h))],
        out_specs=pl.BlockSpec((BLK, qw), lambda h, i: (i, h)),
        out_shape=jax.ShapeDtypeStruct((s, H_NSA * HEAD_DIM), F32),
        scratch_shapes=[pltpu.VMEM((rows, HEAD_DIM), BF16),
                        pltpu.VMEM((rows, HEAD_DIM), F32),
                        pltpu.VMEM((rows, HEAD_DIM), F32),
                        pltpu.VMEM((rows, HEAD_DIM), F32),
                        pltpu.SMEM((n_qblk + _SEL_GROUP,), jnp.int32)],
        compiler_params=_params("parallel", "arbitrary"),
        name="nsa_selected",
    )(need.reshape(-1), z, sel, slope_rows, z, z)


def _nsa_win_kernel(q_ref, slope_ref, k_ref, v_ref, gate_ref, ocmp_ref, oslc_ref, o_ref,
                    qs_ref, m_ref, l_ref, acc_ref, *, g_per):
    i = pl.program_id(1)
    _stack_heads(q_ref, qs_ref, g_per)
    _online_init(m_ref, l_ref, acc_ref)
    n_prev = -(-(NSA_WINDOW - 1) // BLK)

    blocks = []
    for step in range(n_prev + 1):
        j = i - n_prev + step
        present = jnp.where(j >= 0, 1.0, 0.0)
        blocks.append((jnp.maximum(j, 0), functools.partial(
            lambda dist, present: jnp.where((dist >= 0) & (dist < NSA_WINDOW), present, 0.0),
            present=present)))
    _online_visit(i, blocks, qs_ref, k_ref, v_ref, slope_ref, m_ref, l_ref, acc_ref, g_per)
    o_win = acc_ref[...] / jnp.maximum(l_ref[...], 1e-30)
    gates = jax.nn.sigmoid(gate_ref[...])
    for g in range(g_per):
        hs = slice(g * HEAD_DIM, (g + 1) * HEAD_DIM)
        o = (gates[:, 3 * g:3 * g + 1] * ocmp_ref[:, hs]
             + gates[:, 3 * g + 1:3 * g + 2] * oslc_ref[:, hs]
             + gates[:, 3 * g + 2:3 * g + 3] * o_win[g * BLK:(g + 1) * BLK, :])
        o_ref[:, hs] = o.astype(o_ref.dtype)


def _nsa_window_combine(z, gate_logits, o_cmp, o_slc, slope_rows, *, k_col, v_col):
    s = z.shape[0]
    g_per = H_NSA // KV_NSA
    qw = g_per * HEAD_DIM
    rows = g_per * BLK
    head_blk = lambda h, i: (i, h)
    return pl.pallas_call(
        functools.partial(_nsa_win_kernel, g_per=g_per),
        grid=(KV_NSA, s // BLK),
        in_specs=[pl.BlockSpec((BLK, qw), head_blk),
                  pl.BlockSpec((1, rows, HEAD_DIM), lambda h, i: (h, 0, 0)),
                  pl.BlockSpec((s, HEAD_DIM), lambda h, i: (0, k_col + h)),
                  pl.BlockSpec((s, HEAD_DIM), lambda h, i: (0, v_col + h)),
                  pl.BlockSpec((BLK, 128), head_blk),
                  pl.BlockSpec((BLK, qw), head_blk),
                  pl.BlockSpec((BLK, qw), head_blk)],
        out_specs=pl.BlockSpec((BLK, qw), head_blk),
        out_shape=jax.ShapeDtypeStruct((s, H_NSA * HEAD_DIM), BF16),
        scratch_shapes=[pltpu.VMEM((rows, HEAD_DIM), BF16),
                        pltpu.VMEM((rows, HEAD_DIM), F32),
                        pltpu.VMEM((rows, HEAD_DIM), F32),
                        pltpu.VMEM((rows, HEAD_DIM), F32)],
        compiler_params=_params("parallel", "arbitrary"),
        name="nsa_window_combine",
    )(z, slope_rows, z, z, gate_logits, o_cmp, o_slc)


def _ab_mixer(h, w_in, sinks, w_out, x_res):
    z = _matmul(h, w_in, BF16, tm=1024, tn=768)
    d_sb = H_SB * HEAD_DIM
    o_a = _stick_breaking(z, n_heads=H_SB, q_col=0, k_col=H_SB, v_col=2 * H_SB)
    q_b = 3 * d_sb
    k_b = q_b + H_SW * HEAD_DIM
    v_b = k_b + KV_SW * HEAD_DIM
    o_b = _window_sink_attention(z, sinks, q_col=q_b, k_col=k_b, v_col=v_b)
    o = jnp.concatenate([o_a, o_b], axis=-1)
    return _matmul(o, w_out, F32, tm=1024, tn=512, residual=x_res)


def _nsa_mixer(h, w_in, cmp_pos, wk1, wk2, wv1, wv2, w_out, x_res):
    s = h.shape[0]
    dq = H_NSA * HEAD_DIM
    dkv = KV_NSA * HEAD_DIM
    g_per = H_NSA // KV_NSA
    main = dq + 6 * dkv
    z = _matmul(h, w_in, BF16, tm=1024, tn=512, n=main)
    w_gate = w_in[:, main:].reshape(-1, KV_NSA, 3 * g_per)
    w_gate = jnp.pad(w_gate, ((0, 0), (0, 0), (0, 128 - 3 * g_per))).reshape(-1, KV_NSA * 128)
    gate_logits = _matmul(h, w_gate.astype(BF16), F32, tm=1024, tn=KV_NSA * 128)

    n_chunk = s // NSA_CMP_STRIDE
    to_chunks = lambda c: (z[:, c:c + dkv].reshape(n_chunk, NSA_CMP_STRIDE, KV_NSA, HEAD_DIM)
                           .transpose(2, 0, 1, 3).reshape(KV_NSA, n_chunk, NSA_CMP_STRIDE * HEAD_DIM))
    chunks = jnp.concatenate([to_chunks(dq), to_chunks(dq + dkv)], axis=0)
    kv_cmp = _compress(chunks, cmp_pos.reshape(1, -1).astype(F32),
                       jnp.stack([wk1, wv1]).astype(BF16), jnp.stack([wk2, wv2]).astype(BF16))
    k_cmp, v_cmp = kv_cmp[:KV_NSA], kv_cmp[KV_NSA:]

    slopes = _alibi_slopes(H_NSA, KV_NSA) * np.float32(LOG2E)
    slope_rows = jnp.asarray(np.repeat(np.repeat(slopes, BLK, axis=1)[..., None], HEAD_DIM, axis=2))
    o_cmp, sel, need = _nsa_compressed(z, k_cmp, v_cmp, jnp.asarray(slopes.reshape(-1)))
    col = lambda off: (dq + off * dkv) // HEAD_DIM
    o_slc = _nsa_selected(z, sel, need, slope_rows, k_col=col(2), v_col=col(3))
    o = _nsa_window_combine(z, gate_logits, o_cmp, o_slc, slope_rows, k_col=col(4), v_col=col(5))
    return _matmul(o, w_out, F32, tm=1024, tn=512, residual=x_res)


def _swiglu(h, w_gate_up, w_down, layer, x_res):
    act = _gate_up(h, w_gate_up, layer, tm=2048, tn=256)
    return _matmul(act, w_down, F32, tm=512, tn=512, residual=x_res, b_index=layer)


def kernel(x, attn_norm, ffn_norm, w_gate_up, w_down, ab_w_in, ab_sinks, ab_w_out, nsa_w_in,
           nsa_cmp_pos, nsa_cmp_wk1, nsa_cmp_wk2, nsa_cmp_wv1, nsa_cmp_wv2, nsa_w_out, final_norm):
    b, s, d = x.shape
    w_down = w_down.astype(BF16)
    outs = []
    for bi in range(b):
        xb = x[bi]
        for layer in range(attn_norm.shape[0]):
            h = _rmsnorm(xb, attn_norm[layer], BF16)
            if layer % 2 == 0:
                e = layer // 2
                xb = _ab_mixer(h, ab_w_in[e], ab_sinks[e], ab_w_out[e], xb)
            else:
                o = layer // 2
                xb = _nsa_mixer(h, nsa_w_in[o], nsa_cmp_pos[o], nsa_cmp_wk1[o], nsa_cmp_wk2[o],
                                nsa_cmp_wv1[o], nsa_cmp_wv2[o], nsa_w_out[o], xb)
            h = _rmsnorm(xb, ffn_norm[layer], BF16)
            xb = _swiglu(h, w_gate_up, w_down, layer, xb)
        outs.append(_rmsnorm(xb, final_norm, F32))
    return jnp.stack(outs, axis=0)
```

```python
import functools

import numpy as np
import jax
import jax.numpy as jnp
from jax import lax
from jax.experimental import pallas as pl
from jax.experimental.pallas import tpu as pltpu

F32 = jnp.float32
BF16 = jnp.bfloat16

HEAD_DIM = 128
BLK = 128
EPS = 1e-5
NEG = -1e30
SCALE = HEAD_DIM ** -0.5
LOG2E = 1.4426950408889634

H_SB = 8
H_SW = 24
KV_SW = 3
SW_WINDOW = 128

H_NSA = 32
KV_NSA = 2
NSA_CMP_LEN = 32
NSA_CMP_STRIDE = 16
NSA_SEL_LEN = 64
NSA_TOP_N = 8
NSA_WINDOW = 512
NSA_FORCE = 1e4

VMEM_LIMIT_BYTES = 52 * 2 ** 20
GATE_UP_VMEM_LIMIT_BYTES = 60 * 2 ** 20

_NT = (((1,), (1,)), ((), ()))


def _params(*sem, vmem_limit_bytes=VMEM_LIMIT_BYTES):
    return pltpu.CompilerParams(dimension_semantics=sem, vmem_limit_bytes=vmem_limit_bytes)


def _dot(a, b):
    return jnp.dot(a, b, preferred_element_type=F32)


def _dot_nt(a, b):
    return lax.dot_general(a, b, _NT, preferred_element_type=F32)


def _alibi_slopes(n_heads, n_kv):
    s = np.exp2(np.float32(-8.0) * np.arange(1, n_heads + 1, dtype=np.float32)
                / np.float32(n_heads)).astype(np.float32)
    return s.reshape(n_heads // n_kv, n_kv).T


def _rmsnorm_kernel(x_ref, g_ref, o_ref):
    x = x_ref[...]
    ms = jnp.mean(x * x, axis=-1, keepdims=True)
    o_ref[...] = (x * lax.rsqrt(ms + EPS) * g_ref[...]).astype(o_ref.dtype)


def _rmsnorm(x, g, out_dtype, tm=512):
    m, d = x.shape
    return pl.pallas_call(
        _rmsnorm_kernel,
        grid=(m // tm,),
        in_specs=[pl.BlockSpec((tm, d), lambda i: (i, 0)),
                  pl.BlockSpec((1, d), lambda i: (0, 0))],
        out_specs=pl.BlockSpec((tm, d), lambda i: (i, 0)),
        out_shape=jax.ShapeDtypeStruct((m, d), out_dtype),
        compiler_params=_params("parallel"),
        name="rmsnorm",
    )(x, g.reshape(1, d).astype(F32))


def _mm_kernel(*refs, nk, has_res):
    if has_res:
        a_ref, b_ref, r_ref, o_ref = refs[:4]
        scratch = refs[4:]
    else:
        a_ref, b_ref, o_ref = refs[:3]
        r_ref = None
        scratch = refs[3:]

    def finish(acc):
        if has_res:
            acc = acc + r_ref[...]
        o_ref[...] = acc.astype(o_ref.dtype)

    part = _dot(a_ref[...], b_ref[...])
    if nk == 1:
        finish(part)
    else:
        acc_ref, = scratch
        k = pl.program_id(2)

        @pl.when(k == 0)
        def _():
            acc_ref[...] = part

        @pl.when(k > 0)
        def _():
            acc_ref[...] += part

        @pl.when(k == nk - 1)
        def _():
            finish(acc_ref[...])


def _matmul(a, b, out_dtype, *, tm, tn, tk=None, residual=None, b_index=None):
    m, kd = a.shape
    n = b.shape[-1]
    tk = kd if tk is None else tk
    nk = kd // tk
    assert m % tm == 0 and n % tn == 0 and kd % tk == 0
    if b_index is None:
        b_spec = pl.BlockSpec((tk, tn), lambda i, j, k: (k, j))
    else:
        b_spec = pl.BlockSpec((None, tk, tn), lambda i, j, k: (b_index, k, j))
    in_specs = [pl.BlockSpec((tm, tk), lambda i, j, k: (i, k)), b_spec]
    args = [a, b]
    if residual is not None:
        in_specs.append(pl.BlockSpec((tm, tn), lambda i, j, k: (i, j)))
        args.append(residual)
    return pl.pallas_call(
        functools.partial(_mm_kernel, nk=nk, has_res=residual is not None),
        grid=(m // tm, n // tn, nk),
        in_specs=in_specs,
        out_specs=pl.BlockSpec((tm, tn), lambda i, j, k: (i, j)),
        out_shape=jax.ShapeDtypeStruct((m, n), out_dtype),
        scratch_shapes=[pltpu.VMEM((tm, tn), F32)] if nk > 1 else [],
        compiler_params=_params("parallel", "parallel", "arbitrary"),
        name="matmul",
    )(*args)


def _gate_up_kernel(a_ref, wg_ref, wu_ref, o_ref):
    a = a_ref[...]
    gate = _dot(a, wg_ref[...].astype(BF16))
    up = _dot(a, wu_ref[...].astype(BF16))
    o_ref[...] = (gate * jax.nn.sigmoid(gate) * up).astype(o_ref.dtype)


def _gate_up(a, w_gate_up, layer, *, tm, tn):
    m, kd = a.shape
    f = w_gate_up.shape[-1] // 2
    nf = f // tn
    assert m % tm == 0 and f % tn == 0
    return pl.pallas_call(
        _gate_up_kernel,
        grid=(m // tm, nf),
        in_specs=[pl.BlockSpec((tm, kd), lambda i, j: (i, 0)),
                  pl.BlockSpec((None, kd, tn), lambda i, j: (layer, 0, j)),
                  pl.BlockSpec((None, kd, tn), lambda i, j: (layer, 0, j + nf))],
        out_specs=pl.BlockSpec((tm, tn), lambda i, j: (i, j)),
        out_shape=jax.ShapeDtypeStruct((m, f), BF16),
        compiler_params=_params("parallel", "parallel", vmem_limit_bytes=GATE_UP_VMEM_LIMIT_BYTES),
        name="gate_up",
    )(a, w_gate_up, w_gate_up)


_SB_DEAD = -200.0


def _sb_kernel(q_ref, k_ref, v_ref, o_ref, acc_ref, later_ref, z_ref, w_ref, *, tq, tk):
    qi = pl.program_id(1)
    ratio = tq // tk
    assert ratio % 2 == 0
    q = q_ref[...]
    row = lax.broadcasted_iota(jnp.int32, (tk, tk), 0)
    col = lax.broadcasted_iota(jnp.int32, (tk, tk), 1)
    suffix = jnp.where(row >= col, 1.0, 0.0).astype(BF16)
    acc_ref[...] = jnp.zeros_like(acc_ref)
    later_ref[...] = jnp.zeros_like(later_ref)

    def logits(j, r0=0):
        start = pl.multiple_of(j * tk, tk)
        return _dot_nt(q[r0:], k_ref[pl.ds(start, tk), :]) * (-SCALE * LOG2E)

    def weights(j, nz, masked, r0=0):
        neg_abs = lax.bitcast_convert_type(
            lax.bitcast_convert_type(nz, jnp.uint32) | jnp.uint32(0x80000000), F32)
        a = jnp.minimum(nz, 0.0) - jnp.log2(1.0 + jnp.exp2(neg_abs))
        if masked:
            t_pos = qi * tq + r0 + lax.broadcasted_iota(jnp.int32, nz.shape, 0)
            s_pos = j * tk + lax.broadcasted_iota(jnp.int32, nz.shape, 1)
            before = s_pos < t_pos
            a = jnp.where(before, a, 0.0)
        within = _dot(a.astype(BF16), suffix)
        later = later_ref[r0:, :]
        w = jnp.exp2(within + jnp.concatenate([later] * (tk // HEAD_DIM), axis=1) - nz)
        later_ref[r0:, :] = later + within[:, 0:1]
        if masked:
            w = jnp.where(before, w, 0.0)
        return w.astype(BF16)

    def accumulate(j, w, r0=0):
        start = pl.multiple_of(j * tk, tk)
        acc_ref[r0:, :] += _dot(w, v_ref[pl.ds(start, tk), :])

    n_full = ratio * qi

    for b in reversed(range(ratio)):
        j = n_full + b
        accumulate(j, weights(j, logits(j, b * tk), True, b * tk), b * tk)

    def step(j, slot):
        accumulate(j + 1, w_ref[1 - slot])
        z_ref[1 - slot] = logits(jnp.maximum(j - 1, 0))
        w_ref[slot] = weights(j, z_ref[slot], False)

    def remaining_mass():
        return jnp.max(later_ref[...])

    def two_steps(carry):
        s, _ = carry
        j = n_full - 1 - 2 * s
        step(j, 0)
        step(j - 1, 1)
        return s + 1, remaining_mass()

    def unfinished(carry):
        s, mass = carry
        return jnp.logical_and(s < n_full // 2, mass > _SB_DEAD)

    z_ref[0] = logits(jnp.maximum(n_full - 1, 0))
    w_ref[1] = jnp.zeros((tq, tk), BF16)
    trips, _ = lax.while_loop(unfinished, two_steps, (jnp.int32(0), remaining_mass()))
    accumulate(n_full - 2 * trips, w_ref[1])
    o_ref[...] = acc_ref[...].astype(o_ref.dtype)


def _stick_breaking(z, out, *, n_heads, q_col, k_col, v_col, out_col, tq=1024, tk=256):
    s = z.shape[0]

    def body(q_ref, k_ref, v_ref, out_in_ref, o_ref, *scratch):
        del out_in_ref
        _sb_kernel(q_ref, k_ref, v_ref, o_ref, *scratch, tq=tq, tk=tk)

    return pl.pallas_call(
        body,
        grid=(n_heads, s // tq),
        in_specs=[pl.BlockSpec((tq, HEAD_DIM), lambda h, i: (i, q_col + h)),
                  pl.BlockSpec((s, HEAD_DIM), lambda h, i: (0, k_col + h)),
                  pl.BlockSpec((s, HEAD_DIM), lambda h, i: (0, v_col + h)),
                  pl.BlockSpec(memory_space=pl.ANY)],
        out_specs=pl.BlockSpec((tq, HEAD_DIM), lambda h, i: (i, out_col + h)),
        out_shape=jax.ShapeDtypeStruct(out.shape, out.dtype),
        input_output_aliases={3: 0},
        scratch_shapes=[pltpu.VMEM((tq, HEAD_DIM), F32),
                        pltpu.VMEM((tq, HEAD_DIM), F32),
                        pltpu.VMEM((2, tq, tk), F32),
                        pltpu.VMEM((2, tq, tk), BF16)],
        compiler_params=_params("parallel", "arbitrary"),
        name="stick_breaking",
    )(z, z, z, out)


def _swa_kernel(sink_ref, q_ref, kp_ref, kc_ref, vp_ref, vc_ref, o_ref, *, slopes):
    i = pl.program_id(0)
    n_kv, g_per = slopes.shape
    qq = lax.broadcasted_iota(jnp.int32, (BLK, 2 * BLK), 0)
    kk = lax.broadcasted_iota(jnp.int32, (BLK, 2 * BLK), 1)
    dist = qq + BLK - kk
    first_key = jnp.where(i > 0, 0, BLK)
    mask = (dist >= 0) & (dist < SW_WINDOW) & (kk >= first_key)
    dist_f = dist.astype(F32)
    for h in range(n_kv):
        cs = slice(h * HEAD_DIM, (h + 1) * HEAD_DIM)
        k = jnp.concatenate([kp_ref[:, cs], kc_ref[:, cs]], axis=0)
        v = jnp.concatenate([vp_ref[:, cs], vc_ref[:, cs]], axis=0)
        heads = [h * g_per + g for g in range(g_per)]
        cols = [slice(head * HEAD_DIM, (head + 1) * HEAD_DIM) for head in heads]
        sinks = [sink_ref[head] for head in heads]
        s = [_dot_nt(q_ref[:, hs], k) for hs in cols]
        s = [jnp.where(mask, x * SCALE - float(slopes[h, g]) * dist_f, NEG) for g, x in enumerate(s)]
        m = [jnp.maximum(jnp.max(x, axis=-1, keepdims=True), sink) for x, sink in zip(s, sinks)]
        e = [jnp.where(mask, jnp.exp(x - mx), 0.0) for x, mx in zip(s, m)]
        p = [x / (jnp.sum(x, axis=-1, keepdims=True) + jnp.exp(sink - mx))
             for x, mx, sink in zip(e, m, sinks)]
        for hs, x in zip(cols, p):
            o_ref[:, hs] = _dot(x.astype(BF16), v).astype(o_ref.dtype)


def _window_sink_attention(z, sinks, *, q_col, k_col, v_col, out_width):
    s = z.shape[0]
    qw = H_SW * HEAD_DIM
    kw = KV_SW * HEAD_DIM
    assert q_col % qw == 0 and k_col % kw == 0 and v_col % kw == 0
    qb, kb, vb = q_col // qw, k_col // kw, v_col // kw
    prev = lambda i: jnp.maximum(i - 1, 0)
    return pl.pallas_call(
        functools.partial(_swa_kernel, slopes=_alibi_slopes(H_SW, KV_SW)),
        grid=(s // BLK,),
        in_specs=[pl.BlockSpec(memory_space=pltpu.SMEM),
                  pl.BlockSpec((BLK, qw), lambda i: (i, qb)),
                  pl.BlockSpec((BLK, kw), lambda i: (prev(i), kb)),
                  pl.BlockSpec((BLK, kw), lambda i: (i, kb)),
                  pl.BlockSpec((BLK, kw), lambda i: (prev(i), vb)),
                  pl.BlockSpec((BLK, kw), lambda i: (i, vb))],
        out_specs=pl.BlockSpec((BLK, qw), lambda i: (i, 0)),
        out_shape=jax.ShapeDtypeStruct((s, out_width), BF16),
        compiler_params=_params("parallel"),
        name="window_sink_attention",
    )(sinks.astype(F32), z, z, z, z, z)


def _compress_kernel(x_ref, pos_ref, w1_ref, w2_ref, o_ref):
    x = x_ref[0].astype(F32)
    half = x.shape[1]
    n_chunk = x.shape[0]
    lo = _dot((x + pos_ref[:, :half]).astype(BF16), w1_ref[0, :half, :])
    hi = _dot((x + pos_ref[:, half:]).astype(BF16), w1_ref[0, half:, :])
    pre = lo + pltpu.roll(hi, n_chunk - 1, 0)
    act = jax.nn.gelu(pre, approximate=True)
    out = _dot(act.astype(BF16), w2_ref[0])
    last = lax.broadcasted_iota(jnp.int32, out.shape, 0) == n_chunk - 1
    o_ref[0] = jnp.where(last, 0.0, out).astype(o_ref.dtype)


def _compress(x, pos, w1, w2):
    n, n_chunk, width = x.shape
    hid = w1.shape[-1]
    return pl.pallas_call(
        _compress_kernel,
        grid=(n,),
        in_specs=[pl.BlockSpec((1, n_chunk, width), lambda i: (i, 0, 0)),
                  pl.BlockSpec((1, 2 * width), lambda i: (0, 0)),
                  pl.BlockSpec((1, 2 * width, hid), lambda i: (i // KV_NSA, 0, 0)),
                  pl.BlockSpec((1, hid, HEAD_DIM), lambda i: (i // KV_NSA, 0, 0))],
        out_specs=pl.BlockSpec((1, n_chunk, HEAD_DIM), lambda i: (i, 0, 0)),
        out_shape=jax.ShapeDtypeStruct((n, n_chunk, HEAD_DIM), BF16),
        compiler_params=_params("parallel"),
        name="nsa_compress",
    )(x, pos, w1, w2)


_CMP_CHUNK = 256
_CMP_UNROLL = 8


def _split3(x):
    hi = x.astype(BF16)
    r1 = x - hi.astype(F32)
    mid = r1.astype(BF16)
    lo = (r1 - mid.astype(F32)).astype(BF16)
    return hi, mid, lo


def _nsa_cmp_kernel(slope_ref, q_ref, kc_ref, vc_ref, o_ref, sel_ref, need_ref,
                    imp_ref, qs_ref, os_ref, *, g_per, n_slc):
    i = pl.program_id(0)
    n_kv, n_cmp = kc_ref.shape[0], kc_ref.shape[1]
    n_heads = n_kv * g_per
    assert g_per % _CMP_UNROLL == 0
    _stack_heads(q_ref, qs_ref, n_heads)
    imp_ref[...] = jnp.zeros_like(imp_ref)

    def attend(width):
        t = i * BLK + lax.broadcasted_iota(jnp.int32, (BLK, width), 0)
        c_end = (lax.broadcasted_iota(jnp.int32, (BLK, width), 1) * NSA_CMP_STRIDE
                 + (NSA_CMP_LEN - 1))
        dist = t - c_end
        mask = dist >= 0
        dist_f = dist.astype(F32)

        def group(gg, carry):
            kv = gg // (g_per // _CMP_UNROLL)
            kc = kc_ref[kv, :width, :]
            vc = vc_ref[kv, :width, :]
            heads = [gg * _CMP_UNROLL + u for u in range(_CMP_UNROLL)]
            rows = [pl.ds(pl.multiple_of(g * BLK, BLK), BLK) for g in heads]
            s = [_dot_nt(qs_ref[r, :], kc) for r in rows]
            s = [jnp.where(mask, x * (SCALE * LOG2E) - slope_ref[g] * dist_f, NEG)
                 for x, g in zip(s, heads)]
            m = [jnp.max(x, axis=-1, keepdims=True) for x in s]
            e = [jnp.where(mask, jnp.exp2(x - mx), 0.0) for x, mx in zip(s, m)]
            p = [x / jnp.maximum(jnp.sum(x, axis=-1, keepdims=True), 1e-30) for x in e]
            for r, x in zip(rows, p):
                os_ref[r, :] = _dot(x.astype(BF16), vc)
            total = p[0]
            for x in p[1:]:
                total = total + x
            imp_ref[kv, :, :width] += total
            return carry

        lax.fori_loop(0, n_heads // _CMP_UNROLL, group, 0)

    n_visible = (BLK * i + BLK - NSA_CMP_LEN) // NSA_CMP_STRIDE + 1
    widths = sorted({min(w, n_cmp) for w in range(_CMP_CHUNK, n_cmp + _CMP_CHUNK, _CMP_CHUNK)})
    case = jnp.minimum((n_visible - 1) // _CMP_CHUNK, len(widths) - 1)
    for idx, width in enumerate(widths):
        pl.when(case == idx)(functools.partial(attend, width))
    for g in range(n_heads):
        o_ref[:, g * HEAD_DIM:(g + 1) * HEAD_DIM] = os_ref[g * BLK:(g + 1) * BLK, :]

    groups = range(n_kv)
    r = NSA_SEL_LEN // NSA_CMP_STRIDE
    ci = lax.broadcasted_iota(jnp.int32, (n_cmp, n_slc), 0)
    cj = lax.broadcasted_iota(jnp.int32, (n_cmp, n_slc), 1)
    pool = jnp.where((ci >= r * cj - 1) & (ci <= r * cj + r - 1), 1.0, 0.0).astype(BF16)
    parts = [_split3(imp_ref[kv]) for kv in groups]
    imp = [_dot(hi, pool) + (_dot(mid, pool) + _dot(lo, pool)) for hi, mid, lo in parts]

    blk = lax.broadcasted_iota(jnp.int32, (BLK, n_slc), 1)
    cur = (i * BLK + lax.broadcasted_iota(jnp.int32, (BLK, n_slc), 0)) // NSA_SEL_LEN
    valid = blk <= cur
    forced = (blk == 0) | (blk == cur) | (blk == cur - 1)
    bonus = jnp.where(forced, NSA_FORCE, 0.0)
    score = [jnp.where(valid, x + bonus, NEG) for x in imp]
    blk_f = blk.astype(F32)
    chosen = [jnp.zeros((BLK, n_slc), F32) for _ in groups]
    for _ in range(min(NSA_TOP_N, n_slc)):
        best = [jnp.max(x, axis=-1, keepdims=True) for x in score]
        first = [jnp.min(jnp.where(x == b, blk_f, float(n_slc)), axis=-1, keepdims=True)
                 for x, b in zip(score, best)]
        pick = [blk_f == f for f in first]
        chosen = [jnp.where(p, 1.0, c) for p, c in zip(pick, chosen)]
        score = [jnp.where(p, -jnp.inf, x) for p, x in zip(pick, score)]
    chosen = [jnp.where(valid, c, 0.0) for c in chosen]
    for kv in groups:
        sel_ref[kv] = chosen[kv].astype(sel_ref.dtype)

    n_kb = n_slc * NSA_SEL_LEN // BLK
    pi = lax.broadcasted_iota(jnp.int32, (n_slc, n_kb), 0)
    pj = lax.broadcasted_iota(jnp.int32, (n_slc, n_kb), 1)
    pair = jnp.where(pi * NSA_SEL_LEN // BLK == pj, 1.0, 0.0).astype(BF16)
    n_word = need_ref.shape[-1]
    bi = lax.broadcasted_iota(jnp.int32, (n_kb, n_word), 0)
    bw = lax.broadcasted_iota(jnp.int32, (n_kb, n_word), 1)
    weight = jnp.where(bi // 16 == bw, jnp.left_shift(1, bi % 16), 0).astype(F32).astype(BF16)
    for kv in groups:
        any_q = jnp.max(chosen[kv], axis=0, keepdims=True)
        any_q = jnp.broadcast_to(any_q, (8, n_slc)).astype(BF16)
        per_kb = jnp.where(_dot(any_q, pair) > 0.5, 1.0, 0.0).astype(BF16)
        need_ref[kv, 0] = _dot(per_kb, weight).astype(jnp.int32)


def _nsa_compressed(z, k_cmp, v_cmp, slopes):
    s = z.shape[0]
    n_slc = s // NSA_SEL_LEN
    g_per = H_NSA // KV_NSA
    qw = H_NSA * HEAD_DIM
    n_cmp = k_cmp.shape[1]
    n_word = 128
    assert (s // BLK) <= 16 * n_word
    return pl.pallas_call(
        functools.partial(_nsa_cmp_kernel, g_per=g_per, n_slc=n_slc),
        grid=(s // BLK,),
        in_specs=[pl.BlockSpec(memory_space=pltpu.SMEM),
                  pl.BlockSpec((BLK, qw), lambda i: (i, 0)),
                  pl.BlockSpec((KV_NSA, n_cmp, HEAD_DIM), lambda i: (0, 0, 0)),
                  pl.BlockSpec((KV_NSA, n_cmp, HEAD_DIM), lambda i: (0, 0, 0))],
        out_specs=[pl.BlockSpec((BLK, qw), lambda i: (i, 0)),
                   pl.BlockSpec((KV_NSA, BLK, n_slc), lambda i: (0, i, 0)),
                   pl.BlockSpec((KV_NSA, 1, 8, n_word), lambda i: (0, i, 0, 0))],
        out_shape=[jax.ShapeDtypeStruct((s, qw), F32),
                   jax.ShapeDtypeStruct((KV_NSA, s, n_slc), BF16),
                   jax.ShapeDtypeStruct((KV_NSA, s // BLK, 8, n_word), jnp.int32)],
        scratch_shapes=[pltpu.VMEM((KV_NSA, BLK, n_cmp), F32),
                        pltpu.VMEM((H_NSA * BLK, HEAD_DIM), BF16),
                        pltpu.VMEM((H_NSA * BLK, HEAD_DIM), F32)],
        compiler_params=_params("parallel"),
        name="nsa_compressed",
    )(slopes, z, k_cmp, v_cmp)


def _stack_heads(q_ref, qs_ref, g_per):
    for g in range(g_per):
        qs_ref[g * BLK:(g + 1) * BLK, :] = q_ref[:, g * HEAD_DIM:(g + 1) * HEAD_DIM]


_ROW_SPLIT = 2


def _online_visit(i, blocks, qs_ref, k_ref, v_ref, slope_ref, m_ref, l_ref, acc_ref, g_per):
    nb = len(blocks)
    starts = [pl.multiple_of(j * BLK, BLK) for j, _ in blocks]
    k = jnp.concatenate([k_ref[pl.ds(st, BLK), :] for st in starts], axis=0)
    v = jnp.concatenate([v_ref[pl.ds(st, BLK), :] for st in starts], axis=0)
    v_ones = jnp.concatenate([v, jnp.ones((nb * BLK, HEAD_DIM), BF16)], axis=1)
    qq = lax.broadcasted_iota(jnp.int32, (BLK, BLK), 0) - lax.broadcasted_iota(jnp.int32, (BLK, BLK), 1)
    dists = [(i - j) * BLK + qq for j, _ in blocks]
    ok1 = jnp.concatenate([key_ok(d) for (_, key_ok), d in zip(blocks, dists)], axis=1)
    dist1 = jnp.concatenate([d.astype(F32) for d in dists], axis=1)
    g_sub = g_per // _ROW_SPLIT
    ok = jnp.concatenate([ok1] * g_sub, axis=0) > 0.5
    dist_f = jnp.concatenate([dist1] * g_sub, axis=0)

    groups = [slice(r * g_sub * BLK, (r + 1) * g_sub * BLK) for r in range(_ROW_SPLIT)]
    s = [_dot_nt(qs_ref[rows, :], k) for rows in groups]
    s = [jnp.where(ok, x * (SCALE * LOG2E) - jnp.concatenate([slope_ref[0, rows, :]] * nb, axis=1) * dist_f, NEG)
         for x, rows in zip(s, groups)]
    m_prev = [m_ref[rows, :] for rows in groups]
    m_new = [jnp.maximum(mp, jnp.max(x, axis=-1, keepdims=True)) for mp, x in zip(m_prev, s)]
    alpha = [jnp.exp2(mp - mn) for mp, mn in zip(m_prev, m_new)]
    p = [jnp.where(ok, jnp.exp2(x - jnp.concatenate([mn] * nb, axis=1)), 0.0) for x, mn in zip(s, m_new)]
    pv = [_dot(x.astype(BF16), v_ones) for x in p]
    for rows, a, y, mn in zip(groups, alpha, pv, m_new):
        l_ref[rows, :] = a * l_ref[rows, :] + y[:, HEAD_DIM:]
        acc_ref[rows, :] = a * acc_ref[rows, :] + y[:, :HEAD_DIM]
        m_ref[rows, :] = mn


def _online_init(m_ref, l_ref, acc_ref):
    m_ref[...] = jnp.full_like(m_ref, NEG)
    l_ref[...] = jnp.zeros_like(l_ref)
    acc_ref[...] = jnp.zeros_like(acc_ref)


_SEL_GROUP = 8


def _nsa_sel_kernel(need_ref, q_ref, sel_ref, slope_ref, k_ref, v_ref, o_ref,
                    qs_ref, m_ref, l_ref, acc_ref, list_ref, *, g_per, n_qblk, n_word):
    h = pl.program_id(0)
    i = pl.program_id(1)
    _stack_heads(q_ref, qs_ref, g_per)
    _online_init(m_ref, l_ref, acc_ref)
    sel = sel_ref[0]
    n_slc = sel.shape[1]
    per_blk = BLK // NSA_SEL_LEN
    eb = lax.broadcasted_iota(jnp.int32, (n_slc, BLK), 0)
    el = lax.broadcasted_iota(jnp.int32, (n_slc, BLK), 1) // NSA_SEL_LEN
    base = (h * n_qblk + i) * n_word

    def scan(j, n):
        word = need_ref[base + jnp.right_shift(j, 4)]
        needed = jnp.bitwise_and(jnp.right_shift(word, jnp.bitwise_and(j, 15)), 1)
        list_ref[n] = j
        return n + needed

    n_needed = lax.fori_loop(0, i + 1, scan, 0)

    def visit(first, count):
        blocks = []
        for u in range(count):
            j = list_ref[first + u]
            expand = jnp.where(eb == per_blk * j + el, 1.0, 0.0).astype(BF16)
            picked = _dot(sel, expand)
            blocks.append((j, functools.partial(
                lambda dist, picked: jnp.where(dist >= 0, picked, 0.0), picked=picked)))
        _online_visit(i, blocks, qs_ref, k_ref, v_ref, slope_ref, m_ref, l_ref, acc_ref, g_per)

    def group(t, carry):
        visit(t * _SEL_GROUP, _SEL_GROUP)
        return carry

    n_groups = n_needed // _SEL_GROUP
    lax.fori_loop(0, n_groups, group, 0)
    done = n_groups * _SEL_GROUP
    size = _SEL_GROUP // 2
    while size >= 1:
        has = jnp.bitwise_and(n_needed, size) != 0
        pl.when(has)(functools.partial(visit, done, size))
        done = done + jnp.where(has, size, 0)
        size //= 2
    out = acc_ref[...] / jnp.maximum(l_ref[...], 1e-30)
    for g in range(g_per):
        o_ref[:, g * HEAD_DIM:(g + 1) * HEAD_DIM] = out[g * BLK:(g + 1) * BLK, :]


def _nsa_selected(z, sel, need, slope_rows, *, k_col, v_col):
    s = z.shape[0]
    g_per = H_NSA // KV_NSA
    qw = g_per * HEAD_DIM
    rows = g_per * BLK
    n_slc = sel.shape[-1]
    n_qblk = s // BLK
    n_word = -(-n_qblk // 16)
    need = need[:, :, 0, :n_word]
    return pl.pallas_call(
        functools.partial(_nsa_sel_kernel, g_per=g_per, n_qblk=n_qblk, n_word=n_word),
        grid=(KV_NSA, n_qblk),
        in_specs=[pl.BlockSpec(memory_space=pltpu.SMEM),
                  pl.BlockSpec((BLK, qw), lambda h, i: (i, h)),
                  pl.BlockSpec((1, BLK, n_slc), lambda h, i: (h, i, 0)),
                  pl.BlockSpec((1, rows, HEAD_DIM), lambda h, i: (h, 0, 0)),
                  pl.BlockSpec((s, HEAD_DIM), lambda h, i: (0, k_col + h)),
                  pl.BlockSpec((s, HEAD_DIM), lambda h, i: (0, v_col + h))],
        out_specs=pl.BlockSpec((BLK, qw), lambda h, i: (i, h)),
        out_shape=jax.ShapeDtypeStruct((s, H_NSA * HEAD_DIM), F32),
        scratch_shapes=[pltpu.VMEM((rows, HEAD_DIM), BF16),
                        pltpu.VMEM((rows, HEAD_DIM), F32),
                        pltpu.VMEM((rows, HEAD_DIM), F32),
                        pltpu.VMEM((rows, HEAD_DIM), F32),
                        pltpu.SMEM((n_qblk + _SEL_GROUP,), jnp.int32)],
        compiler_params=_params("parallel", "arbitrary"),
        name="nsa_selected",
    )(need.reshape(-1), z, sel, slope_rows, z, z)


def _nsa_win_kernel(q_ref, slope_ref, k_ref, v_ref, gate_ref, ocmp_ref, oslc_ref, o_ref,
                    qs_ref, m_ref, l_ref, acc_ref, *, g_per):
    i = pl.program_id(1)
    _stack_heads(q_ref, qs_ref, g_per)
    _online_init(m_ref, l_ref, acc_ref)
    n_prev = -(-(NSA_WINDOW - 1) // BLK)

    blocks = []
    for step in range(n_prev + 1):
        j = i - n_prev + step
        present = jnp.where(j >= 0, 1.0, 0.0)
        blocks.append((jnp.maximum(j, 0), functools.partial(
            lambda dist, present: jnp.where((dist >= 0) & (dist < NSA_WINDOW), present, 0.0),
            present=present)))
    _online_visit(i, blocks, qs_ref, k_ref, v_ref, slope_ref, m_ref, l_ref, acc_ref, g_per)
    o_win = acc_ref[...] / jnp.maximum(l_ref[...], 1e-30)
    gates = jax.nn.sigmoid(gate_ref[...])
    for g in range(g_per):
        hs = slice(g * HEAD_DIM, (g + 1) * HEAD_DIM)
        o = (gates[:, 3 * g:3 * g + 1] * ocmp_ref[:, hs]
             + gates[:, 3 * g + 1:3 * g + 2] * oslc_ref[:, hs]
             + gates[:, 3 * g + 2:3 * g + 3] * o_win[g * BLK:(g + 1) * BLK, :])
        o_ref[:, hs] = o.astype(o_ref.dtype)


def _nsa_window_combine(z, gate_logits, o_cmp, o_slc, slope_rows, *, k_col, v_col):
    s = z.shape[0]
    g_per = H_NSA // KV_NSA
    qw = g_per * HEAD_DIM
    rows = g_per * BLK
    head_blk = lambda h, i: (i, h)
    return pl.pallas_call(
        functools.partial(_nsa_win_kernel, g_per=g_per),
        grid=(KV_NSA, s // BLK),
        in_specs=[pl.BlockSpec((BLK, qw), head_blk),
                  pl.BlockSpec((1, rows, HEAD_DIM), lambda h, i: (h, 0, 0)),
                  pl.BlockSpec((s, HEAD_DIM), lambda h, i: (0, k_col + h)),
                  pl.BlockSpec((s, HEAD_DIM), lambda h, i: (0, v_col + h)),
                  pl.BlockSpec((BLK, 128), head_blk),
                  pl.BlockSpec((BLK, qw), head_blk),
                  pl.BlockSpec((BLK, qw), head_blk)],
        out_specs=pl.BlockSpec((BLK, qw), head_blk),
        out_shape=jax.ShapeDtypeStruct((s, H_NSA * HEAD_DIM), BF16),
        scratch_shapes=[pltpu.VMEM((rows, HEAD_DIM), BF16),
                        pltpu.VMEM((rows, HEAD_DIM), F32),
                        pltpu.VMEM((rows, HEAD_DIM), F32),
                        pltpu.VMEM((rows, HEAD_DIM), F32)],
        compiler_params=_params("parallel", "arbitrary"),
        name="nsa_window_combine",
    )(z, slope_rows, z, z, gate_logits, o_cmp, o_slc)


def _ab_mixer(h, w_in, sinks, w_out, x_res):
    z = _matmul(h, w_in.astype(BF16), BF16, tm=1024, tn=768)
    d_sb = H_SB * HEAD_DIM
    q_b = 3 * d_sb
    k_b = q_b + H_SW * HEAD_DIM
    v_b = k_b + KV_SW * HEAD_DIM
    o = _window_sink_attention(z, sinks, q_col=q_b, k_col=k_b, v_col=v_b, out_width=w_out.shape[0])
    o = _stick_breaking(z, o, n_heads=H_SB, q_col=0, k_col=H_SB, v_col=2 * H_SB, out_col=H_SW)
    w_out = jnp.concatenate([w_out[d_sb:], w_out[:d_sb]], axis=0).astype(BF16)
    return _matmul(o, w_out, F32, tm=1024, tn=512, residual=x_res)


def _nsa_mixer(h, w_in, cmp_pos, wk1, wk2, wv1, wv2, w_out, x_res):
    s = h.shape[0]
    dq = H_NSA * HEAD_DIM
    dkv = KV_NSA * HEAD_DIM
    g_per = H_NSA // KV_NSA
    main = dq + 6 * dkv
    z = _matmul(h, w_in[:, :main].astype(BF16), BF16, tm=1024, tn=512)
    w_gate = w_in[:, main:].reshape(-1, KV_NSA, 3 * g_per)
    w_gate = jnp.pad(w_gate, ((0, 0), (0, 0), (0, 128 - 3 * g_per))).reshape(-1, KV_NSA * 128)
    gate_logits = _matmul(h, w_gate.astype(BF16), F32, tm=1024, tn=KV_NSA * 128)

    n_chunk = s // NSA_CMP_STRIDE
    to_chunks = lambda c: (z[:, c:c + dkv].reshape(n_chunk, NSA_CMP_STRIDE, KV_NSA, HEAD_DIM)
                           .transpose(2, 0, 1, 3).reshape(KV_NSA, n_chunk, NSA_CMP_STRIDE * HEAD_DIM))
    chunks = jnp.concatenate([to_chunks(dq), to_chunks(dq + dkv)], axis=0)
    kv_cmp = _compress(chunks, cmp_pos.reshape(1, -1).astype(F32),
                       jnp.stack([wk1, wv1]).astype(BF16), jnp.stack([wk2, wv2]).astype(BF16))
    k_cmp, v_cmp = kv_cmp[:KV_NSA], kv_cmp[KV_NSA:]

    slopes = _alibi_slopes(H_NSA, KV_NSA) * np.float32(LOG2E)
    slope_rows = jnp.asarray(np.repeat(np.repeat(slopes, BLK, axis=1)[..., None], HEAD_DIM, axis=2))
    o_cmp, sel, need = _nsa_compressed(z, k_cmp, v_cmp, jnp.asarray(slopes.reshape(-1)))
    col = lambda off: (dq + off * dkv) // HEAD_DIM
    o_slc = _nsa_selected(z, sel, need, slope_rows, k_col=col(2), v_col=col(3))
    o = _nsa_window_combine(z, gate_logits, o_cmp, o_slc, slope_rows, k_col=col(4), v_col=col(5))
    return _matmul(o, w_out.astype(BF16), F32, tm=1024, tn=512, residual=x_res)


def _swiglu(h, w_gate_up, w_down, layer, x_res):
    act = _gate_up(h, w_gate_up, layer, tm=2048, tn=256)
    return _matmul(act, w_down, F32, tm=512, tn=512, residual=x_res, b_index=layer)


def kernel(x, attn_norm, ffn_norm, w_gate_up, w_down, ab_w_in, ab_sinks, ab_w_out, nsa_w_in,
           nsa_cmp_pos, nsa_cmp_wk1, nsa_cmp_wk2, nsa_cmp_wv1, nsa_cmp_wv2, nsa_w_out, final_norm):
    b, s, d = x.shape
    w_down = w_down.astype(BF16)
    outs = []
    for bi in range(b):
        xb = x[bi]
        for layer in range(attn_norm.shape[0]):
            h = _rmsnorm(xb, attn_norm[layer], BF16)
            if layer % 2 == 0:
                e = layer // 2
                xb = _ab_mixer(h, ab_w_in[e], ab_sinks[e], ab_w_out[e], xb)
            else:
                o = layer // 2
                xb = _nsa_mixer(h, nsa_w_in[o], nsa_cmp_pos[o], nsa_cmp_wk1[o], nsa_cmp_wk2[o],
                                nsa_cmp_wv1[o], nsa_cmp_wv2[o], nsa_w_out[o], xb)
            h = _rmsnorm(xb, ffn_norm[layer], BF16)
            xb = _swiglu(h, w_gate_up, w_down, layer, xb)
        outs.append(_rmsnorm(xb, final_norm, F32))
    return jnp.stack(outs, axis=0)
```

```python
import functools

import numpy as np
import jax
import jax.numpy as jnp
from jax import lax
from jax.experimental import pallas as pl
from jax.experimental.pallas import tpu as pltpu

F32 = jnp.float32
BF16 = jnp.bfloat16

HEAD_DIM = 128
BLK = 128
EPS = 1e-5
NEG = -1e30
SCALE = HEAD_DIM ** -0.5
LOG2E = 1.4426950408889634

H_SB = 8
H_SW = 24
KV_SW = 3
SW_WINDOW = 128

H_NSA = 32
KV_NSA = 2
NSA_CMP_LEN = 32
NSA_CMP_STRIDE = 16
NSA_SEL_LEN = 64
NSA_TOP_N = 8
NSA_WINDOW = 512
NSA_FORCE = 1e4

VMEM_LIMIT_BYTES = 52 * 2 ** 20
GATE_UP_VMEM_LIMIT_BYTES = 60 * 2 ** 20

_NT = (((1,), (1,)), ((), ()))


def _params(*sem, vmem_limit_bytes=VMEM_LIMIT_BYTES):
    return pltpu.CompilerParams(dimension_semantics=sem, vmem_limit_bytes=vmem_limit_bytes)


def _dot(a, b):
    return jnp.dot(a, b, preferred_element_type=F32)


def _dot_nt(a, b):
    return lax.dot_general(a, b, _NT, preferred_element_type=F32)


def _alibi_slopes(n_heads, n_kv):
    s = np.exp2(np.float32(-8.0) * np.arange(1, n_heads + 1, dtype=np.float32)
                / np.float32(n_heads)).astype(np.float32)
    return s.reshape(n_heads // n_kv, n_kv).T


def _rmsnorm_kernel(x_ref, g_ref, o_ref):
    x = x_ref[...]
    ms = jnp.mean(x * x, axis=-1, keepdims=True)
    o_ref[...] = (x * lax.rsqrt(ms + EPS) * g_ref[...]).astype(o_ref.dtype)


def _rmsnorm(x, g, out_dtype, tm=256):
    m, d = x.shape
    return pl.pallas_call(
        _rmsnorm_kernel,
        grid=(m // tm,),
        in_specs=[pl.BlockSpec((tm, d), lambda i: (i, 0)),
                  pl.BlockSpec((1, d), lambda i: (0, 0))],
        out_specs=pl.BlockSpec((tm, d), lambda i: (i, 0)),
        out_shape=jax.ShapeDtypeStruct((m, d), out_dtype),
        compiler_params=_params("parallel"),
        name="rmsnorm",
    )(x, g.reshape(1, d).astype(F32))


def _mm_kernel(*refs, nk, has_res):
    if has_res:
        a_ref, b_ref, r_ref, o_ref = refs[:4]
        scratch = refs[4:]
    else:
        a_ref, b_ref, o_ref = refs[:3]
        r_ref = None
        scratch = refs[3:]

    def finish(acc):
        if has_res:
            acc = acc + r_ref[...]
        o_ref[...] = acc.astype(o_ref.dtype)

    part = _dot(a_ref[...], b_ref[...])
    if nk == 1:
        finish(part)
    else:
        acc_ref, = scratch
        k = pl.program_id(2)

        @pl.when(k == 0)
        def _():
            acc_ref[...] = part

        @pl.when(k > 0)
        def _():
            acc_ref[...] += part

        @pl.when(k == nk - 1)
        def _():
            finish(acc_ref[...])


def _matmul(a, b, out_dtype, *, tm, tn, tk=None, residual=None, b_index=None):
    m, kd = a.shape
    n = b.shape[-1]
    tk = kd if tk is None else tk
    nk = kd // tk
    assert m % tm == 0 and n % tn == 0 and kd % tk == 0
    if b_index is None:
        b_spec = pl.BlockSpec((tk, tn), lambda i, j, k: (k, j))
    else:
        b_spec = pl.BlockSpec((None, tk, tn), lambda i, j, k: (b_index, k, j))
    in_specs = [pl.BlockSpec((tm, tk), lambda i, j, k: (i, k)), b_spec]
    args = [a, b]
    if residual is not None:
        in_specs.append(pl.BlockSpec((tm, tn), lambda i, j, k: (i, j)))
        args.append(residual)
    return pl.pallas_call(
        functools.partial(_mm_kernel, nk=nk, has_res=residual is not None),
        grid=(m // tm, n // tn, nk),
        in_specs=in_specs,
        out_specs=pl.BlockSpec((tm, tn), lambda i, j, k: (i, j)),
        out_shape=jax.ShapeDtypeStruct((m, n), out_dtype),
        scratch_shapes=[pltpu.VMEM((tm, tn), F32)] if nk > 1 else [],
        compiler_params=_params("parallel", "parallel", "arbitrary"),
        name="matmul",
    )(*args)


def _gate_up_kernel(a_ref, wg_ref, wu_ref, o_ref):
    a = a_ref[...]
    gate = _dot(a, wg_ref[...].astype(BF16))
    up = _dot(a, wu_ref[...].astype(BF16))
    o_ref[...] = (gate * jax.nn.sigmoid(gate) * up).astype(o_ref.dtype)


def _gate_up(a, w_gate_up, layer, *, tm, tn):
    m, kd = a.shape
    f = w_gate_up.shape[-1] // 2
    nf = f // tn
    assert m % tm == 0 and f % tn == 0
    return pl.pallas_call(
        _gate_up_kernel,
        grid=(m // tm, nf),
        in_specs=[pl.BlockSpec((tm, kd), lambda i, j: (i, 0)),
                  pl.BlockSpec((None, kd, tn), lambda i, j: (layer, 0, j)),
                  pl.BlockSpec((None, kd, tn), lambda i, j: (layer, 0, j + nf))],
        out_specs=pl.BlockSpec((tm, tn), lambda i, j: (i, j)),
        out_shape=jax.ShapeDtypeStruct((m, f), BF16),
        compiler_params=_params("parallel", "parallel", vmem_limit_bytes=GATE_UP_VMEM_LIMIT_BYTES),
        name="gate_up",
    )(a, w_gate_up, w_gate_up)


_SB_DEAD = -200.0


def _sb_kernel(q_ref, k_ref, v_ref, o_ref, acc_ref, later_ref, z_ref, w_ref, *, tq, tk):
    qi = pl.program_id(1)
    ratio = tq // tk
    assert ratio % 2 == 0
    q = q_ref[...]
    row = lax.broadcasted_iota(jnp.int32, (tk, tk), 0)
    col = lax.broadcasted_iota(jnp.int32, (tk, tk), 1)
    suffix = jnp.where(row >= col, 1.0, 0.0).astype(BF16)
    acc_ref[...] = jnp.zeros_like(acc_ref)
    later_ref[...] = jnp.zeros_like(later_ref)

    def logits(j, r0=0):
        start = pl.multiple_of(j * tk, tk)
        return _dot_nt(q[r0:], k_ref[pl.ds(start, tk), :]) * (-SCALE * LOG2E)

    def weights(j, nz, masked, r0=0):
        neg_abs = lax.bitcast_convert_type(
            lax.bitcast_convert_type(nz, jnp.uint32) | jnp.uint32(0x80000000), F32)
        a = jnp.minimum(nz, 0.0) - jnp.log2(1.0 + jnp.exp2(neg_abs))
        if masked:
            t_pos = qi * tq + r0 + lax.broadcasted_iota(jnp.int32, nz.shape, 0)
            s_pos = j * tk + lax.broadcasted_iota(jnp.int32, nz.shape, 1)
            before = s_pos < t_pos
            a = jnp.where(before, a, 0.0)
        within = _dot(a.astype(BF16), suffix)
        later = later_ref[r0:, :]
        w = jnp.exp2(within + jnp.concatenate([later] * (tk // HEAD_DIM), axis=1) - nz)
        later_ref[r0:, :] = later + within[:, 0:1]
        if masked:
            w = jnp.where(before, w, 0.0)
        return w.astype(BF16)

    def accumulate(j, w, r0=0):
        start = pl.multiple_of(j * tk, tk)
        acc_ref[r0:, :] += _dot(w, v_ref[pl.ds(start, tk), :])

    n_full = ratio * qi

    for b in reversed(range(ratio)):
        j = n_full + b
        accumulate(j, weights(j, logits(j, b * tk), True, b * tk), b * tk)

    def step(j, slot):
        accumulate(j + 1, w_ref[1 - slot])
        z_ref[1 - slot] = logits(jnp.maximum(j - 1, 0))
        w_ref[slot] = weights(j, z_ref[slot], False)

    def remaining_mass():
        return jnp.max(later_ref[...])

    def two_steps(carry):
        s, _ = carry
        j = n_full - 1 - 2 * s
        step(j, 0)
        step(j - 1, 1)
        return s + 1, remaining_mass()

    def unfinished(carry):
        s, mass = carry
        return jnp.logical_and(s < n_full // 2, mass > _SB_DEAD)

    z_ref[0] = logits(jnp.maximum(n_full - 1, 0))
    w_ref[1] = jnp.zeros((tq, tk), BF16)
    trips, _ = lax.while_loop(unfinished, two_steps, (jnp.int32(0), remaining_mass()))
    accumulate(n_full - 2 * trips, w_ref[1])
    o_ref[...] = acc_ref[...].astype(o_ref.dtype)


def _stick_breaking(z, *, n_heads, q_col, k_col, v_col, tq=1024, tk=256):
    s = z.shape[0]
    return pl.pallas_call(
        functools.partial(_sb_kernel, tq=tq, tk=tk),
        grid=(n_heads, s // tq),
        in_specs=[pl.BlockSpec((tq, HEAD_DIM), lambda h, i: (i, q_col + h)),
                  pl.BlockSpec((s, HEAD_DIM), lambda h, i: (0, k_col + h)),
                  pl.BlockSpec((s, HEAD_DIM), lambda h, i: (0, v_col + h))],
        out_specs=pl.BlockSpec((tq, HEAD_DIM), lambda h, i: (i, h)),
        out_shape=jax.ShapeDtypeStruct((s, n_heads * HEAD_DIM), BF16),
        scratch_shapes=[pltpu.VMEM((tq, HEAD_DIM), F32),
                        pltpu.VMEM((tq, HEAD_DIM), F32),
                        pltpu.VMEM((2, tq, tk), F32),
                        pltpu.VMEM((2, tq, tk), BF16)],
        compiler_params=_params("parallel", "arbitrary"),
        name="stick_breaking",
    )(z, z, z)


def _swa_kernel(sink_ref, q_ref, kp_ref, kc_ref, vp_ref, vc_ref, o_ref, *, slopes):
    i = pl.program_id(0)
    n_kv, g_per = slopes.shape
    qq = lax.broadcasted_iota(jnp.int32, (BLK, 2 * BLK), 0)
    kk = lax.broadcasted_iota(jnp.int32, (BLK, 2 * BLK), 1)
    dist = qq + BLK - kk
    first_key = jnp.where(i > 0, 0, BLK)
    mask = (dist >= 0) & (dist < SW_WINDOW) & (kk >= first_key)
    dist_f = dist.astype(F32)
    for h in range(n_kv):
        cs = slice(h * HEAD_DIM, (h + 1) * HEAD_DIM)
        k = jnp.concatenate([kp_ref[:, cs], kc_ref[:, cs]], axis=0)
        v = jnp.concatenate([vp_ref[:, cs], vc_ref[:, cs]], axis=0)
        heads = [h * g_per + g for g in range(g_per)]
        cols = [slice(head * HEAD_DIM, (head + 1) * HEAD_DIM) for head in heads]
        sinks = [sink_ref[head] for head in heads]
        s = [_dot_nt(q_ref[:, hs], k) for hs in cols]
        s = [jnp.where(mask, x * SCALE - float(slopes[h, g]) * dist_f, NEG) for g, x in enumerate(s)]
        m = [jnp.maximum(jnp.max(x, axis=-1, keepdims=True), sink) for x, sink in zip(s, sinks)]
        e = [jnp.where(mask, jnp.exp(x - mx), 0.0) for x, mx in zip(s, m)]
        p = [x / (jnp.sum(x, axis=-1, keepdims=True) + jnp.exp(sink - mx))
             for x, mx, sink in zip(e, m, sinks)]
        for hs, x in zip(cols, p):
            o_ref[:, hs] = _dot(x.astype(BF16), v).astype(o_ref.dtype)


def _window_sink_attention(z, sinks, *, q_col, k_col, v_col):
    s = z.shape[0]
    qw = H_SW * HEAD_DIM
    kw = KV_SW * HEAD_DIM
    assert q_col % qw == 0 and k_col % kw == 0 and v_col % kw == 0
    qb, kb, vb = q_col // qw, k_col // kw, v_col // kw
    prev = lambda i: jnp.maximum(i - 1, 0)
    return pl.pallas_call(
        functools.partial(_swa_kernel, slopes=_alibi_slopes(H_SW, KV_SW)),
        grid=(s // BLK,),
        in_specs=[pl.BlockSpec(memory_space=pltpu.SMEM),
                  pl.BlockSpec((BLK, qw), lambda i: (i, qb)),
                  pl.BlockSpec((BLK, kw), lambda i: (prev(i), kb)),
                  pl.BlockSpec((BLK, kw), lambda i: (i, kb)),
                  pl.BlockSpec((BLK, kw), lambda i: (prev(i), vb)),
                  pl.BlockSpec((BLK, kw), lambda i: (i, vb))],
        out_specs=pl.BlockSpec((BLK, qw), lambda i: (i, 0)),
        out_shape=jax.ShapeDtypeStruct((s, qw), BF16),
        compiler_params=_params("parallel"),
        name="window_sink_attention",
    )(sinks.astype(F32), z, z, z, z, z)


def _compress_kernel(x_ref, pos_ref, w1_ref, w2_ref, o_ref):
    x = x_ref[0].astype(F32)
    half = x.shape[1]
    n_chunk = x.shape[0]
    lo = _dot((x + pos_ref[:, :half]).astype(BF16), w1_ref[0, :half, :])
    hi = _dot((x + pos_ref[:, half:]).astype(BF16), w1_ref[0, half:, :])
    pre = lo + pltpu.roll(hi, n_chunk - 1, 0)
    act = jax.nn.gelu(pre, approximate=True)
    out = _dot(act.astype(BF16), w2_ref[0])
    last = lax.broadcasted_iota(jnp.int32, out.shape, 0) == n_chunk - 1
    o_ref[0] = jnp.where(last, 0.0, out).astype(o_ref.dtype)


def _compress(x, pos, w1, w2):
    n, n_chunk, width = x.shape
    hid = w1.shape[-1]
    return pl.pallas_call(
        _compress_kernel,
        grid=(n,),
        in_specs=[pl.BlockSpec((1, n_chunk, width), lambda i: (i, 0, 0)),
                  pl.BlockSpec((1, 2 * width), lambda i: (0, 0)),
                  pl.BlockSpec((1, 2 * width, hid), lambda i: (i // KV_NSA, 0, 0)),
                  pl.BlockSpec((1, hid, HEAD_DIM), lambda i: (i // KV_NSA, 0, 0))],
        out_specs=pl.BlockSpec((1, n_chunk, HEAD_DIM), lambda i: (i, 0, 0)),
        out_shape=jax.ShapeDtypeStruct((n, n_chunk, HEAD_DIM), BF16),
        compiler_params=_params("parallel"),
        name="nsa_compress",
    )(x, pos, w1, w2)


_CMP_CHUNK = 256
_CMP_UNROLL = 8


def _split3(x):
    hi = x.astype(BF16)
    r1 = x - hi.astype(F32)
    mid = r1.astype(BF16)
    lo = (r1 - mid.astype(F32)).astype(BF16)
    return hi, mid, lo


def _nsa_cmp_kernel(slope_ref, q_ref, kc_ref, vc_ref, o_ref, sel_ref, need_ref,
                    imp_ref, qs_ref, os_ref, pool_ref, *, g_per, n_slc):
    i = pl.program_id(0)
    n_kv, n_cmp = kc_ref.shape[0], kc_ref.shape[1]
    n_heads = n_kv * g_per
    assert g_per % _CMP_UNROLL == 0
    _stack_heads(q_ref, qs_ref, n_heads)
    imp_ref[...] = jnp.zeros_like(imp_ref)

    def attend(width):
        t = i * BLK + lax.broadcasted_iota(jnp.int32, (BLK, width), 0)
        c_end = (lax.broadcasted_iota(jnp.int32, (BLK, width), 1) * NSA_CMP_STRIDE
                 + (NSA_CMP_LEN - 1))
        dist = t - c_end
        mask = dist >= 0
        dist_f = dist.astype(F32)

        def group(gg, carry):
            kv = gg // (g_per // _CMP_UNROLL)
            kc = kc_ref[kv, :width, :]
            vc = vc_ref[kv, :width, :]
            heads = [gg * _CMP_UNROLL + u for u in range(_CMP_UNROLL)]
            rows = [pl.ds(pl.multiple_of(g * BLK, BLK), BLK) for g in heads]
            s = [_dot_nt(qs_ref[r, :], kc) for r in rows]
            s = [jnp.where(mask, x * (SCALE * LOG2E) - slope_ref[g] * dist_f, NEG)
                 for x, g in zip(s, heads)]
            m = [jnp.max(x, axis=-1, keepdims=True) for x in s]
            e = [jnp.where(mask, jnp.exp2(x - mx), 0.0) for x, mx in zip(s, m)]
            p = [x / jnp.maximum(jnp.sum(x, axis=-1, keepdims=True), 1e-30) for x in e]
            for r, x in zip(rows, p):
                os_ref[r, :] = _dot(x.astype(BF16), vc)
            total = p[0]
            for x in p[1:]:
                total = total + x
            imp_ref[kv, :, :width] += total
            return carry

        lax.fori_loop(0, n_heads // _CMP_UNROLL, group, 0)

    n_visible = (BLK * i + BLK - NSA_CMP_LEN) // NSA_CMP_STRIDE + 1
    widths = sorted({min(w, n_cmp) for w in range(_CMP_CHUNK, n_cmp + _CMP_CHUNK, _CMP_CHUNK)})
    case = jnp.minimum((n_visible - 1) // _CMP_CHUNK, len(widths) - 1)
    for idx, width in enumerate(widths):
        pl.when(case == idx)(functools.partial(attend, width))
    for g in range(n_heads):
        o_ref[:, g * HEAD_DIM:(g + 1) * HEAD_DIM] = os_ref[g * BLK:(g + 1) * BLK, :]

    groups = range(n_kv)
    r = NSA_SEL_LEN // NSA_CMP_STRIDE
    @pl.when(i == 0)
    def _():
        ci = lax.broadcasted_iota(jnp.int32, (n_cmp, n_slc), 0)
        cj = lax.broadcasted_iota(jnp.int32, (n_cmp, n_slc), 1)
        pool_ref[...] = jnp.where((ci >= r * cj - 1) & (ci <= r * cj + r - 1), 1.0, 0.0).astype(BF16)

    pool = pool_ref[...]
    parts = [_split3(imp_ref[kv]) for kv in groups]
    imp = [_dot(hi, pool) + (_dot(mid, pool) + _dot(lo, pool)) for hi, mid, lo in parts]

    blk = lax.broadcasted_iota(jnp.int32, (BLK, n_slc), 1)
    cur = (i * BLK + lax.broadcasted_iota(jnp.int32, (BLK, n_slc), 0)) // NSA_SEL_LEN
    valid = blk <= cur
    forced = (blk == 0) | (blk == cur) | (blk == cur - 1)
    bonus = jnp.where(forced, NSA_FORCE, 0.0)
    score = [jnp.where(valid, x + bonus, NEG) for x in imp]
    blk_f = blk.astype(F32)
    chosen = [jnp.zeros((BLK, n_slc), F32) for _ in groups]
    for _ in range(min(NSA_TOP_N, n_slc)):
        best = [jnp.max(x, axis=-1, keepdims=True) for x in score]
        first = [jnp.min(jnp.where(x == b, blk_f, float(n_slc)), axis=-1, keepdims=True)
                 for x, b in zip(score, best)]
        pick = [blk_f == f for f in first]
        chosen = [jnp.where(p, 1.0, c) for p, c in zip(pick, chosen)]
        score = [jnp.where(p, -jnp.inf, x) for p, x in zip(pick, score)]
    chosen = [jnp.where(valid, c, 0.0) for c in chosen]
    for kv in groups:
        sel_ref[kv] = chosen[kv].astype(sel_ref.dtype)

    n_kb = n_slc * NSA_SEL_LEN // BLK
    pi = lax.broadcasted_iota(jnp.int32, (n_slc, n_kb), 0)
    pj = lax.broadcasted_iota(jnp.int32, (n_slc, n_kb), 1)
    pair = jnp.where(pi * NSA_SEL_LEN // BLK == pj, 1.0, 0.0).astype(BF16)
    n_word = need_ref.shape[-1]
    bi = lax.broadcasted_iota(jnp.int32, (n_kb, n_word), 0)
    bw = lax.broadcasted_iota(jnp.int32, (n_kb, n_word), 1)
    weight = jnp.where(bi // 16 == bw, jnp.left_shift(1, bi % 16), 0).astype(F32).astype(BF16)
    for kv in groups:
        any_q = jnp.max(chosen[kv], axis=0, keepdims=True)
        any_q = jnp.broadcast_to(any_q, (8, n_slc)).astype(BF16)
        per_kb = jnp.where(_dot(any_q, pair) > 0.5, 1.0, 0.0).astype(BF16)
        need_ref[kv, 0] = _dot(per_kb, weight).astype(jnp.int32)


def _nsa_compressed(z, k_cmp, v_cmp, slopes):
    s = z.shape[0]
    n_slc = s // NSA_SEL_LEN
    g_per = H_NSA // KV_NSA
    qw = H_NSA * HEAD_DIM
    n_cmp = k_cmp.shape[1]
    n_word = 128
    assert (s // BLK) <= 16 * n_word
    return pl.pallas_call(
        functools.partial(_nsa_cmp_kernel, g_per=g_per, n_slc=n_slc),
        grid=(s // BLK,),
        in_specs=[pl.BlockSpec(memory_space=pltpu.SMEM),
                  pl.BlockSpec((BLK, qw), lambda i: (i, 0)),
                  pl.BlockSpec((KV_NSA, n_cmp, HEAD_DIM), lambda i: (0, 0, 0)),
                  pl.BlockSpec((KV_NSA, n_cmp, HEAD_DIM), lambda i: (0, 0, 0))],
        out_specs=[pl.BlockSpec((BLK, qw), lambda i: (i, 0)),
                   pl.BlockSpec((KV_NSA, BLK, n_slc), lambda i: (0, i, 0)),
                   pl.BlockSpec((KV_NSA, 1, 8, n_word), lambda i: (0, i, 0, 0))],
        out_shape=[jax.ShapeDtypeStruct((s, qw), F32),
                   jax.ShapeDtypeStruct((KV_NSA, s, n_slc), BF16),
                   jax.ShapeDtypeStruct((KV_NSA, s // BLK, 8, n_word), jnp.int32)],
        scratch_shapes=[pltpu.VMEM((KV_NSA, BLK, n_cmp), F32),
                        pltpu.VMEM((H_NSA * BLK, HEAD_DIM), BF16),
                        pltpu.VMEM((H_NSA * BLK, HEAD_DIM), F32),
                        pltpu.VMEM((n_cmp, n_slc), BF16)],
        compiler_params=_params("arbitrary"),
        name="nsa_compressed",
    )(slopes, z, k_cmp, v_cmp)


def _stack_heads(q_ref, qs_ref, g_per):
    for g in range(g_per):
        qs_ref[g * BLK:(g + 1) * BLK, :] = q_ref[:, g * HEAD_DIM:(g + 1) * HEAD_DIM]


_ROW_SPLIT = 2


def _online_visit(i, blocks, qs_ref, k_ref, v_ref, slope_ref, m_ref, l_ref, acc_ref, g_per):
    nb = len(blocks)
    starts = [pl.multiple_of(j * BLK, BLK) for j, _ in blocks]
    k = jnp.concatenate([k_ref[pl.ds(st, BLK), :] for st in starts], axis=0)
    v = jnp.concatenate([v_ref[pl.ds(st, BLK), :] for st in starts], axis=0)
    v_ones = jnp.concatenate([v, jnp.ones((nb * BLK, HEAD_DIM), BF16)], axis=1)
    qq = lax.broadcasted_iota(jnp.int32, (BLK, BLK), 0) - lax.broadcasted_iota(jnp.int32, (BLK, BLK), 1)
    dists = [(i - j) * BLK + qq for j, _ in blocks]
    ok1 = jnp.concatenate([key_ok(d) for (_, key_ok), d in zip(blocks, dists)], axis=1)
    dist1 = jnp.concatenate([d.astype(F32) for d in dists], axis=1)
    g_sub = g_per // _ROW_SPLIT
    ok = jnp.concatenate([ok1] * g_sub, axis=0) > 0.5
    dist_f = jnp.concatenate([dist1] * g_sub, axis=0)

    groups = [slice(r * g_sub * BLK, (r + 1) * g_sub * BLK) for r in range(_ROW_SPLIT)]
    s = [_dot_nt(qs_ref[rows, :], k) for rows in groups]
    s = [jnp.where(ok, x * (SCALE * LOG2E) - jnp.concatenate([slope_ref[0, rows, :]] * nb, axis=1) * dist_f, NEG)
         for x, rows in zip(s, groups)]
    m_prev = [m_ref[rows, :] for rows in groups]
    m_new = [jnp.maximum(mp, jnp.max(x, axis=-1, keepdims=True)) for mp, x in zip(m_prev, s)]
    alpha = [jnp.exp2(mp - mn) for mp, mn in zip(m_prev, m_new)]
    p = [jnp.where(ok, jnp.exp2(x - jnp.concatenate([mn] * nb, axis=1)), 0.0) for x, mn in zip(s, m_new)]
    pv = [_dot(x.astype(BF16), v_ones) for x in p]
    for rows, a, y, mn in zip(groups, alpha, pv, m_new):
        l_ref[rows, :] = a * l_ref[rows, :] + y[:, HEAD_DIM:]
        acc_ref[rows, :] = a * acc_ref[rows, :] + y[:, :HEAD_DIM]
        m_ref[rows, :] = mn


def _online_init(m_ref, l_ref, acc_ref):
    m_ref[...] = jnp.full_like(m_ref, NEG)
    l_ref[...] = jnp.zeros_like(l_ref)
    acc_ref[...] = jnp.zeros_like(acc_ref)


_SEL_GROUP = 8


def _nsa_sel_kernel(need_ref, q_ref, sel_ref, slope_ref, k_ref, v_ref, o_ref,
                    qs_ref, m_ref, l_ref, acc_ref, list_ref, *, g_per, n_qblk, n_word):
    h = pl.program_id(0)
    i = pl.program_id(1)
    _stack_heads(q_ref, qs_ref, g_per)
    _online_init(m_ref, l_ref, acc_ref)
    sel = sel_ref[0]
    n_slc = sel.shape[1]
    per_blk = BLK // NSA_SEL_LEN
    eb = lax.broadcasted_iota(jnp.int32, (n_slc, BLK), 0)
    el = lax.broadcasted_iota(jnp.int32, (n_slc, BLK), 1) // NSA_SEL_LEN
    base = (h * n_qblk + i) * n_word

    def scan(j, n):
        word = need_ref[base + jnp.right_shift(j, 4)]
        needed = jnp.bitwise_and(jnp.right_shift(word, jnp.bitwise_and(j, 15)), 1)
        list_ref[n] = j
        return n + needed

    n_needed = lax.fori_loop(0, i + 1, scan, 0)

    def visit(first, count):
        blocks = []
        for u in range(count):
            j = list_ref[first + u]
            expand = jnp.where(eb == per_blk * j + el, 1.0, 0.0).astype(BF16)
            picked = _dot(sel, expand)
            blocks.append((j, functools.partial(
                lambda dist, picked: jnp.where(dist >= 0, picked, 0.0), picked=picked)))
        _online_visit(i, blocks, qs_ref, k_ref, v_ref, slope_ref, m_ref, l_ref, acc_ref, g_per)

    def group(t, carry):
        visit(t * _SEL_GROUP, _SEL_GROUP)
        return carry

    n_groups = n_needed // _SEL_GROUP
    lax.fori_loop(0, n_groups, group, 0)
    done = n_groups * _SEL_GROUP
    size = _SEL_GROUP // 2
    while size >= 1:
        has = jnp.bitwise_and(n_needed, size) != 0
        pl.when(has)(functools.partial(visit, done, size))
        done = done + jnp.where(has, size, 0)
        size //= 2
    out = acc_ref[...] / jnp.maximum(l_ref[...], 1e-30)
    for g in range(g_per):
        o_ref[:, g * HEAD_DIM:(g + 1) * HEAD_DIM] = out[g * BLK:(g + 1) * BLK, :]


def _nsa_selected(z, sel, need, slope_rows, *, k_col, v_col):
    s = z.shape[0]
    g_per = H_NSA // KV_NSA
    qw = g_per * HEAD_DIM
    rows = g_per * BLK
    n_slc = sel.shape[-1]
    n_qblk = s // BLK
    n_word = -(-n_qblk // 16)
    need = need[:, :, 0, :n_word]
    return pl.pallas_call(
        functools.partial(_nsa_sel_kernel, g_per=g_per, n_qblk=n_qblk, n_word=n_word),
        grid=(KV_NSA, n_qblk),
        in_specs=[pl.BlockSpec(memory_space=pltpu.SMEM),
                  pl.BlockSpec((BLK, qw), lambda h, i: (i, h)),
                  pl.BlockSpec((1, BLK, n_slc), lambda h, i: (h, i, 0)),
                  pl.BlockSpec((1, rows, HEAD_DIM), lambda h, i: (h, 0, 0)),
                  pl.BlockSpec((s, HEAD_DIM), lambda h, i: (0, k_col + h)),
                  pl.BlockSpec((s, HEAD_DIM), lambda h, i: (0, v_col + h))],
        out_specs=pl.BlockSpec((BLK, qw), lambda h, i: (i, h)),
        out_shape=jax.ShapeDtypeStruct((s, H_NSA * HEAD_DIM), F32),
        scratch_shapes=[pltpu.VMEM((rows, HEAD_DIM), BF16),
                        pltpu.VMEM((rows, HEAD_DIM), F32),
                        pltpu.VMEM((rows, HEAD_DIM), F32),
                        pltpu.VMEM((rows, HEAD_DIM), F32),
                        pltpu.SMEM((n_qblk + _SEL_GROUP,), jnp.int32)],
        compiler_params=_params("parallel", "arbitrary"),
        name="nsa_selected",
    )(need.reshape(-1), z, sel, slope_rows, z, z)


def _nsa_win_kernel(q_ref, slope_ref, k_ref, v_ref, gate_ref, ocmp_ref, oslc_ref, o_ref,
                    qs_ref, m_ref, l_ref, acc_ref, *, g_per):
    i = pl.program_id(1)
    _stack_heads(q_ref, qs_ref, g_per)
    _online_init(m_ref, l_ref, acc_ref)
    n_prev = -(-(NSA_WINDOW - 1) // BLK)

    blocks = []
    for step in range(n_prev + 1):
        j = i - n_prev + step
        present = jnp.where(j >= 0, 1.0, 0.0)
        blocks.append((jnp.maximum(j, 0), functools.partial(
            lambda dist, present: jnp.where((dist >= 0) & (dist < NSA_WINDOW), present, 0.0),
            present=present)))
    _online_visit(i, blocks, qs_ref, k_ref, v_ref, slope_ref, m_ref, l_ref, acc_ref, g_per)
    o_win = acc_ref[...] / jnp.maximum(l_ref[...], 1e-30)
    gates = jax.nn.sigmoid(gate_ref[...])
    for g in range(g_per):
        hs = slice(g * HEAD_DIM, (g + 1) * HEAD_DIM)
        o = (gates[:, 3 * g:3 * g + 1] * ocmp_ref[:, hs]
             + gates[:, 3 * g + 1:3 * g + 2] * oslc_ref[:, hs]
             + gates[:, 3 * g + 2:3 * g + 3] * o_win[g * BLK:(g + 1) * BLK, :])
        o_ref[:, hs] = o.astype(o_ref.dtype)


def _nsa_window_combine(z, gate_logits, o_cmp, o_slc, slope_rows, *, k_col, v_col):
    s = z.shape[0]
    g_per = H_NSA // KV_NSA
    qw = g_per * HEAD_DIM
    rows = g_per * BLK
    head_blk = lambda h, i: (i, h)
    return pl.pallas_call(
        functools.partial(_nsa_win_kernel, g_per=g_per),
        grid=(KV_NSA, s // BLK),
        in_specs=[pl.BlockSpec((BLK, qw), head_blk),
                  pl.BlockSpec((1, rows, HEAD_DIM), lambda h, i: (h, 0, 0)),
                  pl.BlockSpec((s, HEAD_DIM), lambda h, i: (0, k_col + h)),
                  pl.BlockSpec((s, HEAD_DIM), lambda h, i: (0, v_col + h)),
                  pl.BlockSpec((BLK, 128), head_blk),
                  pl.BlockSpec((BLK, qw), head_blk),
                  pl.BlockSpec((BLK, qw), head_blk)],
        out_specs=pl.BlockSpec((BLK, qw), head_blk),
        out_shape=jax.ShapeDtypeStruct((s, H_NSA * HEAD_DIM), BF16),
        scratch_shapes=[pltpu.VMEM((rows, HEAD_DIM), BF16),
                        pltpu.VMEM((rows, HEAD_DIM), F32),
                        pltpu.VMEM((rows, HEAD_DIM), F32),
                        pltpu.VMEM((rows, HEAD_DIM), F32)],
        compiler_params=_params("parallel", "arbitrary"),
        name="nsa_window_combine",
    )(z, slope_rows, z, z, gate_logits, o_cmp, o_slc)


def _ab_mixer(h, w_in, sinks, w_out, x_res):
    z = _matmul(h, w_in.astype(BF16), BF16, tm=1024, tn=768)
    d_sb = H_SB * HEAD_DIM
    o_a = _stick_breaking(z, n_heads=H_SB, q_col=0, k_col=H_SB, v_col=2 * H_SB)
    q_b = 3 * d_sb
    k_b = q_b + H_SW * HEAD_DIM
    v_b = k_b + KV_SW * HEAD_DIM
    o_b = _window_sink_attention(z, sinks, q_col=q_b, k_col=k_b, v_col=v_b)
    o = jnp.concatenate([o_a, o_b], axis=-1)
    return _matmul(o, w_out.astype(BF16), F32, tm=1024, tn=512, residual=x_res)


def _nsa_mixer(h, w_in, cmp_pos, wk1, wk2, wv1, wv2, w_out, x_res):
    s = h.shape[0]
    dq = H_NSA * HEAD_DIM
    dkv = KV_NSA * HEAD_DIM
    g_per = H_NSA // KV_NSA
    main = dq + 6 * dkv
    z = _matmul(h, w_in[:, :main].astype(BF16), BF16, tm=1024, tn=512)
    w_gate = w_in[:, main:].reshape(-1, KV_NSA, 3 * g_per)
    w_gate = jnp.pad(w_gate, ((0, 0), (0, 0), (0, 128 - 3 * g_per))).reshape(-1, KV_NSA * 128)
    gate_logits = _matmul(h, w_gate.astype(BF16), F32, tm=1024, tn=KV_NSA * 128)

    n_chunk = s // NSA_CMP_STRIDE
    to_chunks = lambda c: (z[:, c:c + dkv].reshape(n_chunk, NSA_CMP_STRIDE, KV_NSA, HEAD_DIM)
                           .transpose(2, 0, 1, 3).reshape(KV_NSA, n_chunk, NSA_CMP_STRIDE * HEAD_DIM))
    chunks = jnp.concatenate([to_chunks(dq), to_chunks(dq + dkv)], axis=0)
    kv_cmp = _compress(chunks, cmp_pos.reshape(1, -1).astype(F32),
                       jnp.stack([wk1, wv1]).astype(BF16), jnp.stack([wk2, wv2]).astype(BF16))
    k_cmp, v_cmp = kv_cmp[:KV_NSA], kv_cmp[KV_NSA:]

    slopes = _alibi_slopes(H_NSA, KV_NSA) * np.float32(LOG2E)
    slope_rows = jnp.asarray(np.repeat(np.repeat(slopes, BLK, axis=1)[..., None], HEAD_DIM, axis=2))
    o_cmp, sel, need = _nsa_compressed(z, k_cmp, v_cmp, jnp.asarray(slopes.reshape(-1)))
    col = lambda off: (dq + off * dkv) // HEAD_DIM
    o_slc = _nsa_selected(z, sel, need, slope_rows, k_col=col(2), v_col=col(3))
    o = _nsa_window_combine(z, gate_logits, o_cmp, o_slc, slope_rows, k_col=col(4), v_col=col(5))
    return _matmul(o, w_out.astype(BF16), F32, tm=1024, tn=512, residual=x_res)


def _swiglu(h, w_gate_up, w_down, layer, x_res):
    act = _gate_up(h, w_gate_up, layer, tm=2048, tn=256)
    return _matmul(act, w_down, F32, tm=512, tn=512, residual=x_res, b_index=layer)


def kernel(x, attn_norm, ffn_norm, w_gate_up, w_down, ab_w_in, ab_sinks, ab_w_out, nsa_w_in,
           nsa_cmp_pos, nsa_cmp_wk1, nsa_cmp_wk2, nsa_cmp_wv1, nsa_cmp_wv2, nsa_w_out, final_norm):
    b, s, d = x.shape
    w_down = w_down.astype(BF16)
    outs = []
    for bi in range(b):
        xb = x[bi]
        for layer in range(attn_norm.shape[0]):
            h = _rmsnorm(xb, attn_norm[layer], BF16)
            if layer % 2 == 0:
                e = layer // 2
                xb = _ab_mixer(h, ab_w_in[e], ab_sinks[e], ab_w_out[e], xb)
            else:
                o = layer // 2
                xb = _nsa_mixer(h, nsa_w_in[o], nsa_cmp_pos[o], nsa_cmp_wk1[o], nsa_cmp_wk2[o],
                                nsa_cmp_wv1[o], nsa_cmp_wv2[o], nsa_w_out[o], xb)
            h = _rmsnorm(xb, ffn_norm[layer], BF16)
            xb = _swiglu(h, w_gate_up, w_down, layer, xb)
        outs.append(_rmsnorm(xb, final_norm, F32))
    return jnp.stack(outs, axis=0)
```
